```python
import jax, jax.numpy as jnp
from jax import lax
import numpy as np

D_MODEL = 1024
BATCH = 8
SEQ = 4096
DEPTH = 1

ATT_HEAD_DIM = 64
ATT_HEADS_PER_GROUP = 8
DILATED_GROUPS = ((128, 1), (512, 4), (2048, 16))
N_ATT_GROUPS = len(DILATED_GROUPS)
ATT_WIDTH = N_ATT_GROUPS * ATT_HEADS_PER_GROUP * ATT_HEAD_DIM
ATT_OUT_WIDTH = ATT_HEADS_PER_GROUP * ATT_HEAD_DIM
BAND_BLOCK = 128
ROPE_THETA = 10000.0

RET_HEADS = 4
RET_QK_WIDTH = D_MODEL // 2
RET_V_WIDTH = D_MODEL
RET_KEY_DIM = RET_QK_WIDTH // RET_HEADS
RET_VALUE_DIM = RET_V_WIDTH // RET_HEADS
RET_CHUNK = 128

FFN_HIDDEN = ((8 * D_MODEL // 3 + 255) // 256) * 256
NORM_EPS = 1e-6

IN_SPLITS = (ATT_WIDTH, ATT_WIDTH, ATT_WIDTH,
             RET_QK_WIDTH, RET_QK_WIDTH, RET_V_WIDTH, RET_V_WIDTH,
             D_MODEL, D_MODEL)
IN_WIDTH = int(sum(IN_SPLITS))
IN_OFFSETS = tuple(int(o) for o in np.cumsum(IN_SPLITS)[:-1])

kernel_name = "hybrid_dilated_attn_retention_gated"


def rmsnorm(x, g):
    xf = x.astype(jnp.float32)
    y = xf * lax.rsqrt(jnp.mean(xf * xf, axis=-1, keepdims=True) + NORM_EPS)
    return (y * g.astype(jnp.float32)).astype(x.dtype)


def apply_rope(t, pos):
    hd = t.shape[-1]
    inv = ROPE_THETA ** (-jnp.arange(0, hd, 2, dtype=jnp.float32) / hd)
    ang = pos[:, None] * inv[None, :]
    c = jnp.cos(ang)[:, None, :]
    s = jnp.sin(ang)[:, None, :]
    tf = t.astype(jnp.float32)
    t1, t2 = tf[..., : hd // 2], tf[..., hd // 2:]
    out = jnp.concatenate([t1 * c - t2 * s, t2 * c + t1 * s], axis=-1)
    return out.astype(t.dtype)


def dilated_causal_group(q, k, v, window, dilation):
    B, S, H, hd = q.shape
    n_strides = window // dilation
    L = S // dilation
    nb = -(-L // BAND_BLOCK)
    Lp = nb * BAND_BLOCK

    def to_sub(t):
        t = t.reshape(B, L, dilation, H, hd).transpose(0, 2, 3, 1, 4)
        return jnp.pad(t, ((0, 0), (0, 0), (0, 0), (0, Lp - L), (0, 0)))

    qs, ks, vs = to_sub(q), to_sub(k), to_sub(v)
    qb = qs.reshape(B, dilation, H, nb, BAND_BLOCK, hd)

    def band(t):
        tp = jnp.pad(t, ((0, 0), (0, 0), (0, 0), (BAND_BLOCK, 0), (0, 0)))
        prev = tp[:, :, :, :Lp].reshape(B, dilation, H, nb, BAND_BLOCK, hd)
        cur = t.reshape(B, dilation, H, nb, BAND_BLOCK, hd)
        return jnp.concatenate([prev, cur], axis=4)

    kb, vb = band(ks), band(vs)
    scores = jnp.einsum('bdhnqc,bdhnkc->bdhnqk', qb, kb).astype(jnp.float32) * (hd ** -0.5)
    qi = jnp.arange(BAND_BLOCK)[:, None]
    kj = jnp.arange(2 * BAND_BLOCK)[None, :]
    dist = BAND_BLOCK + qi - kj
    key_idx = jnp.arange(nb)[:, None, None] * BAND_BLOCK + kj[None] - BAND_BLOCK
    mask = (dist >= 0)[None] & (dist <= n_strides)[None] & (key_idx >= 0)
    scores = jnp.where(mask, scores, jnp.float32(-1e30))
    m = jnp.max(scores, axis=-1, keepdims=True)
    p = jnp.exp(scores - m)
    l = jnp.sum(p, axis=-1, keepdims=True)
    o = jnp.einsum('bdhnqk,bdhnkc->bdhnqc', (p / l).astype(v.dtype), vb)
    lse = (m + jnp.log(l))[..., 0]
    o = o.reshape(B, dilation, H, Lp, hd)[:, :, :, :L].transpose(0, 3, 1, 2, 4).reshape(B, S, H, hd)
    lse = lse.reshape(B, dilation, H, Lp)[:, :, :, :L].transpose(0, 3, 1, 2).reshape(B, S, H)
    return o, lse


def dilated_attention(q, k, v, pos):
    B, S, _ = q.shape
    shp = (B, S, N_ATT_GROUPS, ATT_HEADS_PER_GROUP, ATT_HEAD_DIM)
    q = apply_rope(q.reshape(B, S, -1, ATT_HEAD_DIM), pos).reshape(shp)
    k = apply_rope(k.reshape(B, S, -1, ATT_HEAD_DIM), pos).reshape(shp)
    v = v.reshape(shp)
    outs, lses = [], []
    for g, (window, dilation) in enumerate(DILATED_GROUPS):
        o, lse = dilated_causal_group(q[:, :, g], k[:, :, g], v[:, :, g], window, dilation)
        outs.append(o)
        lses.append(lse)
    w = jax.nn.softmax(jnp.stack(lses, axis=0), axis=0)
    o = sum(w[g][..., None].astype(outs[g].dtype) * outs[g] for g in range(N_ATT_GROUPS))
    return o.reshape(B, S, ATT_OUT_WIDTH)


def retnet_theta_shift(t, pos):
    dk = t.shape[-1]
    ang_base = 1.0 / (ROPE_THETA ** jnp.linspace(0.0, 1.0, dk // 2, dtype=jnp.float32))
    ang = pos[:, None] * ang_base[None, :]
    c = jnp.cos(ang)[:, None, :]
    s = jnp.sin(ang)[:, None, :]
    t0, t1 = t[..., 0::2], t[..., 1::2]
    r0 = t0 * c - t1 * s
    r1 = t1 * c + t0 * s
    return jnp.stack([r0, r1], axis=-1).reshape(t.shape)


def retention(q, k, v, pos):
    B, S, _ = q.shape
    C = RET_CHUNK
    nc = S // C
    q = retnet_theta_shift(q.astype(jnp.float32).reshape(B, S, RET_HEADS, RET_KEY_DIM), pos)
    k = retnet_theta_shift(k.astype(jnp.float32).reshape(B, S, RET_HEADS, RET_KEY_DIM), pos)
    k = k * (RET_KEY_DIM ** -0.5)
    v = v.astype(jnp.float32).reshape(B, S, RET_HEADS, RET_VALUE_DIM)

    def chunks(t):
        return t.reshape(B, nc, C, RET_HEADS, t.shape[-1]).transpose(0, 3, 1, 2, 4)

    qc, kc, vc = chunks(q), chunks(k), chunks(v)
    log_g = jnp.log1p(-(2.0 ** (-5.0 - jnp.arange(RET_HEADS, dtype=jnp.float32))))
    idx = jnp.arange(C, dtype=jnp.float32)
    diff = idx[:, None] - idx[None, :]
    decay = jnp.where(diff[None] >= 0, jnp.exp(jnp.maximum(diff, 0.0)[None] * log_g[:, None, None]), 0.0)
    inner = jnp.einsum('bhnid,bhnjd->bhnij', qc, kc) * decay[None, :, None]
    inner = jnp.einsum('bhnij,bhnje->bhnie', inner, vc)
    zeta = jnp.exp((C - 1 - idx)[None, :] * log_g[:, None])
    xi = jnp.exp((idx + 1.0)[None, :] * log_g[:, None])
    kv = jnp.einsum('bhncd,bhnce->bhnde', kc * zeta[None, :, None, :, None], vc)
    chunk_decay = jnp.exp(C * log_g)

    def step(R, kv_c):
        return R * chunk_decay[None, :, None, None] + kv_c, R

    R0 = jnp.zeros((B, RET_HEADS, RET_KEY_DIM, RET_VALUE_DIM), jnp.float32)
    _, R_prev = lax.scan(step, R0, kv.transpose(2, 0, 1, 3, 4))
    cross = jnp.einsum('bhncd,nbhde->bhnce', qc * xi[None, :, None, :, None], R_prev)
    o = (inner + cross).transpose(0, 2, 3, 1, 4).reshape(B, S, RET_HEADS, RET_VALUE_DIM)
    mu = jnp.mean(o, axis=-1, keepdims=True)
    var = jnp.mean(jnp.square(o - mu), axis=-1, keepdims=True)
    o = (o - mu) * lax.rsqrt(var + NORM_EPS)
    return o.reshape(B, S, RET_V_WIDTH)


def setup_inputs(seed: int = 0) -> dict:
    key = jax.random.key(seed)
    ks = jax.random.split(key, 12)
    f = jnp.float32

    def w(k, shape, fan_in):
        return jax.random.normal(k, shape, f) * (fan_in ** -0.5)

    def gain(k, shape):
        return 1.0 + 0.02 * jax.random.normal(k, shape, f)

    return {
        "x": jax.random.normal(ks[0], (BATCH, SEQ, D_MODEL), f),
        "norm_mix_g": gain(ks[1], (DEPTH, D_MODEL)),
        "w_in": w(ks[2], (DEPTH, D_MODEL, IN_WIDTH), D_MODEL),
        "w_out_attn": w(ks[3], (DEPTH, ATT_OUT_WIDTH, D_MODEL), ATT_OUT_WIDTH),
        "w_out_ret": w(ks[4], (DEPTH, RET_V_WIDTH, D_MODEL), RET_V_WIDTH),
        "w_out": w(ks[5], (DEPTH, D_MODEL, D_MODEL), D_MODEL),
        "norm_ffn_g": gain(ks[6], (DEPTH, D_MODEL)),
        "w_ffn_gate": w(ks[7], (DEPTH, D_MODEL, FFN_HIDDEN), D_MODEL),
        "w_ffn_up": w(ks[8], (DEPTH, D_MODEL, FFN_HIDDEN), D_MODEL),
        "w_ffn_down": w(ks[9], (DEPTH, FFN_HIDDEN, D_MODEL), FFN_HIDDEN),
        "norm_final_g": gain(ks[10], (D_MODEL,)),
    }


def reference(x, norm_mix_g, w_in, w_out_attn, w_out_ret, w_out, norm_ffn_g,
              w_ffn_gate, w_ffn_up, w_ffn_down, norm_final_g):
    S = x.shape[1]
    pos = jnp.arange(S, dtype=jnp.float32)
    for layer in range(DEPTH):
        h = rmsnorm(x, norm_mix_g[layer])
        proj = h @ w_in[layer]
        (qa, ka, va, qr, kr, vr, gr, gate_a, gate_r) = jnp.split(proj, IN_OFFSETS, axis=-1)
        ya = dilated_attention(qa, ka, va, pos) @ w_out_attn[layer]
        yr = retention(qr, kr, vr, pos).astype(x.dtype) * jax.nn.silu(gr)
        yr = yr @ w_out_ret[layer]
        merged = jax.nn.sigmoid(gate_a) * ya + jax.nn.sigmoid(gate_r) * yr
        x = x + merged @ w_out[layer]
        h2 = rmsnorm(x, norm_ffn_g[layer])
        x = x + (jax.nn.silu(h2 @ w_ffn_gate[layer]) * (h2 @ w_ffn_up[layer])) @ w_ffn_down[layer]
    return rmsnorm(x, norm_final_g)
```

```python
import functools

import jax
import jax.numpy as jnp
from jax import lax
from jax.experimental import pallas as pl
from jax.experimental.pallas import tpu as pltpu

F32 = jnp.float32
BF16 = jnp.bfloat16

LANES = 128
V7X_VMEM_LIMIT_BYTES = 60 * 1024 * 1024

D_MODEL = 1024
HEAD_DIM = 64
HEADS_PER_GROUP = 8
DILATIONS = (1, 4, 16)
BAND = 128
GROUP_WIDTH = HEADS_PER_GROUP * HEAD_DIM
ATT_WIDTH = len(DILATIONS) * GROUP_WIDTH
PAIRS = GROUP_WIDTH // LANES
ROPE_THETA = 10000.0

RET_HEADS = 4
RET_KEY_DIM = 128
RET_VALUE_DIM = 256
RET_QK_WIDTH = RET_HEADS * RET_KEY_DIM
RET_V_WIDTH = RET_HEADS * RET_VALUE_DIM
RET_CHUNK = 128

FFN_HIDDEN = 2816
NORM_EPS = 1e-6
MASK_VALUE = -1e30

IN_WIDTH = 3 * ATT_WIDTH + 2 * RET_QK_WIDTH + 2 * RET_V_WIDTH + 2 * D_MODEL
COL_TILE = 512
ATT_TILES = 3 * ATT_WIDTH // COL_TILE
REST_SLABS = (IN_WIDTH - 3 * ATT_WIDTH) // LANES
REST_QR, REST_KR, REST_VR, REST_GR, REST_GA, REST_GRT = 0, 4, 8, 16, 24, 32

TM_IN = 256
TM_OUT = 512
FFN_CHUNK = 256


def _rms(x, g):
    ms = jnp.mean(x * x, axis=-1, keepdims=True)
    return x * lax.rsqrt(ms + NORM_EPS) * g


def _rotate(t, cos, sin_signed):
    return t * cos + pltpu.roll(t, LANES // 2, 1) * sin_signed


def _inproj_kernel(x_ref, g_ref, w_ref, ca0, sa0, ca1, sa1, ca2, sa2, cr, sr,
                   a0_ref, a1_ref, a2_ref, rest_ref, y_scr, h_scr):
    tm = x_ref.shape[0]
    y = _rms(x_ref[...], g_ref[...])
    h_scr[0] = y.astype(BF16)
    n_slabs = D_MODEL // LANES
    for k in range(n_slabs):
        y_scr[k] = y[:, k * LANES:(k + 1) * LANES]
    for gi, d in enumerate(DILATIONS):
        if d == 1:
            continue
        n = tm // d
        for r in range(d):
            for k in range(n_slabs):
                h_scr[gi, r * n:(r + 1) * n, k * LANES:(k + 1) * LANES] = (
                    y_scr[k, pl.ds(r, n, stride=d), :].astype(BF16))

    att_tabs = ((ca0, sa0), (ca1, sa1), (ca2, sa2))

    def write_att(g, slab, val):
        v = val.astype(BF16)
        d = DILATIONS[g]
        n = tm // d
        if g == 0:
            a0_ref[slab] = v
        else:
            out = a1_ref if g == 1 else a2_ref
            for r in range(d):
                out[slab, r] = v[r * n:(r + 1) * n, :]

    for c in range(ATT_TILES):
        kind, g = divmod(c, 3)
        res = jnp.dot(h_scr[g], w_ref[:, c * COL_TILE:(c + 1) * COL_TILE],
                      preferred_element_type=F32)
        for k in range(PAIRS):
            s = res[:, k * LANES:(k + 1) * LANES]
            if kind < 2:
                s = _rotate(s, att_tabs[g][0][...], att_tabs[g][1][...])
            write_att(g, kind * PAIRS + k, s)

    n_rest_tiles = REST_SLABS * LANES // COL_TILE
    for c in range(n_rest_tiles):
        col = 3 * ATT_WIDTH + c * COL_TILE
        res = jnp.dot(h_scr[0], w_ref[:, col:col + COL_TILE], preferred_element_type=F32)
        for k in range(COL_TILE // LANES):
            slab = c * (COL_TILE // LANES) + k
            s = res[:, k * LANES:(k + 1) * LANES]
            if slab < REST_VR:
                s = _rotate(s, cr[...], sr[...])
                if slab >= REST_KR:
                    s = s * (RET_KEY_DIM ** -0.5)
            rest_ref[slab] = s.astype(BF16)


def _in_projection(x2d, g, w, tabs, batch, seq):
    t = x2d.shape[0]
    tm = TM_IN
    nt = seq // tm
    grid = (t // tm,)
    tab_spec = pl.BlockSpec((tm, LANES), lambda i: (i % nt, 0))
    in_specs = [
        pl.BlockSpec((tm, D_MODEL), lambda i: (i, 0)),
        pl.BlockSpec((1, D_MODEL), lambda i: (0, 0)),
        pl.BlockSpec((D_MODEL, IN_WIDTH), lambda i: (0, 0), pipeline_mode=pl.Buffered(1)),
    ] + [tab_spec] * 8
    n_att = 3 * PAIRS
    d1, d2 = DILATIONS[1], DILATIONS[2]
    out_shape = [
        jax.ShapeDtypeStruct((n_att, t, LANES), BF16),
        jax.ShapeDtypeStruct((n_att, batch, d1, seq // d1, LANES), BF16),
        jax.ShapeDtypeStruct((n_att, batch, d2, seq // d2, LANES), BF16),
        jax.ShapeDtypeStruct((REST_SLABS, t, LANES), BF16),
    ]
    out_specs = [
        pl.BlockSpec((n_att, tm, LANES), lambda i: (0, i, 0)),
        pl.BlockSpec((n_att, None, d1, tm // d1, LANES), lambda i: (0, i // nt, 0, i % nt, 0)),
        pl.BlockSpec((n_att, None, d2, tm // d2, LANES), lambda i: (0, i // nt, 0, i % nt, 0)),
        pl.BlockSpec((REST_SLABS, tm, LANES), lambda i: (0, i, 0)),
    ]
    return pl.pallas_call(
        _inproj_kernel,
        grid=grid,
        in_specs=in_specs,
        out_specs=out_specs,
        out_shape=out_shape,
        scratch_shapes=[
            pltpu.VMEM((D_MODEL // LANES, tm, LANES), F32),
            pltpu.VMEM((len(DILATIONS), tm, D_MODEL), BF16),
        ],
        compiler_params=pltpu.CompilerParams(
            dimension_semantics=("arbitrary",), vmem_limit_bytes=V7X_VMEM_LIMIT_BYTES),
        name="in_projection",
    )(x2d, g, w, *tabs)


def _attn_block(q, kk, vv, mask, qk_first_head, v_first_head):
    zero = jnp.zeros_like(q)
    q2 = jnp.concatenate([jnp.where(qk_first_head, q, zero), jnp.where(qk_first_head, zero, q)], axis=0)
    s = lax.dot_general(q2, kk, (((1,), (1,)), ((), ())), preferred_element_type=F32)
    s = jnp.where(mask, s, MASK_VALUE)
    m = jnp.max(s, axis=1, keepdims=True)
    p = jnp.exp(s - m)
    l = jnp.sum(p, axis=1, keepdims=True)
    pv = jnp.dot(p.astype(BF16), vv, preferred_element_type=F32)
    o2 = pv * (1.0 / l)
    lse2 = m + jnp.log(l)
    nq = q.shape[0]
    o = jnp.where(v_first_head, o2[:nq], o2[nq:])
    lse = jnp.where(v_first_head, lse2[:nq], lse2[nq:])
    return o, lse


def _attn_kernel(q0, k0, v0, q1, k1, v1, q2, k2, v2, o_ref, og1, lg1, og2, lg2):
    blk = BAND
    seq = q0.shape[0]
    lane = lax.broadcasted_iota(jnp.int32, (blk, LANES), 1)
    qk_first = (lane // (HEAD_DIM // 2)) % 2 == 0
    v_first = lane < HEAD_DIM
    qi = lax.broadcasted_iota(jnp.int32, (2 * blk, 2 * blk), 0) & (blk - 1)
    kj = lax.broadcasted_iota(jnp.int32, (2 * blk, 2 * blk), 1)
    band_mask = (kj >= qi) & (kj <= qi + BAND)
    qi1 = lax.broadcasted_iota(jnp.int32, (2 * blk, blk), 0) & (blk - 1)
    kj1 = lax.broadcasted_iota(jnp.int32, (2 * blk, blk), 1)
    first_mask = kj1 <= qi1

    def first_block(qr, kr, vr):
        return _attn_block(qr[0:blk, :], kr[0:blk, :], vr[0:blk, :], first_mask, qk_first, v_first)

    def later_block(qr, kr, vr, n):
        st = pl.multiple_of(n * blk, blk)
        return _attn_block(qr[pl.ds(st, blk), :], kr[pl.ds(st - blk, 2 * blk), :],
                           vr[pl.ds(st - blk, 2 * blk), :], band_mask, qk_first, v_first)

    def scatter(og, lg, d, r, n, o, lse):
        rows = pl.ds(n * (blk * d) + r, blk, stride=d)
        og[rows, :] = o
        lg[rows, :] = lse

    def dilated_group(qr, kr, vr, og, lg, d):
        nb = seq // d // blk

        def per_sub(r, carry):
            qs, ks, vs = qr.at[r], kr.at[r], vr.at[r]
            o, lse = first_block(qs, ks, vs)
            scatter(og, lg, d, r, 0, o, lse)

            def per_block(n, c):
                o, lse = later_block(qs, ks, vs, n)
                scatter(og, lg, d, r, n, o, lse)
                return c

            return lax.fori_loop(1, nb, per_block, carry)

        lax.fori_loop(0, d, per_sub, 0)

    dilated_group(q1, k1, v1, og1, lg1, DILATIONS[1])
    dilated_group(q2, k2, v2, og2, lg2, DILATIONS[2])

    def merge(n, o0, l0):
        rows = pl.ds(n * blk if isinstance(n, int) else pl.multiple_of(n * blk, blk), blk)
        l1, l2 = lg1[rows, :], lg2[rows, :]
        mx = jnp.maximum(l0, jnp.maximum(l1, l2))
        e0, e1, e2 = jnp.exp(l0 - mx), jnp.exp(l1 - mx), jnp.exp(l2 - mx)
        num = e0 * o0 + e1 * og1[rows, :] + e2 * og2[rows, :]
        o_ref[rows, :] = (num * (1.0 / (e0 + e1 + e2))).astype(o_ref.dtype)

    o, lse = first_block(q0, k0, v0)
    merge(0, o, lse)

    def dense_block(n, c):
        o, lse = later_block(q0, k0, v0, n)
        merge(n, o, lse)
        return c

    lax.fori_loop(1, seq // blk, dense_block, 0)


def _dilated_attention(a0, a1, a2, batch, seq):
    d1, d2 = DILATIONS[1], DILATIONS[2]
    specs0 =[pl.BlockSpec((None, seq, LANES), functools.partial(lambda b, p, k: (k * PAIRS + p, b, 0), k=k))
              for k in range(3)]
    specs1 = [pl.BlockSpec((None, None, d1, seq // d1, LANES),
                           functools.partial(lambda b, p, k: (k * PAIRS + p, b, 0, 0, 0), k=k))
              for k in range(3)]
    specs2 = [pl.BlockSpec((None, None, d2, seq // d2, LANES),
                           functools.partial(lambda b, p, k: (k * PAIRS + p, b, 0, 0, 0), k=k))
              for k in range(3)]
    in_specs = specs0 + specs1 + specs2
    return pl.pallas_call(
        _attn_kernel,
        grid=(batch, PAIRS),
        in_specs=in_specs,
        out_specs=pl.BlockSpec((None, seq, LANES), lambda b, p: (p, b, 0)),
        out_shape=jax.ShapeDtypeStruct((PAIRS, batch * seq, LANES), BF16),
        scratch_shapes=[pltpu.VMEM((seq, LANES), F32)] * 4,
        compiler_params=pltpu.CompilerParams(
            dimension_semantics=("arbitrary", "arbitrary"), vmem_limit_bytes=V7X_VMEM_LIMIT_BYTES),
        name="dilated_attention",
    )(a0, a0, a0, a1, a1, a1, a2, a2, a2)


def _retention_kernel(q_ref, k_ref, v_ref, g_ref, dec_ref, zeta_ref, xi_ref, cd_ref, o_ref, state):
    c_len = RET_CHUNK
    seq = q_ref.shape[0]
    state[...] = jnp.zeros_like(state)
    dec = dec_ref[...]
    zeta = zeta_ref[...]
    xi = xi_ref[...]
    xi2 = jnp.concatenate([xi, xi], axis=1)
    cd = cd_ref[...]
    cd2 = jnp.concatenate([cd, cd], axis=1)

    def chunk(c, carry):
        rows = pl.ds(pl.multiple_of(c * c_len, c_len), c_len)
        qc = q_ref[rows, :]
        kc = k_ref[rows, :]
        vc = jnp.concatenate([v_ref[0, rows, :], v_ref[1, rows, :]], axis=1)
        a = lax.dot_general(qc, kc, (((1,), (1,)), ((), ())), preferred_element_type=F32) * dec
        r_prev = state[...]
        o = jnp.dot(a.astype(BF16), vc, preferred_element_type=F32)
        o = o + jnp.dot(qc, r_prev.astype(BF16), preferred_element_type=F32) * xi2
        kz = (kc.astype(F32) * zeta).astype(BF16)
        kv = lax.dot_general(kz, vc, (((0,), (0,)), ((), ())), preferred_element_type=F32)
        state[...] = r_prev * cd2 + kv
        mu = jnp.mean(o, axis=-1, keepdims=True)
        oc = o - mu
        var = jnp.mean(oc * oc, axis=-1, keepdims=True)
        on = oc * lax.rsqrt(var + NORM_EPS)
        gate = jnp.concatenate([g_ref[0, rows, :], g_ref[1, rows, :]], axis=1).astype(F32)
        out = on * (gate * jax.nn.sigmoid(gate))
        o_ref[0, rows, :] = out[:, :LANES].astype(o_ref.dtype)
        o_ref[1, rows, :] = out[:, LANES:].astype(o_ref.dtype)
        return carry

    lax.fori_loop(0, seq // c_len, chunk, 0)


def _retention(rest, tabs, batch, seq):
    dec, zeta, xi, cd = tabs
    vs = RET_VALUE_DIM // LANES
    tab_spec = pl.BlockSpec((None, RET_CHUNK, LANES), lambda b, h: (h, 0, 0))
    in_specs = [
        pl.BlockSpec((None, seq, LANES), lambda b, h: (REST_QR + h, b, 0)),
        pl.BlockSpec((None, seq, LANES), lambda b, h: (REST_KR + h, b, 0)),
        pl.BlockSpec((vs, seq, LANES), lambda b, h: (REST_VR // vs + h, b, 0)),
        pl.BlockSpec((vs, seq, LANES), lambda b, h: (REST_GR // vs + h, b, 0)),
        tab_spec, tab_spec, tab_spec,
        pl.BlockSpec((None, 1, LANES), lambda b, h: (h, 0, 0)),
    ]
    return pl.pallas_call(
        _retention_kernel,
        grid=(batch, RET_HEADS),
        in_specs=in_specs,
        out_specs=pl.BlockSpec((vs, seq, LANES), lambda b, h: (h, b, 0)),
        out_shape=jax.ShapeDtypeStruct((RET_V_WIDTH // LANES, batch * seq, LANES), BF16),
        scratch_shapes=[pltpu.VMEM((RET_KEY_DIM, RET_VALUE_DIM), F32)],
        compiler_params=pltpu.CompilerParams(
            dimension_semantics=("arbitrary", "arbitrary"), vmem_limit_bytes=V7X_VMEM_LIMIT_BYTES),
        name="retention",
    )(rest, rest, rest, rest, dec, zeta, xi, cd)


def _slabs(ref):
    return jnp.concatenate([ref[k] for k in range(ref.shape[0])], axis=1)


def _merge_kernel(oa_ref, yr_ref, ga_ref, gr_ref, x_ref, wa_ref, wr_ref, wo_ref, g2_ref,
                  x1_ref, h2_ref):
    ya = jnp.dot(_slabs(oa_ref), wa_ref[...], preferred_element_type=F32)
    yr = jnp.dot(_slabs(yr_ref), wr_ref[...], preferred_element_type=F32)
    merged = (jax.nn.sigmoid(_slabs(ga_ref).astype(F32)) * ya
              + jax.nn.sigmoid(_slabs(gr_ref).astype(F32)) * yr)
    x1 = x_ref[...] + jnp.dot(merged.astype(BF16), wo_ref[...], preferred_element_type=F32)
    x1_ref[...] = x1
    h2_ref[...] = _rms(x1, g2_ref[...]).astype(h2_ref.dtype)


def _merge(oa, yr, rest, x2d, wa, wr, wo, g2):
    t = x2d.shape[0]
    tm = TM_OUT
    gs = D_MODEL // LANES
    full = lambda shape: pl.BlockSpec(shape, lambda i: (0,) * len(shape), pipeline_mode=pl.Buffered(1))
    in_specs = [
        pl.BlockSpec((PAIRS, tm, LANES), lambda i: (0, i, 0)),
        pl.BlockSpec((RET_V_WIDTH // LANES, tm, LANES), lambda i: (0, i, 0)),
        pl.BlockSpec((gs, tm, LANES), lambda i: (REST_GA // gs, i, 0)),
        pl.BlockSpec((gs, tm, LANES), lambda i: (REST_GRT // gs, i, 0)),
        pl.BlockSpec((tm, D_MODEL), lambda i: (i, 0)),
        full((GROUP_WIDTH, D_MODEL)), full((RET_V_WIDTH, D_MODEL)), full((D_MODEL, D_MODEL)),
        pl.BlockSpec((1, D_MODEL), lambda i: (0, 0)),
    ]
    return pl.pallas_call(
        _merge_kernel,
        grid=(t // tm,),
        in_specs=in_specs,
        out_specs=[pl.BlockSpec((tm, D_MODEL), lambda i: (i, 0))] * 2,
        out_shape=[jax.ShapeDtypeStruct((t, D_MODEL), F32), jax.ShapeDtypeStruct((t, D_MODEL), BF16)],
        compiler_params=pltpu.CompilerParams(
            dimension_semantics=("arbitrary",), vmem_limit_bytes=V7X_VMEM_LIMIT_BYTES),
        name="branch_merge",
    )(oa, yr, rest, rest, x2d, wa, wr, wo, g2)


def _ffn_kernel(h2_ref, x1_ref, wg_ref, wu_ref, wd_ref, gf_ref, out_ref, act_scr):
    h2 = h2_ref[...]
    for c in range(0, FFN_HIDDEN, FFN_CHUNK):
        gate = jnp.dot(h2, wg_ref[:, c:c + FFN_CHUNK], preferred_element_type=F32)
        up = jnp.dot(h2, wu_ref[:, c:c + FFN_CHUNK], preferred_element_type=F32)
        act_scr[:, c:c + FFN_CHUNK] = (gate * jax.nn.sigmoid(gate) * up).astype(BF16)
    down = jnp.dot(act_scr[...], wd_ref[...], preferred_element_type=F32)
    out_ref[...] = _rms(x1_ref[...] + down, gf_ref[...])


def _ffn(h2, x1, wg, wu, wd, gf):
    t = x1.shape[0]
    tm = TM_OUT
    full = lambda shape: pl.BlockSpec(shape, lambda i: (0,) * len(shape), pipeline_mode=pl.Buffered(1))
    row = pl.BlockSpec((tm, D_MODEL), lambda i: (i, 0))
    return pl.pallas_call(
        _ffn_kernel,
        grid=(t // tm,),
        in_specs=[row, row, full((D_MODEL, FFN_HIDDEN)), full((D_MODEL, FFN_HIDDEN)),
                  full((FFN_HIDDEN, D_MODEL)), pl.BlockSpec((1, D_MODEL), lambda i: (0, 0))],
        out_specs=row,
        out_shape=jax.ShapeDtypeStruct((t, D_MODEL), F32),
        scratch_shapes=[pltpu.VMEM((tm, FFN_HIDDEN), BF16)],
        compiler_params=pltpu.CompilerParams(
            dimension_semantics=("arbitrary",), vmem_limit_bytes=V7X_VMEM_LIMIT_BYTES),
        name="swiglu_ffn",
    )(h2, x1, wg, wu, wd, gf)


def _prep_w_in(w):
    half = HEAD_DIM // 2

    def pair_layout(wq):
        wq = wq.reshape(D_MODEL, len(DILATIONS), PAIRS, 2, 2, half)
        return wq.transpose(0, 1, 2, 4, 3, 5).reshape(D_MODEL, ATT_WIDTH)

    def even_odd(wq):
        wq = wq.reshape(D_MODEL, RET_HEADS, RET_KEY_DIM // 2, 2)
        return wq.transpose(0, 1, 3, 2).reshape(D_MODEL, RET_QK_WIDTH)

    o = 0
    qa = pair_layout(w[:, o:o + ATT_WIDTH]) * (HEAD_DIM ** -0.5); o += ATT_WIDTH
    ka = pair_layout(w[:, o:o + ATT_WIDTH]); o += ATT_WIDTH
    va = w[:, o:o + ATT_WIDTH]; o += ATT_WIDTH
    qr = even_odd(w[:, o:o + RET_QK_WIDTH]); o += RET_QK_WIDTH
    kr = even_odd(w[:, o:o + RET_QK_WIDTH]); o += RET_QK_WIDTH
    rest = w[:, o:]
    return jnp.concatenate([qa, ka, va, qr, kr, rest], axis=1).astype(BF16)


def _position_tables(seq):
    pos = jnp.arange(seq, dtype=F32)
    half = HEAD_DIM // 2
    inv = ROPE_THETA ** (-jnp.arange(0, HEAD_DIM, 2, dtype=F32) / HEAD_DIM)
    ang = pos[:, None] * inv[None, :]
    c, s = jnp.cos(ang), jnp.sin(ang)
    cos_a = jnp.concatenate([c, c, c, c], axis=1)
    sin_a = jnp.concatenate([-s, -s, s, s], axis=1)
    base = 1.0 / (ROPE_THETA ** jnp.linspace(0.0, 1.0, RET_KEY_DIM // 2, dtype=F32))
    ang_r = pos[:, None] * base[None, :]
    cr, sr = jnp.cos(ang_r), jnp.sin(ang_r)
    cos_r = jnp.concatenate([cr, cr], axis=1)
    sin_r = jnp.concatenate([-sr, sr], axis=1)

    def regroup(tab, d):
        return tab.reshape(seq // TM_IN, TM_IN // d, d, LANES).transpose(0, 2, 1, 3).reshape(seq, LANES)

    tabs = []
    for d in DILATIONS:
        tabs += [regroup(cos_a, d), regroup(sin_a, d)]
    return tabs + [cos_r, sin_r]


def _retention_tables():
    c = RET_CHUNK
    log_g = jnp.log1p(-(2.0 ** (-5.0 - jnp.arange(RET_HEADS, dtype=F32))))
    idx = jnp.arange(c, dtype=F32)
    diff = idx[:, None] - idx[None, :]
    decay = jnp.where(diff[None] >= 0, jnp.exp(jnp.maximum(diff, 0.0)[None] * log_g[:, None, None]), 0.0)
    zeta = jnp.exp((c - 1 - idx)[None, :] * log_g[:, None])
    xi = jnp.exp((idx + 1.0)[None, :] * log_g[:, None])
    chunk_decay = jnp.exp(c * log_g)
    bcast = lambda v: jnp.broadcast_to(v[:, :, None], (RET_HEADS, c, LANES))
    cd = jnp.broadcast_to(chunk_decay[:, None, None], (RET_HEADS, 1, LANES))
    return decay, bcast(zeta), bcast(xi), cd


def kernel(x, norm_mix_g, w_in, w_out_attn, w_out_ret, w_out, norm_ffn_g, w_ffn_gate, w_ffn_up,
           w_ffn_down, norm_final_g):
    batch, seq, _ = x.shape
    assert w_in.shape[0] == 1, "single-layer problem"
    assert seq % (TM_IN * 1) == 0 and TM_IN % (DILATIONS[-1] * 16) == 0
    x2d = x.reshape(batch * seq, D_MODEL)
    a0, a1, a2, rest = _in_projection(
        x2d, norm_mix_g[0][None, :], _prep_w_in(w_in[0]), _position_tables(seq), batch, seq)
    oa = _dilated_attention(a0, a1, a2, batch, seq)
    yr = _retention(rest, _retention_tables(), batch, seq)
    x1, h2 = _merge(oa, yr, rest, x2d, w_out_attn[0].astype(BF16), w_out_ret[0].astype(BF16),
                    w_out[0].astype(BF16), norm_ffn_g[0][None, :])
    out = _ffn(h2, x1, w_ffn_gate[0].astype(BF16), w_ffn_up[0].astype(BF16),
               w_ffn_down[0].astype(BF16), norm_final_g[None, :])
    return out.reshape(batch, seq, D_MODEL)
```

```python
import functools

import jax
import jax.numpy as jnp
from jax import lax
from jax.experimental import pallas as pl
from jax.experimental.pallas import tpu as pltpu

F32 = jnp.float32
BF16 = jnp.bfloat16

LANES = 128
V7X_VMEM_LIMIT_BYTES = 60 * 1024 * 1024

D_MODEL = 1024
HEAD_DIM = 64
HEADS_PER_GROUP = 8
DILATIONS = (1, 4, 16)
BAND = 128
GROUP_WIDTH = HEADS_PER_GROUP * HEAD_DIM
ATT_WIDTH = len(DILATIONS) * GROUP_WIDTH
PAIRS = GROUP_WIDTH // LANES
ROPE_THETA = 10000.0

RET_HEADS = 4
RET_KEY_DIM = 128
RET_VALUE_DIM = 256
RET_QK_WIDTH = RET_HEADS * RET_KEY_DIM
RET_V_WIDTH = RET_HEADS * RET_VALUE_DIM
RET_CHUNK = 128

FFN_HIDDEN = 2816
NORM_EPS = 1e-6
MASK_VALUE = -1e30

IN_WIDTH = 3 * ATT_WIDTH + 2 * RET_QK_WIDTH + 2 * RET_V_WIDTH + 2 * D_MODEL
COL_TILE = 512
ATT_TILES = 3 * ATT_WIDTH // COL_TILE
REST_SLABS = (IN_WIDTH - 3 * ATT_WIDTH) // LANES
REST_QR, REST_KR, REST_VR, REST_GR, REST_GA, REST_GRT = 0, 4, 8, 16, 24, 32

TM_IN = 256
TM_OUT = 512
FFN_CHUNK = 256
ATT_BODY_BLOCKS = 16
LOG2_E = 1.4426950408889634
RET_UNROLL = 16


def _rms(x, g):
    ms = jnp.mean(x * x, axis=-1, keepdims=True)
    return x * lax.rsqrt(ms + NORM_EPS) * g


def _rotate(t, cos, sin_signed):
    return t * cos + pltpu.roll(t, LANES // 2, 1) * sin_signed


def _inproj_kernel(x_ref, g_ref, w_ref, ca0, sa0, ca1, sa1, ca2, sa2, cr, sr,
                   a0_ref, a1_ref, a2_ref, rest_ref, y_scr, h_scr):
    tm = x_ref.shape[0]
    y = _rms(x_ref[...], g_ref[...])
    h_scr[0] = y.astype(BF16)
    n_slabs = D_MODEL // LANES
    for k in range(n_slabs):
        y_scr[k] = y[:, k * LANES:(k + 1) * LANES]
    for gi, d in enumerate(DILATIONS):
        if d == 1:
            continue
        n = tm // d
        for r in range(d):
            for k in range(n_slabs):
                h_scr[gi, r * n:(r + 1) * n, k * LANES:(k + 1) * LANES] = (
                    y_scr[k, pl.ds(r, n, stride=d), :].astype(BF16))

    att_tabs = ((ca0, sa0), (ca1, sa1), (ca2, sa2))

    def write_att(g, slab, val):
        v = val.astype(BF16)
        d = DILATIONS[g]
        n = tm // d
        if g == 0:
            a0_ref[slab] = v
        else:
            out = a1_ref if g == 1 else a2_ref
            for r in range(d):
                out[slab, r] = v[r * n:(r + 1) * n, :]

    for c in range(ATT_TILES):
        kind, g = divmod(c, 3)
        res = jnp.dot(h_scr[g], w_ref[:, c * COL_TILE:(c + 1) * COL_TILE],
                      preferred_element_type=F32)
        for k in range(PAIRS):
            s = res[:, k * LANES:(k + 1) * LANES]
            if kind < 2:
                s = _rotate(s, att_tabs[g][0][...], att_tabs[g][1][...])
            write_att(g, kind * PAIRS + k, s)

    n_rest_tiles = REST_SLABS * LANES // COL_TILE
    for c in range(n_rest_tiles):
        col = 3 * ATT_WIDTH + c * COL_TILE
        res = jnp.dot(h_scr[0], w_ref[:, col:col + COL_TILE], preferred_element_type=F32)
        for k in range(COL_TILE // LANES):
            slab = c * (COL_TILE // LANES) + k
            s = res[:, k * LANES:(k + 1) * LANES]
            if slab < REST_VR:
                s = _rotate(s, cr[...], sr[...])
                if slab >= REST_KR:
                    s = s * (RET_KEY_DIM ** -0.5)
            rest_ref[slab] = s.astype(BF16)


def _in_projection(x2d, g, w, tabs, batch, seq):
    t = x2d.shape[0]
    tm = TM_IN
    nt = seq // tm
    grid = (t // tm,)
    tab_spec = pl.BlockSpec((tm, LANES), lambda i: (i % nt, 0))
    in_specs = [
        pl.BlockSpec((tm, D_MODEL), lambda i: (i, 0)),
        pl.BlockSpec((1, D_MODEL), lambda i: (0, 0)),
        pl.BlockSpec((D_MODEL, IN_WIDTH), lambda i: (0, 0), pipeline_mode=pl.Buffered(1)),
    ] + [tab_spec] * 8
    n_att = 3 * PAIRS
    d1, d2 = DILATIONS[1], DILATIONS[2]
    out_shape = [
        jax.ShapeDtypeStruct((n_att, t, LANES), BF16),
        jax.ShapeDtypeStruct((n_att, batch, d1, seq // d1, LANES), BF16),
        jax.ShapeDtypeStruct((n_att, batch, d2, seq // d2, LANES), BF16),
        jax.ShapeDtypeStruct((REST_SLABS, t, LANES), BF16),
    ]
    out_specs = [
        pl.BlockSpec((n_att, tm, LANES), lambda i: (0, i, 0)),
        pl.BlockSpec((n_att, None, d1, tm // d1, LANES), lambda i: (0, i // nt, 0, i % nt, 0)),
        pl.BlockSpec((n_att, None, d2, tm // d2, LANES), lambda i: (0, i // nt, 0, i % nt, 0)),
        pl.BlockSpec((REST_SLABS, tm, LANES), lambda i: (0, i, 0)),
    ]
    return pl.pallas_call(
        _inproj_kernel,
        grid=grid,
        in_specs=in_specs,
        out_specs=out_specs,
        out_shape=out_shape,
        scratch_shapes=[
            pltpu.VMEM((D_MODEL // LANES, tm, LANES), F32),
            pltpu.VMEM((len(DILATIONS), tm, D_MODEL), BF16),
        ],
        compiler_params=pltpu.CompilerParams(
            dimension_semantics=("arbitrary",), vmem_limit_bytes=V7X_VMEM_LIMIT_BYTES),
        name="in_projection",
    )(x2d, g, w, *tabs)


def _attn_block(q, kk, vv, bias, qk_first_head, v_first_head):
    nq = q.shape[0]
    zero = jnp.zeros_like(q)
    q2 = jnp.concatenate([jnp.where(qk_first_head, q, zero), jnp.where(qk_first_head, zero, q)], axis=0)
    s = lax.dot_general(q2, kk, (((1,), (1,)), ((), ())), preferred_element_type=F32) + bias
    m = jnp.max(s, axis=1, keepdims=True)
    p = jnp.exp2(s - m)
    l = jnp.sum(p, axis=1, keepdims=True)
    pv = jnp.dot(p.astype(BF16), vv, preferred_element_type=F32)
    acc = jnp.where(v_first_head, pv[:nq], pv[nq:])
    den = jnp.where(v_first_head, l[:nq], l[nq:])
    mm = jnp.where(v_first_head, m[:nq], m[nq:])
    return acc, den, mm


def _attn_kernel(q0, k0, v0, q1, k1, v1, q2, k2, v2, o_ref, acc1, den1, max1, acc2, den2, max2):
    blk = BAND
    seq = q0.shape[0]
    lane = lax.broadcasted_iota(jnp.int32, (1, LANES), 1)
    qk_first = (lane // (HEAD_DIM // 2)) % 2 == 0
    v_first = lane < HEAD_DIM
    qi = lax.broadcasted_iota(jnp.int32, (2 * blk, 2 * blk), 0) & (blk - 1)
    kj = lax.broadcasted_iota(jnp.int32, (2 * blk, 2 * blk), 1)
    band_bias = jnp.where((kj >= qi) & (kj <= qi + BAND), 0.0, MASK_VALUE).astype(F32)
    qi1 = lax.broadcasted_iota(jnp.int32, (2 * blk, blk), 0) & (blk - 1)
    kj1 = lax.broadcasted_iota(jnp.int32, (2 * blk, blk), 1)
    first_bias = jnp.where(kj1 <= qi1, 0.0, MASK_VALUE).astype(F32)

    def block(qr, kr, vr, n):
        if isinstance(n, int) and n == 0:
            return _attn_block(qr[0:blk, :], kr[0:blk, :], vr[0:blk, :], first_bias, qk_first, v_first)
        st = n * blk if isinstance(n, int) else pl.multiple_of(n * blk, blk)
        return _attn_block(qr[pl.ds(st, blk), :], kr[pl.ds(st - blk, 2 * blk), :],
                           vr[pl.ds(st - blk, 2 * blk), :], band_bias, qk_first, v_first)

    def dilated_group(qr, kr, vr, outs, d):
        nb = seq // d // blk

        def per_sub(r, carry):
            qs, ks, vs = qr.at[r], kr.at[r], vr.at[r]
            for n in range(nb):
                parts = block(qs, ks, vs, n)
                rows = pl.ds(n * (blk * d) + r, blk, stride=d)
                for ref, val in zip(outs, parts):
                    ref[rows, :] = val
            return carry

        lax.fori_loop(0, d, per_sub, 0, unroll=max(1, ATT_BODY_BLOCKS // nb))

    dilated_group(q1, k1, v1, (acc1, den1, max1), DILATIONS[1])
    dilated_group(q2, k2, v2, (acc2, den2, max2), DILATIONS[2])

    def merge(n, a0, d0, m0):
        rows = pl.ds(n * blk if isinstance(n, int) else pl.multiple_of(n * blk, blk), blk)
        m1, m2 = max1[rows, :], max2[rows, :]
        mx = jnp.maximum(m0, jnp.maximum(m1, m2))
        e0, e1, e2 = jnp.exp2(m0 - mx), jnp.exp2(m1 - mx), jnp.exp2(m2 - mx)
        num = e0 * a0 + e1 * acc1[rows, :] + e2 * acc2[rows, :]
        den = e0 * d0 + e1 * den1[rows, :] + e2 * den2[rows, :]
        o_ref[rows, :] = (num * (1.0 / den)).astype(o_ref.dtype)

    def dense_body(j, carry):
        for i in range(ATT_BODY_BLOCKS):
            n = j * ATT_BODY_BLOCKS + i
            merge(n, *block(q0, k0, v0, n))
        return carry

    dense_body(0, 0)
    lax.fori_loop(1, seq // blk // ATT_BODY_BLOCKS, dense_body, 0)


def _dilated_attention(a0, a1, a2, batch, seq):
    d1, d2 = DILATIONS[1], DILATIONS[2]
    specs0 =[pl.BlockSpec((None, seq, LANES), functools.partial(lambda b, p, k: (k * PAIRS + p, b, 0), k=k))
              for k in range(3)]
    specs1 = [pl.BlockSpec((None, None, d1, seq // d1, LANES),
                           functools.partial(lambda b, p, k: (k * PAIRS + p, b, 0, 0, 0), k=k))
              for k in range(3)]
    specs2 = [pl.BlockSpec((None, None, d2, seq // d2, LANES),
                           functools.partial(lambda b, p, k: (k * PAIRS + p, b, 0, 0, 0), k=k))
              for k in range(3)]
    in_specs = specs0 + specs1 + specs2
    return pl.pallas_call(
        _attn_kernel,
        grid=(batch, PAIRS),
        in_specs=in_specs,
        out_specs=pl.BlockSpec((None, seq, LANES), lambda b, p: (p, b, 0)),
        out_shape=jax.ShapeDtypeStruct((PAIRS, batch * seq, LANES), BF16),
        scratch_shapes=[pltpu.VMEM((seq, LANES), F32)] * 6,
        compiler_params=pltpu.CompilerParams(
            dimension_semantics=("arbitrary", "arbitrary"), vmem_limit_bytes=V7X_VMEM_LIMIT_BYTES),
        name="dilated_attention",
    )(a0, a0, a0, a1, a1, a1, a2, a2, a2)


def _retention_kernel(q_ref, k_ref, v_ref, g_ref, dec_ref, zeta_ref, xi_ref, cd_ref, o_ref, state):
    c_len = RET_CHUNK
    seq = q_ref.shape[0]
    state[...] = jnp.zeros_like(state)
    dec = dec_ref[...]
    zeta = zeta_ref[...]
    xi = xi_ref[...]
    xi2 = jnp.concatenate([xi, xi], axis=1)
    cd = cd_ref[...]
    cd2 = jnp.concatenate([cd, cd], axis=1)

    def chunk(c, carry):
        rows = pl.ds(pl.multiple_of(c * c_len, c_len), c_len)
        qc = q_ref[rows, :]
        kc = k_ref[rows, :]
        vc = jnp.concatenate([v_ref[0, rows, :], v_ref[1, rows, :]], axis=1)
        a = lax.dot_general(qc, kc, (((1,), (1,)), ((), ())), preferred_element_type=F32) * dec
        r_prev = state[...]
        o = jnp.dot(a.astype(BF16), vc, preferred_element_type=F32)
        o = o + jnp.dot(qc, r_prev.astype(BF16), preferred_element_type=F32) * xi2
        kz = (kc.astype(F32) * zeta).astype(BF16)
        kv = lax.dot_general(kz, vc, (((0,), (0,)), ((), ())), preferred_element_type=F32)
        state[...] = r_prev * cd2 + kv
        mu = jnp.mean(o, axis=-1, keepdims=True)
        oc = o - mu
        var = jnp.mean(oc * oc, axis=-1, keepdims=True)
        on = oc * lax.rsqrt(var + NORM_EPS)
        gate = jnp.concatenate([g_ref[0, rows, :], g_ref[1, rows, :]], axis=1).astype(F32)
        out = on * (gate * jax.nn.sigmoid(gate))
        o_ref[0, rows, :] = out[:, :LANES].astype(o_ref.dtype)
        o_ref[1, rows, :] = out[:, LANES:].astype(o_ref.dtype)
        return carry

    lax.fori_loop(0, seq // c_len, chunk, 0, unroll=RET_UNROLL)


def _retention(rest, tabs, batch, seq):
    dec, zeta, xi, cd = tabs
    vs = RET_VALUE_DIM // LANES
    tab_spec = pl.BlockSpec((None, RET_CHUNK, LANES), lambda b, h: (h, 0, 0))
    in_specs = [
        pl.BlockSpec((None, seq, LANES), lambda b, h: (REST_QR + h, b, 0)),
        pl.BlockSpec((None, seq, LANES), lambda b, h: (REST_KR + h, b, 0)),
        pl.BlockSpec((vs, seq, LANES), lambda b, h: (REST_VR // vs + h, b, 0)),
        pl.BlockSpec((vs, seq, LANES), lambda b, h: (REST_GR // vs + h, b, 0)),
        tab_spec, tab_spec, tab_spec,
        pl.BlockSpec((None, 1, LANES), lambda b, h: (h, 0, 0)),
    ]
    return pl.pallas_call(
        _retention_kernel,
        grid=(batch, RET_HEADS),
        in_specs=in_specs,
        out_specs=pl.BlockSpec((vs, seq, LANES), lambda b, h: (h, b, 0)),
        out_shape=jax.ShapeDtypeStruct((RET_V_WIDTH // LANES, batch * seq, LANES), BF16),
        scratch_shapes=[pltpu.VMEM((RET_KEY_DIM, RET_VALUE_DIM), F32)],
        compiler_params=pltpu.CompilerParams(
            dimension_semantics=("arbitrary", "arbitrary"), vmem_limit_bytes=V7X_VMEM_LIMIT_BYTES),
        name="retention",
    )(rest, rest, rest, rest, dec, zeta, xi, cd)


def _slabs(ref):
    return jnp.concatenate([ref[k] for k in range(ref.shape[0])], axis=1)


def _merge_kernel(oa_ref, yr_ref, ga_ref, gr_ref, x_ref, wa_ref, wr_ref, wo_ref, g2_ref,
                  x1_ref, h2_ref):
    ya = jnp.dot(_slabs(oa_ref), wa_ref[...], preferred_element_type=F32)
    yr = jnp.dot(_slabs(yr_ref), wr_ref[...], preferred_element_type=F32)
    merged = (jax.nn.sigmoid(_slabs(ga_ref).astype(F32)) * ya
              + jax.nn.sigmoid(_slabs(gr_ref).astype(F32)) * yr)
    x1 = x_ref[...] + jnp.dot(merged.astype(BF16), wo_ref[...], preferred_element_type=F32)
    x1_ref[...] = x1
    h2_ref[...] = _rms(x1, g2_ref[...]).astype(h2_ref.dtype)


def _merge(oa, yr, rest, x2d, wa, wr, wo, g2):
    t = x2d.shape[0]
    tm = TM_OUT
    gs = D_MODEL // LANES
    full = lambda shape: pl.BlockSpec(shape, lambda i: (0,) * len(shape), pipeline_mode=pl.Buffered(1))
    in_specs = [
        pl.BlockSpec((PAIRS, tm, LANES), lambda i: (0, i, 0)),
        pl.BlockSpec((RET_V_WIDTH // LANES, tm, LANES), lambda i: (0, i, 0)),
        pl.BlockSpec((gs, tm, LANES), lambda i: (REST_GA // gs, i, 0)),
        pl.BlockSpec((gs, tm, LANES), lambda i: (REST_GRT // gs, i, 0)),
        pl.BlockSpec((tm, D_MODEL), lambda i: (i, 0)),
        full((GROUP_WIDTH, D_MODEL)), full((RET_V_WIDTH, D_MODEL)), full((D_MODEL, D_MODEL)),
        pl.BlockSpec((1, D_MODEL), lambda i: (0, 0)),
    ]
    return pl.pallas_call(
        _merge_kernel,
        grid=(t // tm,),
        in_specs=in_specs,
        out_specs=[pl.BlockSpec((tm, D_MODEL), lambda i: (i, 0))] * 2,
        out_shape=[jax.ShapeDtypeStruct((t, D_MODEL), F32), jax.ShapeDtypeStruct((t, D_MODEL), BF16)],
        compiler_params=pltpu.CompilerParams(
            dimension_semantics=("arbitrary",), vmem_limit_bytes=V7X_VMEM_LIMIT_BYTES),
        name="branch_merge",
    )(oa, yr, rest, rest, x2d, wa, wr, wo, g2)


def _ffn_kernel(h2_ref, x1_ref, wg_ref, wu_ref, wd_ref, gf_ref, out_ref, act_scr):
    h2 = h2_ref[...]
    for c in range(0, FFN_HIDDEN, FFN_CHUNK):
        gate = jnp.dot(h2, wg_ref[:, c:c + FFN_CHUNK], preferred_element_type=F32)
        up = jnp.dot(h2, wu_ref[:, c:c + FFN_CHUNK], preferred_element_type=F32)
        act_scr[:, c:c + FFN_CHUNK] = (gate * jax.nn.sigmoid(gate) * up).astype(BF16)
    down = jnp.dot(act_scr[...], wd_ref[...], preferred_element_type=F32)
    out_ref[...] = _rms(x1_ref[...] + down, gf_ref[...])


def _ffn(h2, x1, wg, wu, wd, gf):
    t = x1.shape[0]
    tm = TM_OUT
    full = lambda shape: pl.BlockSpec(shape, lambda i: (0,) * len(shape), pipeline_mode=pl.Buffered(1))
    row = pl.BlockSpec((tm, D_MODEL), lambda i: (i, 0))
    return pl.pallas_call(
        _ffn_kernel,
        grid=(t // tm,),
        in_specs=[row, row, full((D_MODEL, FFN_HIDDEN)), full((D_MODEL, FFN_HIDDEN)),
                  full((FFN_HIDDEN, D_MODEL)), pl.BlockSpec((1, D_MODEL), lambda i: (0, 0))],
        out_specs=row,
        out_shape=jax.ShapeDtypeStruct((t, D_MODEL), F32),
        scratch_shapes=[pltpu.VMEM((tm, FFN_HIDDEN), BF16)],
        compiler_params=pltpu.CompilerParams(
            dimension_semantics=("arbitrary",), vmem_limit_bytes=V7X_VMEM_LIMIT_BYTES),
        name="swiglu_ffn",
    )(h2, x1, wg, wu, wd, gf)


def _prep_w_in(w):
    half = HEAD_DIM // 2

    def pair_layout(wq):
        wq = wq.reshape(D_MODEL, len(DILATIONS), PAIRS, 2, 2, half)
        return wq.transpose(0, 1, 2, 4, 3, 5).reshape(D_MODEL, ATT_WIDTH)

    def even_odd(wq):
        wq = wq.reshape(D_MODEL, RET_HEADS, RET_KEY_DIM // 2, 2)
        return wq.transpose(0, 1, 3, 2).reshape(D_MODEL, RET_QK_WIDTH)

    o = 0
    qa = pair_layout(w[:, o:o + ATT_WIDTH]) * (HEAD_DIM ** -0.5 * LOG2_E); o += ATT_WIDTH
    ka = pair_layout(w[:, o:o + ATT_WIDTH]); o += ATT_WIDTH
    va = w[:, o:o + ATT_WIDTH]; o += ATT_WIDTH
    qr = even_odd(w[:, o:o + RET_QK_WIDTH]); o += RET_QK_WIDTH
    kr = even_odd(w[:, o:o + RET_QK_WIDTH]); o += RET_QK_WIDTH
    rest = w[:, o:]
    return jnp.concatenate([qa, ka, va, qr, kr, rest], axis=1).astype(BF16)


def _position_tables(seq):
    pos = jnp.arange(seq, dtype=F32)
    half = HEAD_DIM // 2
    inv = ROPE_THETA ** (-jnp.arange(0, HEAD_DIM, 2, dtype=F32) / HEAD_DIM)
    ang = pos[:, None] * inv[None, :]
    c, s = jnp.cos(ang), jnp.sin(ang)
    cos_a = jnp.concatenate([c, c, c, c], axis=1)
    sin_a = jnp.concatenate([-s, -s, s, s], axis=1)
    base = 1.0 / (ROPE_THETA ** jnp.linspace(0.0, 1.0, RET_KEY_DIM // 2, dtype=F32))
    ang_r = pos[:, None] * base[None, :]
    cr, sr = jnp.cos(ang_r), jnp.sin(ang_r)
    cos_r = jnp.concatenate([cr, cr], axis=1)
    sin_r = jnp.concatenate([-sr, sr], axis=1)

    def regroup(tab, d):
        return tab.reshape(seq // TM_IN, TM_IN // d, d, LANES).transpose(0, 2, 1, 3).reshape(seq, LANES)

    tabs = []
    for d in DILATIONS:
        tabs += [regroup(cos_a, d), regroup(sin_a, d)]
    return tabs + [cos_r, sin_r]


def _retention_tables():
    c = RET_CHUNK
    log_g = jnp.log1p(-(2.0 ** (-5.0 - jnp.arange(RET_HEADS, dtype=F32))))
    idx = jnp.arange(c, dtype=F32)
    diff = idx[:, None] - idx[None, :]
    decay = jnp.where(diff[None] >= 0, jnp.exp(jnp.maximum(diff, 0.0)[None] * log_g[:, None, None]), 0.0)
    zeta = jnp.exp((c - 1 - idx)[None, :] * log_g[:, None])
    xi = jnp.exp((idx + 1.0)[None, :] * log_g[:, None])
    chunk_decay = jnp.exp(c * log_g)
    bcast = lambda v: jnp.broadcast_to(v[:, :, None], (RET_HEADS, c, LANES))
    cd = jnp.broadcast_to(chunk_decay[:, None, None], (RET_HEADS, 1, LANES))
    return decay, bcast(zeta), bcast(xi), cd


def kernel(x, norm_mix_g, w_in, w_out_attn, w_out_ret, w_out, norm_ffn_g, w_ffn_gate, w_ffn_up,
           w_ffn_down, norm_final_g):
    batch, seq, _ = x.shape
    assert w_in.shape[0] == 1, "single-layer problem"
    assert seq % (TM_IN * 1) == 0 and TM_IN % (DILATIONS[-1] * 16) == 0
    x2d = x.reshape(batch * seq, D_MODEL)
    a0, a1, a2, rest = _in_projection(
        x2d, norm_mix_g[0][None, :], _prep_w_in(w_in[0]), _position_tables(seq), batch, seq)
    oa = _dilated_attention(a0, a1, a2, batch, seq)
    yr = _retention(rest, _retention_tables(), batch, seq)
    x1, h2 = _merge(oa, yr, rest, x2d, w_out_attn[0].astype(BF16), w_out_ret[0].astype(BF16),
                    w_out[0].astype(BF16), norm_ffn_g[0][None, :])
    out = _ffn(h2, x1, w_ffn_gate[0].astype(BF16), w_ffn_up[0].astype(BF16),
               w_ffn_down[0].astype(BF16), norm_final_g[None, :])
    return out.reshape(batch, seq, D_MODEL)
```

```python
import functools

import jax
import jax.numpy as jnp
import numpy as np
from jax import lax
from jax.experimental import pallas as pl
from jax.experimental.pallas import tpu as pltpu

F32 = jnp.float32
BF16 = jnp.bfloat16

LANES = 128
V7X_VMEM_LIMIT_BYTES = 60 * 1024 * 1024

D_MODEL = 1024
HEAD_DIM = 64
HEADS_PER_GROUP = 8
DILATIONS = (1, 4, 16)
BAND = 128
GROUP_WIDTH = HEADS_PER_GROUP * HEAD_DIM
ATT_WIDTH = len(DILATIONS) * GROUP_WIDTH
PAIRS = GROUP_WIDTH // LANES
ROPE_THETA = 10000.0

RET_HEADS = 4
RET_KEY_DIM = 128
RET_VALUE_DIM = 256
RET_QK_WIDTH = RET_HEADS * RET_KEY_DIM
RET_V_WIDTH = RET_HEADS * RET_VALUE_DIM
RET_CHUNK = 128

FFN_HIDDEN = 2816
NORM_EPS = 1e-6
MASK_VALUE = -1e30

IN_WIDTH = 3 * ATT_WIDTH + 2 * RET_QK_WIDTH + 2 * RET_V_WIDTH + 2 * D_MODEL
COL_TILE = 512
ATT_TILES = 3 * ATT_WIDTH // COL_TILE
REST_SLABS = (IN_WIDTH - 3 * ATT_WIDTH) // LANES
REST_QR, REST_KR, REST_VR, REST_GR, REST_GA, REST_GRT = 0, 4, 8, 16, 24, 32

TM_IN = 512
TM_OUT = 512
FFN_CHUNK = 256
ATT_BODY_BLOCKS = 16
LOG2_E = 1.4426950408889634
RET_UNROLL = 16


def _rms(x, g):
    ms = jnp.mean(x * x, axis=-1, keepdims=True)
    return x * lax.rsqrt(ms + NORM_EPS) * g


def _rotate(t, cos, sin_signed):
    return t * cos + pltpu.roll(t, LANES // 2, 1) * sin_signed


def _inproj_kernel(x_ref, g_ref, w_ref, ca, sa, cr, sr,
                   a0_ref, a1_ref, a2_ref, rest_ref, y_scr, h_scr, tab_scr):
    tm = x_ref.shape[0]
    n_slabs = D_MODEL // LANES

    def stage_rows(x_ref, dst):
        y = _rms(x_ref[...], g_ref[...])
        dst[0] = y.astype(BF16)
        for k in range(n_slabs):
            y_scr[k] = y[:, k * LANES:(k + 1) * LANES]
        for gi, d in enumerate(DILATIONS):
            if d == 1:
                continue
            n = tm // d
            for r in range(d):
                for k in range(n_slabs):
                    dst[gi, r * n:(r + 1) * n, k * LANES:(k + 1) * LANES] = (
                        y_scr[k, pl.ds(r, n, stride=d), :].astype(BF16))

    stage_rows(x_ref, h_scr)

    for gi, d in enumerate(DILATIONS):
        if d == 1:
            continue
        n = tm // d
        for r in range(d):
            tab_scr[gi - 1, 0, r * n:(r + 1) * n, :] = ca[pl.ds(r, n, stride=d), :]
            tab_scr[gi - 1, 1, r * n:(r + 1) * n, :] = sa[pl.ds(r, n, stride=d), :]

    def att_tables(g):
        if g == 0:
            return ca[...], sa[...]
        return tab_scr[g - 1, 0], tab_scr[g - 1, 1]

    def write_att(g, slab, val):
        v = val.astype(BF16)
        d = DILATIONS[g]
        n = tm // d
        if g == 0:
            a0_ref[slab] = v
        else:
            out = a1_ref if g == 1 else a2_ref
            for r in range(d):
                out[slab, r] = v[r * n:(r + 1) * n, :]

    def att_tile(c):
        kind, g = divmod(c, 3)
        res = jnp.dot(h_scr[g], w_ref[:, c * COL_TILE:(c + 1) * COL_TILE],
                      preferred_element_type=F32)
        for k in range(PAIRS):
            s = res[:, k * LANES:(k + 1) * LANES]
            if kind < 2:
                s = _rotate(s, *att_tables(g))
            write_att(g, kind * PAIRS + k, s)

    def rest_tile(c):
        col = 3 * ATT_WIDTH + c * COL_TILE
        res = jnp.dot(h_scr[0], w_ref[:, col:col + COL_TILE], preferred_element_type=F32)
        for k in range(COL_TILE // LANES):
            slab = c * (COL_TILE // LANES) + k
            s = res[:, k * LANES:(k + 1) * LANES]
            if slab < REST_VR:
                s = _rotate(s, cr[...], sr[...])
                if slab >= REST_KR:
                    s = s * (RET_KEY_DIM ** -0.5)
            rest_ref[slab] = s.astype(BF16)

    for c in range(REST_SLABS * LANES // COL_TILE):
        rest_tile(c)
    for g in range(len(DILATIONS)):
        for kind in range(3):
            att_tile(kind * 3 + g)


def _in_projection(x2d, g, w, tabs, batch, seq):
    t = x2d.shape[0]
    tm = TM_IN
    nt = seq // tm
    n_steps = t // tm
    tab_spec = pl.BlockSpec((tm, LANES), lambda i: (i % nt, 0))
    in_specs = [
        pl.BlockSpec((tm, D_MODEL), lambda i: (i, 0)),
        pl.BlockSpec((1, D_MODEL), lambda i: (0, 0)),
        pl.BlockSpec((D_MODEL, IN_WIDTH), lambda i: (0, 0), pipeline_mode=pl.Buffered(1)),
    ] + [tab_spec] * 4
    n_att = 3 * PAIRS
    d1, d2 = DILATIONS[1], DILATIONS[2]
    out_shape = [
        jax.ShapeDtypeStruct((n_att, t, LANES), BF16),
        jax.ShapeDtypeStruct((n_att, batch, d1, seq // d1, LANES), BF16),
        jax.ShapeDtypeStruct((n_att, batch, d2, seq // d2, LANES), BF16),
        jax.ShapeDtypeStruct((REST_SLABS, t, LANES), BF16),
    ]
    out_specs = [
        pl.BlockSpec((n_att, tm, LANES), lambda i: (0, i, 0)),
        pl.BlockSpec((n_att, None, d1, tm // d1, LANES), lambda i: (0, i // nt, 0, i % nt, 0)),
        pl.BlockSpec((n_att, None, d2, tm // d2, LANES), lambda i: (0, i // nt, 0, i % nt, 0)),
        pl.BlockSpec((REST_SLABS, tm, LANES), lambda i: (0, i, 0)),
    ]
    return pl.pallas_call(
        _inproj_kernel,
        grid=(n_steps,),
        in_specs=in_specs,
        out_specs=out_specs,
        out_shape=out_shape,
        scratch_shapes=[
            pltpu.VMEM((D_MODEL // LANES, tm, LANES), F32),
            pltpu.VMEM((len(DILATIONS), tm, D_MODEL), BF16),
            pltpu.VMEM((len(DILATIONS) - 1, 2, tm, LANES), F32),
        ],
        compiler_params=pltpu.CompilerParams(
            dimension_semantics=("arbitrary",), vmem_limit_bytes=V7X_VMEM_LIMIT_BYTES),
        name="in_projection",
    )(x2d, g, w, *tabs)


def _attn_block(q, kk, vv, bias, qk_first_head, v_first_head):
    nq = q.shape[0]
    zero = jnp.zeros_like(q)
    q2 = jnp.concatenate([jnp.where(qk_first_head, q, zero), jnp.where(qk_first_head, zero, q)], axis=0)
    s = lax.dot_general(q2, kk, (((1,), (1,)), ((), ())), preferred_element_type=F32) + bias
    m = jnp.max(s, axis=1, keepdims=True)
    p = jnp.exp2(s - m)
    l = jnp.sum(p, axis=1, keepdims=True)
    pv = jnp.dot(p.astype(BF16), vv, preferred_element_type=F32)
    acc = jnp.where(v_first_head, pv[:nq], pv[nq:])
    den = jnp.where(v_first_head, l[:nq], l[nq:])
    mm = jnp.where(v_first_head, m[:nq], m[nq:])
    return acc, den, mm


def _attn_kernel(q0, k0, v0, q1, k1, v1, q2, k2, v2, o_ref, acc1, den1, max1, acc2, den2, max2):
    blk = BAND
    seq = q0.shape[0]
    lane = lax.broadcasted_iota(jnp.int32, (1, LANES), 1)
    qk_first = (lane // (HEAD_DIM // 2)) % 2 == 0
    v_first = lane < HEAD_DIM
    qi = lax.broadcasted_iota(jnp.int32, (2 * blk, 2 * blk), 0) & (blk - 1)
    kj = lax.broadcasted_iota(jnp.int32, (2 * blk, 2 * blk), 1)
    band_bias = jnp.where((kj >= qi) & (kj <= qi + BAND), 0.0, MASK_VALUE).astype(F32)
    qi1 = lax.broadcasted_iota(jnp.int32, (2 * blk, blk), 0) & (blk - 1)
    kj1 = lax.broadcasted_iota(jnp.int32, (2 * blk, blk), 1)
    first_bias = jnp.where(kj1 <= qi1, 0.0, MASK_VALUE).astype(F32)

    def block(qr, kr, vr, n):
        if isinstance(n, int) and n == 0:
            return _attn_block(qr[0:blk, :], kr[0:blk, :], vr[0:blk, :], first_bias, qk_first, v_first)
        st = n * blk if isinstance(n, int) else pl.multiple_of(n * blk, blk)
        return _attn_block(qr[pl.ds(st, blk), :], kr[pl.ds(st - blk, 2 * blk), :],
                           vr[pl.ds(st - blk, 2 * blk), :], band_bias, qk_first, v_first)

    def dilated_group(qr, kr, vr, outs, d):
        nb = seq // d // blk

        def per_sub(r, carry):
            qs, ks, vs = qr.at[r], kr.at[r], vr.at[r]
            for n in range(nb):
                parts = block(qs, ks, vs, n)
                rows = pl.ds(n * (blk * d) + r, blk, stride=d)
                for ref, val in zip(outs, parts):
                    ref[rows, :] = val
            return carry

        lax.fori_loop(0, d, per_sub, 0, unroll=max(1, ATT_BODY_BLOCKS // nb))

    dilated_group(q1, k1, v1, (acc1, den1, max1), DILATIONS[1])
    dilated_group(q2, k2, v2, (acc2, den2, max2), DILATIONS[2])

    def merge(n, a0, d0, m0):
        rows = pl.ds(n * blk if isinstance(n, int) else pl.multiple_of(n * blk, blk), blk)
        m1, m2 = max1[rows, :], max2[rows, :]
        mx = jnp.maximum(m0, jnp.maximum(m1, m2))
        e0, e1, e2 = jnp.exp2(m0 - mx), jnp.exp2(m1 - mx), jnp.exp2(m2 - mx)
        num = e0 * a0 + e1 * acc1[rows, :] + e2 * acc2[rows, :]
        den = e0 * d0 + e1 * den1[rows, :] + e2 * den2[rows, :]
        o_ref[rows, :] = (num * (1.0 / den)).astype(o_ref.dtype)

    def dense_body(j, carry):
        for i in range(ATT_BODY_BLOCKS):
            n = j * ATT_BODY_BLOCKS + i
            merge(n, *block(q0, k0, v0, n))
        return carry

    dense_body(0, 0)
    lax.fori_loop(1, seq // blk // ATT_BODY_BLOCKS, dense_body, 0)


def _dilated_attention(a0, a1, a2, batch, seq):
    d1, d2 = DILATIONS[1], DILATIONS[2]
    specs0 =[pl.BlockSpec((None, seq, LANES), functools.partial(lambda b, p, k: (k * PAIRS + p, b, 0), k=k))
              for k in range(3)]
    specs1 = [pl.BlockSpec((None, None, d1, seq // d1, LANES),
                           functools.partial(lambda b, p, k: (k * PAIRS + p, b, 0, 0, 0), k=k))
              for k in range(3)]
    specs2 = [pl.BlockSpec((None, None, d2, seq // d2, LANES),
                           functools.partial(lambda b, p, k: (k * PAIRS + p, b, 0, 0, 0), k=k))
              for k in range(3)]
    in_specs = specs0 + specs1 + specs2
    return pl.pallas_call(
        _attn_kernel,
        grid=(batch, PAIRS),
        in_specs=in_specs,
        out_specs=pl.BlockSpec((None, seq, LANES), lambda b, p: (p, b, 0)),
        out_shape=jax.ShapeDtypeStruct((PAIRS, batch * seq, LANES), BF16),
        scratch_shapes=[pltpu.VMEM((seq, LANES), F32)] * 6,
        compiler_params=pltpu.CompilerParams(
            dimension_semantics=("arbitrary", "arbitrary"), vmem_limit_bytes=V7X_VMEM_LIMIT_BYTES),
        name="dilated_attention",
    )(a0, a0, a0, a1, a1, a1, a2, a2, a2)


def _retention_kernel(q_ref, k_ref, v_ref, g_ref, dec_ref, zeta_ref, xi_ref, cd_ref, o_ref, state):
    c_len = RET_CHUNK
    seq = q_ref.shape[0]
    state[...] = jnp.zeros_like(state)
    dec = dec_ref[...]
    zeta = zeta_ref[...]
    xi = xi_ref[...]
    xi2 = jnp.concatenate([xi, xi], axis=1)
    cd = cd_ref[...]
    cd2 = jnp.concatenate([cd, cd], axis=1)

    def chunk(c, carry):
        rows = pl.ds(pl.multiple_of(c * c_len, c_len), c_len)
        qc = q_ref[rows, :]
        kc = k_ref[rows, :]
        vc = jnp.concatenate([v_ref[0, rows, :], v_ref[1, rows, :]], axis=1)
        a = lax.dot_general(qc, kc, (((1,), (1,)), ((), ())), preferred_element_type=F32) * dec
        r_prev = state[...]
        o = jnp.dot(a.astype(BF16), vc, preferred_element_type=F32)
        o = o + jnp.dot(qc, r_prev.astype(BF16), preferred_element_type=F32) * xi2
        kz = (kc.astype(F32) * zeta).astype(BF16)
        kv = lax.dot_general(kz, vc, (((0,), (0,)), ((), ())), preferred_element_type=F32)
        state[...] = r_prev * cd2 + kv
        mu = jnp.mean(o, axis=-1, keepdims=True)
        oc = o - mu
        var = jnp.mean(oc * oc, axis=-1, keepdims=True)
        on = oc * lax.rsqrt(var + NORM_EPS)
        gate = jnp.concatenate([g_ref[0, rows, :], g_ref[1, rows, :]], axis=1).astype(F32)
        out = on * (gate * jax.nn.sigmoid(gate))
        o_ref[0, rows, :] = out[:, :LANES].astype(o_ref.dtype)
        o_ref[1, rows, :] = out[:, LANES:].astype(o_ref.dtype)
        return carry

    lax.fori_loop(0, seq // c_len, chunk, 0, unroll=RET_UNROLL)


def _retention(rest, tabs, batch, seq):
    dec, zeta, xi, cd = tabs
    vs = RET_VALUE_DIM // LANES
    tab_spec = pl.BlockSpec((None, RET_CHUNK, LANES), lambda b, h: (h, 0, 0))
    in_specs = [
        pl.BlockSpec((None, seq, LANES), lambda b, h: (REST_QR + h, b, 0)),
        pl.BlockSpec((None, seq, LANES), lambda b, h: (REST_KR + h, b, 0)),
        pl.BlockSpec((vs, seq, LANES), lambda b, h: (REST_VR // vs + h, b, 0)),
        pl.BlockSpec((vs, seq, LANES), lambda b, h: (REST_GR // vs + h, b, 0)),
        tab_spec, tab_spec, tab_spec,
        pl.BlockSpec((None, 1, LANES), lambda b, h: (h, 0, 0)),
    ]
    return pl.pallas_call(
        _retention_kernel,
        grid=(batch, RET_HEADS),
        in_specs=in_specs,
        out_specs=pl.BlockSpec((vs, seq, LANES), lambda b, h: (h, b, 0)),
        out_shape=jax.ShapeDtypeStruct((RET_V_WIDTH // LANES, batch * seq, LANES), BF16),
        scratch_shapes=[pltpu.VMEM((RET_KEY_DIM, RET_VALUE_DIM), F32)],
        compiler_params=pltpu.CompilerParams(
            dimension_semantics=("arbitrary", "arbitrary"), vmem_limit_bytes=V7X_VMEM_LIMIT_BYTES),
        name="retention",
    )(rest, rest, rest, rest, dec, zeta, xi, cd)


def _slabs(ref):
    return jnp.concatenate([ref[k] for k in range(ref.shape[0])], axis=1)


def _merge_kernel(oa_ref, yr_ref, ga_ref, gr_ref, x_ref, wa_ref, wr_ref, wo_ref, g2_ref,
                  x1_ref, h2_ref):
    ya = jnp.dot(_slabs(oa_ref), wa_ref[...], preferred_element_type=F32)
    yr = jnp.dot(_slabs(yr_ref), wr_ref[...], preferred_element_type=F32)
    merged = (jax.nn.sigmoid(_slabs(ga_ref).astype(F32)) * ya
              + jax.nn.sigmoid(_slabs(gr_ref).astype(F32)) * yr)
    x1 = x_ref[...] + jnp.dot(merged.astype(BF16), wo_ref[...], preferred_element_type=F32)
    x1_ref[...] = x1
    h2_ref[...] = _rms(x1, g2_ref[...]).astype(h2_ref.dtype)


def _merge(oa, yr, rest, x2d, wa, wr, wo, g2):
    t = x2d.shape[0]
    tm = TM_OUT
    gs = D_MODEL // LANES
    full = lambda shape: pl.BlockSpec(shape, lambda i: (0,) * len(shape), pipeline_mode=pl.Buffered(1))
    in_specs = [
        pl.BlockSpec((PAIRS, tm, LANES), lambda i: (0, i, 0)),
        pl.BlockSpec((RET_V_WIDTH // LANES, tm, LANES), lambda i: (0, i, 0)),
        pl.BlockSpec((gs, tm, LANES), lambda i: (REST_GA // gs, i, 0)),
        pl.BlockSpec((gs, tm, LANES), lambda i: (REST_GRT // gs, i, 0)),
        pl.BlockSpec((tm, D_MODEL), lambda i: (i, 0)),
        full((GROUP_WIDTH, D_MODEL)), full((RET_V_WIDTH, D_MODEL)), full((D_MODEL, D_MODEL)),
        pl.BlockSpec((1, D_MODEL), lambda i: (0, 0)),
    ]
    return pl.pallas_call(
        _merge_kernel,
        grid=(t // tm,),
        in_specs=in_specs,
        out_specs=[pl.BlockSpec((tm, D_MODEL), lambda i: (i, 0))] * 2,
        out_shape=[jax.ShapeDtypeStruct((t, D_MODEL), F32), jax.ShapeDtypeStruct((t, D_MODEL), BF16)],
        compiler_params=pltpu.CompilerParams(
            dimension_semantics=("arbitrary",), vmem_limit_bytes=V7X_VMEM_LIMIT_BYTES),
        name="branch_merge",
    )(oa, yr, rest, rest, x2d, wa, wr, wo, g2)


def _ffn_kernel(h2_ref, x1_ref, wg_ref, wu_ref, wd_ref, gf_ref, out_ref, act_scr):
    h2 = h2_ref[...]
    for c in range(0, FFN_HIDDEN, FFN_CHUNK):
        gate = jnp.dot(h2, wg_ref[:, c:c + FFN_CHUNK], preferred_element_type=F32)
        up = jnp.dot(h2, wu_ref[:, c:c + FFN_CHUNK], preferred_element_type=F32)
        act_scr[:, c:c + FFN_CHUNK] = (gate * jax.nn.sigmoid(gate) * up).astype(BF16)
    down = jnp.dot(act_scr[...], wd_ref[...], preferred_element_type=F32)
    out_ref[...] = _rms(x1_ref[...] + down, gf_ref[...])


def _ffn(h2, x1, wg, wu, wd, gf):
    t = x1.shape[0]
    tm = TM_OUT
    full = lambda shape: pl.BlockSpec(shape, lambda i: (0,) * len(shape), pipeline_mode=pl.Buffered(1))
    row = pl.BlockSpec((tm, D_MODEL), lambda i: (i, 0))
    return pl.pallas_call(
        _ffn_kernel,
        grid=(t // tm,),
        in_specs=[row, row, full((D_MODEL, FFN_HIDDEN)), full((D_MODEL, FFN_HIDDEN)),
                  full((FFN_HIDDEN, D_MODEL)), pl.BlockSpec((1, D_MODEL), lambda i: (0, 0))],
        out_specs=row,
        out_shape=jax.ShapeDtypeStruct((t, D_MODEL), F32),
        scratch_shapes=[pltpu.VMEM((tm, FFN_HIDDEN), BF16)],
        compiler_params=pltpu.CompilerParams(
            dimension_semantics=("arbitrary",), vmem_limit_bytes=V7X_VMEM_LIMIT_BYTES),
        name="swiglu_ffn",
    )(h2, x1, wg, wu, wd, gf)


def _prep_w_in(w):
    half = HEAD_DIM // 2

    def pair_layout(wq):
        wq = wq.reshape(D_MODEL, len(DILATIONS), PAIRS, 2, 2, half)
        return wq.transpose(0, 1, 2, 4, 3, 5).reshape(D_MODEL, ATT_WIDTH)

    def even_odd(wq):
        wq = wq.reshape(D_MODEL, RET_HEADS, RET_KEY_DIM // 2, 2)
        return wq.transpose(0, 1, 3, 2).reshape(D_MODEL, RET_QK_WIDTH)

    o = 0
    qa = pair_layout(w[:, o:o + ATT_WIDTH]) * (HEAD_DIM ** -0.5 * LOG2_E); o += ATT_WIDTH
    ka = pair_layout(w[:, o:o + ATT_WIDTH]); o += ATT_WIDTH
    va = w[:, o:o + ATT_WIDTH]; o += ATT_WIDTH
    qr = even_odd(w[:, o:o + RET_QK_WIDTH]); o += RET_QK_WIDTH
    kr = even_odd(w[:, o:o + RET_QK_WIDTH]); o += RET_QK_WIDTH
    rest = w[:, o:]
    return jnp.concatenate([qa, ka, va, qr, kr, rest], axis=1).astype(BF16)


def _position_tables(seq):
    pos = np.arange(seq, dtype=np.float64)
    inv = ROPE_THETA ** (-np.arange(0, HEAD_DIM, 2, dtype=np.float64) / HEAD_DIM)
    ang = pos[:, None] * inv[None, :]
    c, s = np.cos(ang), np.sin(ang)
    cos_a = np.concatenate([c, c, c, c], axis=1)
    sin_a = np.concatenate([-s, -s, s, s], axis=1)
    base = 1.0 / (ROPE_THETA ** np.linspace(0.0, 1.0, RET_KEY_DIM // 2, dtype=np.float64))
    ang_r = pos[:, None] * base[None, :]
    cr, sr = np.cos(ang_r), np.sin(ang_r)
    cos_r = np.concatenate([cr, cr], axis=1)
    sin_r = np.concatenate([-sr, sr], axis=1)
    return [jnp.asarray(t.astype(np.float32)) for t in (cos_a, sin_a, cos_r, sin_r)]


def _retention_tables():
    c = RET_CHUNK
    log_g = np.log1p(-(2.0 ** (-5.0 - np.arange(RET_HEADS, dtype=np.float64))))
    idx = np.arange(c, dtype=np.float64)
    diff = idx[:, None] - idx[None, :]
    decay = np.where(diff[None] >= 0, np.exp(np.maximum(diff, 0.0)[None] * log_g[:, None, None]), 0.0)
    zeta = np.exp((c - 1 - idx)[None, :] * log_g[:, None])
    xi = np.exp((idx + 1.0)[None, :] * log_g[:, None])
    chunk_decay = np.exp(c * log_g)
    bcast = lambda v: np.broadcast_to(v[:, :, None], (RET_HEADS, c, LANES))
    cd = np.broadcast_to(chunk_decay[:, None, None], (RET_HEADS, 1, LANES))
    return tuple(jnp.asarray(np.ascontiguousarray(t).astype(np.float32))
                 for t in (decay, bcast(zeta), bcast(xi), cd))


def kernel(x, norm_mix_g, w_in, w_out_attn, w_out_ret, w_out, norm_ffn_g, w_ffn_gate, w_ffn_up,
           w_ffn_down, norm_final_g):
    batch, seq, _ = x.shape
    assert w_in.shape[0] == 1, "single-layer problem"
    assert seq % (TM_IN * 1) == 0 and TM_IN % (DILATIONS[-1] * 16) == 0
    x2d = x.reshape(batch * seq, D_MODEL)
    a0, a1, a2, rest = _in_projection(
        x2d, norm_mix_g[0][None, :], _prep_w_in(w_in[0]), _position_tables(seq), batch, seq)
    oa = _dilated_attention(a0, a1, a2, batch, seq)
    yr = _retention(rest, _retention_tables(), batch, seq)
    x1, h2 = _merge(oa, yr, rest, x2d, w_out_attn[0].astype(BF16), w_out_ret[0].astype(BF16),
                    w_out[0].astype(BF16), norm_ffn_g[0][None, :])
    out = _ffn(h2, x1, w_ffn_gate[0].astype(BF16), w_ffn_up[0].astype(BF16),
               w_ffn_down[0].astype(BF16), norm_final_g[None, :])
    return out.reshape(batch, seq, D_MODEL)
```

```python
import functools

import jax
import jax.numpy as jnp
import numpy as np
from jax import lax
from jax.experimental import pallas as pl
from jax.experimental.pallas import tpu as pltpu

F32 = jnp.float32
BF16 = jnp.bfloat16

LANES = 128
V7X_VMEM_LIMIT_BYTES = 60 * 1024 * 1024

D_MODEL = 1024
HEAD_DIM = 64
HEADS_PER_GROUP = 8
DILATIONS = (1, 4, 16)
BAND = 128
GROUP_WIDTH = HEADS_PER_GROUP * HEAD_DIM
ATT_WIDTH = len(DILATIONS) * GROUP_WIDTH
PAIRS = GROUP_WIDTH // LANES
ROPE_THETA = 10000.0

RET_HEADS = 4
RET_KEY_DIM = 128
RET_VALUE_DIM = 256
RET_QK_WIDTH = RET_HEADS * RET_KEY_DIM
RET_V_WIDTH = RET_HEADS * RET_VALUE_DIM
RET_CHUNK = 128

FFN_HIDDEN = 2816
NORM_EPS = 1e-6
MASK_VALUE = -1e30

IN_WIDTH = 3 * ATT_WIDTH + 2 * RET_QK_WIDTH + 2 * RET_V_WIDTH + 2 * D_MODEL
COL_TILE = 512
ATT_TILES = 3 * ATT_WIDTH // COL_TILE
REST_SLABS = (IN_WIDTH - 3 * ATT_WIDTH) // LANES
REST_QR, REST_KR, REST_VR, REST_GR, REST_GA, REST_GRT = 0, 4, 8, 16, 24, 32

TM_IN = 512
TM_OUT = 512
FFN_CHUNK = 256
ATT_BODY_BLOCKS = 16
MERGE_ROWS = 64
ATT_SKEW = 2
LOG2_E = 1.4426950408889634
RET_UNROLL = 16


def _rms(x, g):
    ms = jnp.mean(x * x, axis=-1, keepdims=True)
    return x * lax.rsqrt(ms + NORM_EPS) * g


def _rotate(t, cos, sin_signed):
    return t * cos + pltpu.roll(t, LANES // 2, 1) * sin_signed


def _inproj_kernel(x_ref, g_ref, w_ref, ca, sa, cr, sr,
                   a0_ref, a1_ref, a2_ref, rest_ref, y_scr, h_scr, tab_scr):
    tm = x_ref.shape[0]
    n_slabs = D_MODEL // LANES

    def stage_rows(x_ref, dst):
        y = _rms(x_ref[...], g_ref[...])
        dst[0] = y.astype(BF16)
        for k in range(n_slabs):
            y_scr[k] = y[:, k * LANES:(k + 1) * LANES]
        for gi, d in enumerate(DILATIONS):
            if d == 1:
                continue
            n = tm // d
            for r in range(d):
                for k in range(n_slabs):
                    dst[gi, r * n:(r + 1) * n, k * LANES:(k + 1) * LANES] = (
                        y_scr[k, pl.ds(r, n, stride=d), :].astype(BF16))

    stage_rows(x_ref, h_scr)

    for gi, d in enumerate(DILATIONS):
        if d == 1:
            continue
        n = tm // d
        for r in range(d):
            tab_scr[gi - 1, 0, r * n:(r + 1) * n, :] = ca[pl.ds(r, n, stride=d), :]
            tab_scr[gi - 1, 1, r * n:(r + 1) * n, :] = sa[pl.ds(r, n, stride=d), :]

    def att_tables(g):
        if g == 0:
            return ca[...], sa[...]
        return tab_scr[g - 1, 0], tab_scr[g - 1, 1]

    def write_att(g, slab, val):
        v = val.astype(BF16)
        d = DILATIONS[g]
        n = tm // d
        if g == 0:
            a0_ref[slab] = v
        else:
            out = a1_ref if g == 1 else a2_ref
            for r in range(d):
                out[slab, r] = v[r * n:(r + 1) * n, :]

    def att_tile(c):
        kind, g = divmod(c, 3)
        res = jnp.dot(h_scr[g], w_ref[:, c * COL_TILE:(c + 1) * COL_TILE],
                      preferred_element_type=F32)
        for k in range(PAIRS):
            s = res[:, k * LANES:(k + 1) * LANES]
            if kind < 2:
                s = _rotate(s, *att_tables(g))
            write_att(g, kind * PAIRS + k, s)

    def rest_tile(c):
        col = 3 * ATT_WIDTH + c * COL_TILE
        res = jnp.dot(h_scr[0], w_ref[:, col:col + COL_TILE], preferred_element_type=F32)
        for k in range(COL_TILE // LANES):
            slab = c * (COL_TILE // LANES) + k
            s = res[:, k * LANES:(k + 1) * LANES]
            if slab < REST_VR:
                s = _rotate(s, cr[...], sr[...])
                if slab >= REST_KR:
                    s = s * (RET_KEY_DIM ** -0.5)
            rest_ref[slab] = s.astype(BF16)

    for c in range(REST_SLABS * LANES // COL_TILE):
        rest_tile(c)
    for g in range(len(DILATIONS)):
        for kind in range(3):
            att_tile(kind * 3 + g)


def _in_projection(x2d, g, w, tabs, batch, seq):
    t = x2d.shape[0]
    tm = TM_IN
    nt = seq // tm
    n_steps = t // tm
    tab_spec = pl.BlockSpec((tm, LANES), lambda i: (i % nt, 0))
    in_specs = [
        pl.BlockSpec((tm, D_MODEL), lambda i: (i, 0)),
        pl.BlockSpec((1, D_MODEL), lambda i: (0, 0)),
        pl.BlockSpec((D_MODEL, IN_WIDTH), lambda i: (0, 0), pipeline_mode=pl.Buffered(1)),
    ] + [tab_spec] * 4
    n_att = 3 * PAIRS
    d1, d2 = DILATIONS[1], DILATIONS[2]
    out_shape = [
        jax.ShapeDtypeStruct((n_att, t, LANES), BF16),
        jax.ShapeDtypeStruct((n_att, batch, d1, seq // d1, LANES), BF16),
        jax.ShapeDtypeStruct((n_att, batch, d2, seq // d2, LANES), BF16),
        jax.ShapeDtypeStruct((REST_SLABS, t, LANES), BF16),
    ]
    out_specs = [
        pl.BlockSpec((n_att, tm, LANES), lambda i: (0, i, 0)),
        pl.BlockSpec((n_att, None, d1, tm // d1, LANES), lambda i: (0, i // nt, 0, i % nt, 0)),
        pl.BlockSpec((n_att, None, d2, tm // d2, LANES), lambda i: (0, i // nt, 0, i % nt, 0)),
        pl.BlockSpec((REST_SLABS, tm, LANES), lambda i: (0, i, 0)),
    ]
    return pl.pallas_call(
        _inproj_kernel,
        grid=(n_steps,),
        in_specs=in_specs,
        out_specs=out_specs,
        out_shape=out_shape,
        scratch_shapes=[
            pltpu.VMEM((D_MODEL // LANES, tm, LANES), F32),
            pltpu.VMEM((len(DILATIONS), tm, D_MODEL), BF16),
            pltpu.VMEM((len(DILATIONS) - 1, 2, tm, LANES), F32),
        ],
        compiler_params=pltpu.CompilerParams(
            dimension_semantics=("arbitrary",), vmem_limit_bytes=V7X_VMEM_LIMIT_BYTES),
        name="in_projection",
    )(x2d, g, w, *tabs)


def _attn_scores(q, kk, bias, qk_first_head):
    zero = jnp.zeros_like(q)
    q2 = jnp.concatenate([jnp.where(qk_first_head, q, zero), jnp.where(qk_first_head, zero, q)], axis=0)
    return lax.dot_general(q2, kk, (((1,), (1,)), ((), ())), preferred_element_type=F32) + bias


def _attn_partials(s, vv, v_first_head):
    nq = s.shape[0] // 2
    m = jnp.max(s, axis=1, keepdims=True)
    p = jnp.exp2(s - m)
    l = jnp.sum(p, axis=1, keepdims=True)
    pv = jnp.dot(p.astype(BF16), vv, preferred_element_type=F32)
    acc = jnp.where(v_first_head, pv[:nq], pv[nq:])
    den = jnp.where(v_first_head, l[:nq], l[nq:])
    mm = jnp.where(v_first_head, m[:nq], m[nq:])
    return acc, den, mm


def _attn_kernel(q0, k0, v0, q1, k1, v1, q2, k2, v2, o_ref,
                 acc0, den0, max0, acc1, den1, max1, acc2, den2, max2):
    blk = BAND
    seq = q0.shape[0]
    lane = lax.broadcasted_iota(jnp.int32, (1, LANES), 1)
    qk_first = (lane // (HEAD_DIM // 2)) % 2 == 0
    v_first = lane < HEAD_DIM
    qi = lax.broadcasted_iota(jnp.int32, (2 * blk, 2 * blk), 0) & (blk - 1)
    kj = lax.broadcasted_iota(jnp.int32, (2 * blk, 2 * blk), 1)
    band_bias = jnp.where((kj >= qi) & (kj <= qi + BAND), 0.0, MASK_VALUE).astype(F32)
    qi1 = lax.broadcasted_iota(jnp.int32, (2 * blk, blk), 0) & (blk - 1)
    kj1 = lax.broadcasted_iota(jnp.int32, (2 * blk, blk), 1)
    first_bias = jnp.where(kj1 <= qi1, 0.0, MASK_VALUE).astype(F32)

    def key_rows(n):
        if isinstance(n, int) and n == 0:
            return pl.ds(0, blk), first_bias
        st = n * blk if isinstance(n, int) else pl.multiple_of(n * blk, blk)
        return pl.ds(st - blk, 2 * blk), band_bias

    def query_rows(n):
        return pl.ds(n * blk if isinstance(n, int) else pl.multiple_of(n * blk, blk), blk)

    def run_blocks(blocks, emit):
        scores = []
        for i in range(len(blocks) + ATT_SKEW):
            if i < len(blocks):
                qr, kr, _, n = blocks[i]
                rows, bias = key_rows(n)
                scores.append(_attn_scores(qr[query_rows(n), :], kr[rows, :], bias, qk_first))
            if i >= ATT_SKEW:
                j = i - ATT_SKEW
                _, _, vr, n = blocks[j]
                rows, _ = key_rows(n)
                emit(j, _attn_partials(scores[j], vr[rows, :], v_first))
                scores[j] = None

    def dilated_group(qr, kr, vr, outs, d):
        nb = seq // d // blk
        subs = max(1, ATT_BODY_BLOCKS // nb)

        def body(j, carry):
            blocks, where = [], []
            for ri in range(subs):
                r = j * subs + ri
                for n in range(nb):
                    blocks.append((qr.at[r], kr.at[r], vr.at[r], n))
                    where.append(pl.ds(n * (blk * d) + r, blk, stride=d))

            def emit(i, parts):
                for ref, val in zip(outs, parts):
                    ref[where[i], :] = val

            run_blocks(blocks, emit)
            return carry

        lax.fori_loop(0, d // subs, body, 0)

    dilated_group(q1, k1, v1, (acc1, den1, max1), DILATIONS[1])
    dilated_group(q2, k2, v2, (acc2, den2, max2), DILATIONS[2])

    def dense_body(j, carry):
        ns = [j * ATT_BODY_BLOCKS + i for i in range(ATT_BODY_BLOCKS)]

        def emit(i, parts):
            for ref, val in zip((acc0, den0, max0), parts):
                ref[query_rows(ns[i]), :] = val

        run_blocks([(q0, k0, v0, n) for n in ns], emit)
        return carry

    dense_body(0, 0)
    lax.fori_loop(1, seq // blk // ATT_BODY_BLOCKS, dense_body, 0)

    def merge(t, carry):
        rows = pl.ds(pl.multiple_of(t * MERGE_ROWS, MERGE_ROWS), MERGE_ROWS)
        m0, m1, m2 = max0[rows, :], max1[rows, :], max2[rows, :]
        mx = jnp.maximum(m0, jnp.maximum(m1, m2))
        e0, e1, e2 = jnp.exp2(m0 - mx), jnp.exp2(m1 - mx), jnp.exp2(m2 - mx)
        num = e0 * acc0[rows, :] + e1 * acc1[rows, :] + e2 * acc2[rows, :]
        den = e0 * den0[rows, :] + e1 * den1[rows, :] + e2 * den2[rows, :]
        o_ref[rows, :] = (num * (1.0 / den)).astype(o_ref.dtype)
        return carry

    lax.fori_loop(0, seq // MERGE_ROWS, merge, 0, unroll=2)


def _dilated_attention(a0, a1, a2, batch, seq):
    d1, d2 = DILATIONS[1], DILATIONS[2]
    specs0 =[pl.BlockSpec((None, seq, LANES), functools.partial(lambda b, p, k: (k * PAIRS + p, b, 0), k=k))
              for k in range(3)]
    specs1 = [pl.BlockSpec((None, None, d1, seq // d1, LANES),
                           functools.partial(lambda b, p, k: (k * PAIRS + p, b, 0, 0, 0), k=k))
              for k in range(3)]
    specs2 = [pl.BlockSpec((None, None, d2, seq // d2, LANES),
                           functools.partial(lambda b, p, k: (k * PAIRS + p, b, 0, 0, 0), k=k))
              for k in range(3)]
    in_specs = specs0 + specs1 + specs2
    return pl.pallas_call(
        _attn_kernel,
        grid=(batch, PAIRS),
        in_specs=in_specs,
        out_specs=pl.BlockSpec((None, seq, LANES), lambda b, p: (p, b, 0)),
        out_shape=jax.ShapeDtypeStruct((PAIRS, batch * seq, LANES), BF16),
        scratch_shapes=[pltpu.VMEM((seq, LANES), F32)] * 9,
        compiler_params=pltpu.CompilerParams(
            dimension_semantics=("arbitrary", "arbitrary"), vmem_limit_bytes=V7X_VMEM_LIMIT_BYTES),
        name="dilated_attention",
    )(a0, a0, a0, a1, a1, a1, a2, a2, a2)


def _retention_kernel(q_ref, k_ref, v_ref, dec_ref, zeta_ref, xi_ref, cd_ref, o_ref, kv_scr, state_scr):
    c_len = RET_CHUNK
    n_chunks = q_ref.shape[0] // c_len
    dec = dec_ref[...]
    zeta = zeta_ref[...]
    xi = xi_ref[...]
    xi2 = jnp.concatenate([xi, xi], axis=1)
    cd = cd_ref[...]
    cd2 = jnp.concatenate([cd, cd], axis=1)

    def chunk_rows(c):
        return pl.ds(pl.multiple_of(c * c_len, c_len), c_len)

    def values(rows):
        return jnp.concatenate([v_ref[0, rows, :], v_ref[1, rows, :]], axis=1)

    def chunk_kv(c, carry):
        rows = chunk_rows(c)
        kz = (k_ref[rows, :].astype(F32) * zeta).astype(BF16)
        kv_scr[c] = lax.dot_general(kz, values(rows), (((0,), (0,)), ((), ())),
                                    preferred_element_type=F32)
        return carry

    lax.fori_loop(0, n_chunks, chunk_kv, 0, unroll=RET_UNROLL)

    def chunk_state(c, state):
        state_scr[c] = state.astype(BF16)
        return state * cd2 + kv_scr[c]

    lax.fori_loop(0, n_chunks, chunk_state, jnp.zeros((RET_KEY_DIM, RET_VALUE_DIM), F32))

    def chunk_group_out(j, carry):
        chunks = [j * RET_UNROLL + i for i in range(RET_UNROLL)]
        scores = []
        for c in chunks:
            rows = chunk_rows(c)
            a = lax.dot_general(q_ref[rows, :], k_ref[rows, :], (((1,), (1,)), ((), ())),
                                preferred_element_type=F32) * dec
            scores.append(a.astype(BF16))
        for c, a in zip(chunks, scores):
            rows = chunk_rows(c)
            o = jnp.dot(a, values(rows), preferred_element_type=F32)
            o = o + jnp.dot(q_ref[rows, :], state_scr[c], preferred_element_type=F32) * xi2
            o_ref[0, rows, :] = o[:, :LANES].astype(o_ref.dtype)
            o_ref[1, rows, :] = o[:, LANES:].astype(o_ref.dtype)
        return carry

    lax.fori_loop(0, n_chunks // RET_UNROLL, chunk_group_out, 0)


def _retention(rest, tabs, batch, seq):
    dec, zeta, xi, cd = tabs
    vs = RET_VALUE_DIM // LANES
    tab_spec = pl.BlockSpec((None, RET_CHUNK, LANES), lambda b, h: (h, 0, 0))
    in_specs = [
        pl.BlockSpec((None, seq, LANES), lambda b, h: (REST_QR + h, b, 0)),
        pl.BlockSpec((None, seq, LANES), lambda b, h: (REST_KR + h, b, 0)),
        pl.BlockSpec((vs, seq, LANES), lambda b, h: (REST_VR // vs + h, b, 0)),
        tab_spec, tab_spec, tab_spec,
        pl.BlockSpec((None, 1, LANES), lambda b, h: (h, 0, 0)),
    ]
    return pl.pallas_call(
        _retention_kernel,
        grid=(batch, RET_HEADS),
        in_specs=in_specs,
        out_specs=pl.BlockSpec((vs, seq, LANES), lambda b, h: (h, b, 0)),
        out_shape=jax.ShapeDtypeStruct((RET_V_WIDTH // LANES, batch * seq, LANES), BF16),
        scratch_shapes=[pltpu.VMEM((seq // RET_CHUNK, RET_KEY_DIM, RET_VALUE_DIM), F32),
                        pltpu.VMEM((seq // RET_CHUNK, RET_KEY_DIM, RET_VALUE_DIM), BF16)],
        compiler_params=pltpu.CompilerParams(
            dimension_semantics=("arbitrary", "arbitrary"), vmem_limit_bytes=V7X_VMEM_LIMIT_BYTES),
        name="retention",
    )(rest, rest, rest, dec, zeta, xi, cd)


def _slabs(ref, lo=0, hi=None):
    hi = ref.shape[0] if hi is None else hi
    return jnp.concatenate([ref[k] for k in range(lo, hi)], axis=1)


def _tail_kernel(oa_ref, ret_ref, sg_ref, ga_ref, gr_ref, x_ref, wa_ref, wr_ref, wo_ref, g2_ref,
                 wg_ref, wu_ref, wd_ref, gf_ref, out_ref, ret_scr, act_scr):
    vs = RET_VALUE_DIM // LANES
    for h in range(RET_HEADS):
        o = _slabs(ret_ref, h * vs, (h + 1) * vs).astype(F32)
        oc = o - jnp.mean(o, axis=-1, keepdims=True)
        var = jnp.mean(oc * oc, axis=-1, keepdims=True)
        gate = _slabs(sg_ref, h * vs, (h + 1) * vs).astype(F32)
        gated = oc * lax.rsqrt(var + NORM_EPS) * (gate * jax.nn.sigmoid(gate))
        ret_scr[:, h * RET_VALUE_DIM:(h + 1) * RET_VALUE_DIM] = gated.astype(BF16)
    ya = jnp.dot(_slabs(oa_ref), wa_ref[...], preferred_element_type=F32)
    yr = jnp.dot(ret_scr[...], wr_ref[...], preferred_element_type=F32)
    merged = (jax.nn.sigmoid(_slabs(ga_ref).astype(F32)) * ya
              + jax.nn.sigmoid(_slabs(gr_ref).astype(F32)) * yr)
    x1 = x_ref[...] + jnp.dot(merged.astype(BF16), wo_ref[...], preferred_element_type=F32)
    h2 = _rms(x1, g2_ref[...]).astype(BF16)
    for c in range(0, FFN_HIDDEN, FFN_CHUNK):
        gate = jnp.dot(h2, wg_ref[:, c:c + FFN_CHUNK], preferred_element_type=F32)
        up = jnp.dot(h2, wu_ref[:, c:c + FFN_CHUNK], preferred_element_type=F32)
        act_scr[:, c:c + FFN_CHUNK] = (gate * jax.nn.sigmoid(gate) * up).astype(BF16)
    down = jnp.dot(act_scr[...], wd_ref[...], preferred_element_type=F32)
    out_ref[...] = _rms(x1 + down, gf_ref[...])


def _tail(oa, ret, rest, x2d, wa, wr, wo, g2, wg, wu, wd, gf):
    t = x2d.shape[0]
    tm = TM_OUT
    gs = D_MODEL // LANES
    full = lambda shape: pl.BlockSpec(shape, lambda i: (0,) * len(shape), pipeline_mode=pl.Buffered(1))
    vec = pl.BlockSpec((1, D_MODEL), lambda i: (0, 0))
    row = pl.BlockSpec((tm, D_MODEL), lambda i: (i, 0))
    in_specs = [
        pl.BlockSpec((PAIRS, tm, LANES), lambda i: (0, i, 0)),
        pl.BlockSpec((gs, tm, LANES), lambda i: (0, i, 0)),
        pl.BlockSpec((gs, tm, LANES), lambda i: (REST_GR // gs, i, 0)),
        pl.BlockSpec((gs, tm, LANES), lambda i: (REST_GA // gs, i, 0)),
        pl.BlockSpec((gs, tm, LANES), lambda i: (REST_GRT // gs, i, 0)),
        row,
        full((GROUP_WIDTH, D_MODEL)), full((RET_V_WIDTH, D_MODEL)), full((D_MODEL, D_MODEL)), vec,
        full((D_MODEL, FFN_HIDDEN)), full((D_MODEL, FFN_HIDDEN)), full((FFN_HIDDEN, D_MODEL)), vec,
    ]
    return pl.pallas_call(
        _tail_kernel,
        grid=(t // tm,),
        in_specs=in_specs,
        out_specs=row,
        out_shape=jax.ShapeDtypeStruct((t, D_MODEL), F32),
        scratch_shapes=[pltpu.VMEM((tm, RET_V_WIDTH), BF16), pltpu.VMEM((tm, FFN_HIDDEN), BF16)],
        compiler_params=pltpu.CompilerParams(
            dimension_semantics=("arbitrary",), vmem_limit_bytes=V7X_VMEM_LIMIT_BYTES),
        name="merge_ffn",
    )(oa, ret, rest, rest, rest, x2d, wa, wr, wo, g2, wg, wu, wd, gf)


def _prep_w_in(w):
    half = HEAD_DIM // 2

    def pair_layout(wq):
        wq = wq.reshape(D_MODEL, len(DILATIONS), PAIRS, 2, 2, half)
        return wq.transpose(0, 1, 2, 4, 3, 5).reshape(D_MODEL, ATT_WIDTH)

    def even_odd(wq):
        wq = wq.reshape(D_MODEL, RET_HEADS, RET_KEY_DIM // 2, 2)
        return wq.transpose(0, 1, 3, 2).reshape(D_MODEL, RET_QK_WIDTH)

    o = 0
    qa = pair_layout(w[:, o:o + ATT_WIDTH]) * (HEAD_DIM ** -0.5 * LOG2_E); o += ATT_WIDTH
    ka = pair_layout(w[:, o:o + ATT_WIDTH]); o += ATT_WIDTH
    va = w[:, o:o + ATT_WIDTH]; o += ATT_WIDTH
    qr = even_odd(w[:, o:o + RET_QK_WIDTH]); o += RET_QK_WIDTH
    kr = even_odd(w[:, o:o + RET_QK_WIDTH]); o += RET_QK_WIDTH
    rest = w[:, o:]
    return jnp.concatenate([qa, ka, va, qr, kr, rest], axis=1).astype(BF16)


def _position_tables(seq):
    pos = np.arange(seq, dtype=np.float64)
    inv = ROPE_THETA ** (-np.arange(0, HEAD_DIM, 2, dtype=np.float64) / HEAD_DIM)
    ang = pos[:, None] * inv[None, :]
    c, s = np.cos(ang), np.sin(ang)
    cos_a = np.concatenate([c, c, c, c], axis=1)
    sin_a = np.concatenate([-s, -s, s, s], axis=1)
    base = 1.0 / (ROPE_THETA ** np.linspace(0.0, 1.0, RET_KEY_DIM // 2, dtype=np.float64))
    ang_r = pos[:, None] * base[None, :]
    cr, sr = np.cos(ang_r), np.sin(ang_r)
    cos_r = np.concatenate([cr, cr], axis=1)
    sin_r = np.concatenate([-sr, sr], axis=1)
    return [jnp.asarray(t.astype(np.float32)) for t in (cos_a, sin_a, cos_r, sin_r)]


def _retention_tables():
    c = RET_CHUNK
    log_g = np.log1p(-(2.0 ** (-5.0 - np.arange(RET_HEADS, dtype=np.float64))))
    idx = np.arange(c, dtype=np.float64)
    diff = idx[:, None] - idx[None, :]
    decay = np.where(diff[None] >= 0, np.exp(np.maximum(diff, 0.0)[None] * log_g[:, None, None]), 0.0)
    zeta = np.exp((c - 1 - idx)[None, :] * log_g[:, None])
    xi = np.exp((idx + 1.0)[None, :] * log_g[:, None])
    chunk_decay = np.exp(c * log_g)
    bcast = lambda v: np.broadcast_to(v[:, :, None], (RET_HEADS, c, LANES))
    cd = np.broadcast_to(chunk_decay[:, None, None], (RET_HEADS, 1, LANES))
    return tuple(jnp.asarray(np.ascontiguousarray(t).astype(np.float32))
                 for t in (decay, bcast(zeta), bcast(xi), cd))


def kernel(x, norm_mix_g, w_in, w_out_attn, w_out_ret, w_out, norm_ffn_g, w_ffn_gate, w_ffn_up,
           w_ffn_down, norm_final_g):
    batch, seq, _ = x.shape
    assert w_in.shape[0] == 1, "single-layer problem"
    assert seq % (TM_IN * 1) == 0 and TM_IN % (DILATIONS[-1] * 16) == 0
    x2d = x.reshape(batch * seq, D_MODEL)
    a0, a1, a2, rest = _in_projection(
        x2d, norm_mix_g[0][None, :], _prep_w_in(w_in[0]), _position_tables(seq), batch, seq)
    oa = _dilated_attention(a0, a1, a2, batch, seq)
    ret = _retention(rest, _retention_tables(), batch, seq)
    out = _tail(oa, ret, rest, x2d, w_out_attn[0].astype(BF16), w_out_ret[0].astype(BF16),
                w_out[0].astype(BF16), norm_ffn_g[0][None, :], w_ffn_gate[0].astype(BF16),
                w_ffn_up[0].astype(BF16), w_ffn_down[0].astype(BF16), norm_final_g[None, :])
    return out.reshape(batch, seq, D_MODEL)
```

```python
import functools

import jax
import jax.numpy as jnp
import numpy as np
from jax import lax
from jax.experimental import pallas as pl
from jax.experimental.pallas import tpu as pltpu

F32 = jnp.float32
BF16 = jnp.bfloat16

LANES = 128
V7X_VMEM_LIMIT_BYTES = 60 * 1024 * 1024

D_MODEL = 1024
HEAD_DIM = 64
HEADS_PER_GROUP = 8
DILATIONS = (1, 4, 16)
BAND = 128
GROUP_WIDTH = HEADS_PER_GROUP * HEAD_DIM
ATT_WIDTH = len(DILATIONS) * GROUP_WIDTH
PAIRS = GROUP_WIDTH // LANES
ROPE_THETA = 10000.0

RET_HEADS = 4
RET_KEY_DIM = 128
RET_VALUE_DIM = 256
RET_QK_WIDTH = RET_HEADS * RET_KEY_DIM
RET_V_WIDTH = RET_HEADS * RET_VALUE_DIM
RET_CHUNK = 128

FFN_HIDDEN = 2816
NORM_EPS = 1e-6
MASK_VALUE = -1e30

IN_WIDTH = 3 * ATT_WIDTH + 2 * RET_QK_WIDTH + 2 * RET_V_WIDTH + 2 * D_MODEL
COL_TILE = 512
ATT_TILES = 3 * ATT_WIDTH // COL_TILE
REST_SLABS = (IN_WIDTH - 3 * ATT_WIDTH) // LANES
REST_QR, REST_KR, REST_VR, REST_GR, REST_GA, REST_GRT = 0, 4, 8, 16, 24, 32

TM_IN = 512
TM_OUT = 512
FFN_CHUNK = 256
ATT_BODY_BLOCKS = 16
MERGE_ROWS = 64
ATT_SKEW = 2
LOG2_E = 1.4426950408889634
RET_UNROLL = 16


def _rms(x, g):
    ms = jnp.mean(x * x, axis=-1, keepdims=True)
    return x * lax.rsqrt(ms + NORM_EPS) * g


def _rotate(t, cos, sin_signed):
    return t * cos + pltpu.roll(t, LANES // 2, 1) * sin_signed


def _inproj_kernel(x_ref, g_ref, wq_ref, wk_ref, wv_ref, wqr_ref, wkr_ref, wrest_ref, ca, sa, cr, sr,
                   a0_ref, a1_ref, a2_ref, rest_ref, y_scr, h_scr, tab_scr):
    tm = x_ref.shape[0]
    n_slabs = D_MODEL // LANES

    def stage_rows(x_ref, dst):
        y = _rms(x_ref[...], g_ref[...])
        dst[0] = y.astype(BF16)
        for k in range(n_slabs):
            y_scr[k] = y[:, k * LANES:(k + 1) * LANES]
        for gi, d in enumerate(DILATIONS):
            if d == 1:
                continue
            n = tm // d
            for r in range(d):
                for k in range(n_slabs):
                    dst[gi, r * n:(r + 1) * n, k * LANES:(k + 1) * LANES] = (
                        y_scr[k, pl.ds(r, n, stride=d), :].astype(BF16))

    stage_rows(x_ref, h_scr)

    for gi, d in enumerate(DILATIONS):
        if d == 1:
            continue
        n = tm // d
        for r in range(d):
            tab_scr[gi - 1, 0, r * n:(r + 1) * n, :] = ca[pl.ds(r, n, stride=d), :]
            tab_scr[gi - 1, 1, r * n:(r + 1) * n, :] = sa[pl.ds(r, n, stride=d), :]

    def att_tables(g):
        if g == 0:
            return ca[...], sa[...]
        return tab_scr[g - 1, 0], tab_scr[g - 1, 1]

    def write_att(g, slab, val):
        v = val.astype(BF16)
        d = DILATIONS[g]
        n = tm // d
        if g == 0:
            a0_ref[slab] = v
        else:
            out = a1_ref if g == 1 else a2_ref
            for r in range(d):
                out[slab, r] = v[r * n:(r + 1) * n, :]

    def att_tile(c):
        kind, g = divmod(c, 3)
        w_ref = (wq_ref, wk_ref, wv_ref)[kind]
        res = jnp.dot(h_scr[g], w_ref[:, g * COL_TILE:(g + 1) * COL_TILE],
                      preferred_element_type=F32)
        for k in range(PAIRS):
            s = res[:, k * LANES:(k + 1) * LANES]
            if kind < 2:
                s = _rotate(s, *att_tables(g))
            write_att(g, kind * PAIRS + k, s)

    def rest_tile(c):
        qk_tiles = 2 * RET_QK_WIDTH // COL_TILE
        if c < qk_tiles:
            w_tile = (wqr_ref, wkr_ref)[c][...]
        else:
            w_tile = wrest_ref[:, (c - qk_tiles) * COL_TILE:(c - qk_tiles + 1) * COL_TILE]
        res = jnp.dot(h_scr[0], w_tile, preferred_element_type=F32)
        for k in range(COL_TILE // LANES):
            slab = c * (COL_TILE // LANES) + k
            s = res[:, k * LANES:(k + 1) * LANES]
            if slab < REST_VR:
                s = _rotate(s, cr[...], sr[...])
                if slab >= REST_KR:
                    s = s * (RET_KEY_DIM ** -0.5)
            rest_ref[slab] = s.astype(BF16)

    for c in range(REST_SLABS * LANES // COL_TILE):
        rest_tile(c)
    for g in range(len(DILATIONS)):
        for kind in range(3):
            att_tile(kind * 3 + g)


def _in_projection(x2d, g, w, tabs, batch, seq):
    t = x2d.shape[0]
    tm = TM_IN
    nt = seq // tm
    n_steps = t // tm
    tab_spec = pl.BlockSpec((tm, LANES), lambda i: (i % nt, 0))
    resident = [pl.BlockSpec(piece.shape, lambda i: (0, 0), pipeline_mode=pl.Buffered(1)) for piece in w]
    in_specs = [
        pl.BlockSpec((tm, D_MODEL), lambda i: (i, 0)),
        pl.BlockSpec((1, D_MODEL), lambda i: (0, 0)),
    ] + resident + [tab_spec] * 4
    n_att = 3 * PAIRS
    d1, d2 = DILATIONS[1], DILATIONS[2]
    out_shape = [
        jax.ShapeDtypeStruct((n_att, t, LANES), BF16),
        jax.ShapeDtypeStruct((n_att, batch, d1, seq // d1, LANES), BF16),
        jax.ShapeDtypeStruct((n_att, batch, d2, seq // d2, LANES), BF16),
        jax.ShapeDtypeStruct((REST_SLABS, t, LANES), BF16),
    ]
    out_specs = [
        pl.BlockSpec((n_att, tm, LANES), lambda i: (0, i, 0)),
        pl.BlockSpec((n_att, None, d1, tm // d1, LANES), lambda i: (0, i // nt, 0, i % nt, 0)),
        pl.BlockSpec((n_att, None, d2, tm // d2, LANES), lambda i: (0, i // nt, 0, i % nt, 0)),
        pl.BlockSpec((REST_SLABS, tm, LANES), lambda i: (0, i, 0)),
    ]
    return pl.pallas_call(
        _inproj_kernel,
        grid=(n_steps,),
        in_specs=in_specs,
        out_specs=out_specs,
        out_shape=out_shape,
        scratch_shapes=[
            pltpu.VMEM((D_MODEL // LANES, tm, LANES), F32),
            pltpu.VMEM((len(DILATIONS), tm, D_MODEL), BF16),
            pltpu.VMEM((len(DILATIONS) - 1, 2, tm, LANES), F32),
        ],
        compiler_params=pltpu.CompilerParams(
            dimension_semantics=("arbitrary",), vmem_limit_bytes=V7X_VMEM_LIMIT_BYTES),
        name="in_projection",
    )(x2d, g, *w, *tabs)


def _pitch(d):
    return d + 1 if d % 8 == 0 else d


def _attn_scores(q, kk, bias, qk_first_head):
    zero = jnp.zeros_like(q)
    q2 = jnp.concatenate([jnp.where(qk_first_head, q, zero), jnp.where(qk_first_head, zero, q)], axis=0)
    return lax.dot_general(q2, kk, (((1,), (1,)), ((), ())), preferred_element_type=F32) + bias


def _attn_partials(s, vv, v_first_head):
    nq = s.shape[0] // 2
    m = jnp.max(s, axis=1, keepdims=True)
    p = jnp.exp2(s - m)
    l = jnp.sum(p, axis=1, keepdims=True)
    pv = jnp.dot(p.astype(BF16), vv, preferred_element_type=F32)
    acc = jnp.where(v_first_head, pv[:nq], pv[nq:])
    den = jnp.where(v_first_head, l[:nq], l[nq:])
    mm = jnp.where(v_first_head, m[:nq], m[nq:])
    return acc, den, mm


def _attn_kernel(q0, k0, v0, q1, k1, v1, q2, k2, v2, o_ref,
                 acc0, den0, max0, acc1, den1, max1, acc2, den2, max2):
    blk = BAND
    seq = q0.shape[0]
    lane = lax.broadcasted_iota(jnp.int32, (1, LANES), 1)
    qk_first = (lane // (HEAD_DIM // 2)) % 2 == 0
    v_first = lane < HEAD_DIM
    qi = lax.broadcasted_iota(jnp.int32, (2 * blk, 2 * blk), 0) & (blk - 1)
    kj = lax.broadcasted_iota(jnp.int32, (2 * blk, 2 * blk), 1)
    band_bias = jnp.where((kj >= qi) & (kj <= qi + BAND), 0.0, MASK_VALUE).astype(F32)
    qi1 = lax.broadcasted_iota(jnp.int32, (2 * blk, blk), 0) & (blk - 1)
    kj1 = lax.broadcasted_iota(jnp.int32, (2 * blk, blk), 1)
    first_bias = jnp.where(kj1 <= qi1, 0.0, MASK_VALUE).astype(F32)

    def key_rows(n):
        if isinstance(n, int) and n == 0:
            return pl.ds(0, blk), first_bias
        st = n * blk if isinstance(n, int) else pl.multiple_of(n * blk, blk)
        return pl.ds(st - blk, 2 * blk), band_bias

    def query_rows(n):
        return pl.ds(n * blk if isinstance(n, int) else pl.multiple_of(n * blk, blk), blk)

    def run_blocks(blocks, emit):
        scores = []
        for i in range(len(blocks) + ATT_SKEW):
            if i < len(blocks):
                qr, kr, _, n = blocks[i]
                rows, bias = key_rows(n)
                scores.append(_attn_scores(qr[query_rows(n), :], kr[rows, :], bias, qk_first))
            if i >= ATT_SKEW:
                j = i - ATT_SKEW
                _, _, vr, n = blocks[j]
                rows, _ = key_rows(n)
                emit(j, _attn_partials(scores[j], vr[rows, :], v_first))
                scores[j] = None

    def dilated_group(qr, kr, vr, outs, d):
        nb = seq // d // blk
        subs = max(1, ATT_BODY_BLOCKS // nb)

        def body(j, carry):
            blocks, where = [], []
            for ri in range(subs):
                r = j * subs + ri
                for n in range(nb):
                    blocks.append((qr.at[r], kr.at[r], vr.at[r], n))
                    where.append(pl.ds(n * (blk * _pitch(d)) + r, blk, stride=_pitch(d)))

            def emit(i, parts):
                for ref, val in zip(outs, parts):
                    ref[where[i], :] = val

            run_blocks(blocks, emit)
            return carry

        lax.fori_loop(0, d // subs, body, 0)

    dilated_group(q1, k1, v1, (acc1, den1, max1), DILATIONS[1])
    dilated_group(q2, k2, v2, (acc2, den2, max2), DILATIONS[2])

    def dense_body(j, carry):
        ns = [j * ATT_BODY_BLOCKS + i for i in range(ATT_BODY_BLOCKS)]

        def emit(i, parts):
            for ref, val in zip((acc0, den0, max0), parts):
                ref[query_rows(ns[i]), :] = val

        run_blocks([(q0, k0, v0, n) for n in ns], emit)
        return carry

    dense_body(0, 0)
    lax.fori_loop(1, seq // blk // ATT_BODY_BLOCKS, dense_body, 0)

    def merge(t, carry):
        rows = pl.ds(pl.multiple_of(t * MERGE_ROWS, MERGE_ROWS), MERGE_ROWS)

        def padded(ref, d):
            if _pitch(d) == d:
                return ref[rows, :]
            per = MERGE_ROWS // d
            return jnp.concatenate(
                [ref[pl.ds((t * per + i) * _pitch(d), d), :] for i in range(per)], axis=0)

        d2 = DILATIONS[2]
        m0, m1, m2 = max0[rows, :], max1[rows, :], padded(max2, d2)
        mx = jnp.maximum(m0, jnp.maximum(m1, m2))
        e0, e1, e2 = jnp.exp2(m0 - mx), jnp.exp2(m1 - mx), jnp.exp2(m2 - mx)
        num = e0 * acc0[rows, :] + e1 * acc1[rows, :] + e2 * padded(acc2, d2)
        den = e0 * den0[rows, :] + e1 * den1[rows, :] + e2 * padded(den2, d2)
        o_ref[rows, :] = (num * (1.0 / den)).astype(o_ref.dtype)
        return carry

    lax.fori_loop(0, seq // MERGE_ROWS, merge, 0, unroll=2)


def _dilated_attention(a0, a1, a2, batch, seq):
    d1, d2 = DILATIONS[1], DILATIONS[2]
    specs0 =[pl.BlockSpec((None, seq, LANES), functools.partial(lambda b, p, k: (k * PAIRS + p, b, 0), k=k))
              for k in range(3)]
    specs1 = [pl.BlockSpec((None, None, d1, seq // d1, LANES),
                           functools.partial(lambda b, p, k: (k * PAIRS + p, b, 0, 0, 0), k=k))
              for k in range(3)]
    specs2 = [pl.BlockSpec((None, None, d2, seq // d2, LANES),
                           functools.partial(lambda b, p, k: (k * PAIRS + p, b, 0, 0, 0), k=k))
              for k in range(3)]
    in_specs = specs0 + specs1 + specs2
    return pl.pallas_call(
        _attn_kernel,
        grid=(batch, PAIRS),
        in_specs=in_specs,
        out_specs=pl.BlockSpec((None, seq, LANES), lambda b, p: (p, b, 0)),
        out_shape=jax.ShapeDtypeStruct((PAIRS, batch * seq, LANES), BF16),
        scratch_shapes=[pltpu.VMEM((seq // d * _pitch(d), LANES), F32) for d in DILATIONS for _ in range(3)],
        compiler_params=pltpu.CompilerParams(
            dimension_semantics=("arbitrary", "arbitrary"), vmem_limit_bytes=V7X_VMEM_LIMIT_BYTES),
        name="dilated_attention",
    )(a0, a0, a0, a1, a1, a1, a2, a2, a2)


def _retention_kernel(q_ref, k_ref, v_ref, dec_ref, zeta_ref, xi_ref, cd_ref, o_ref, kv_scr, state_scr):
    c_len = RET_CHUNK
    n_chunks = q_ref.shape[0] // c_len
    dec = dec_ref[...]
    zeta = zeta_ref[...]
    xi = xi_ref[...]
    xi2 = jnp.concatenate([xi, xi], axis=1)
    cd = cd_ref[...]
    cd2 = jnp.concatenate([cd, cd], axis=1)

    def chunk_rows(c):
        return pl.ds(pl.multiple_of(c * c_len, c_len), c_len)

    def values(rows):
        return jnp.concatenate([v_ref[0, rows, :], v_ref[1, rows, :]], axis=1)

    def chunk_kv(c, carry):
        rows = chunk_rows(c)
        kz = (k_ref[rows, :].astype(F32) * zeta).astype(BF16)
        kv_scr[c] = lax.dot_general(kz, values(rows), (((0,), (0,)), ((), ())),
                                    preferred_element_type=F32)
        return carry

    lax.fori_loop(0, n_chunks, chunk_kv, 0, unroll=RET_UNROLL)

    def chunk_state(c, state):
        state_scr[c] = state.astype(BF16)
        return state * cd2 + kv_scr[c]

    lax.fori_loop(0, n_chunks, chunk_state, jnp.zeros((RET_KEY_DIM, RET_VALUE_DIM), F32))

    def chunk_group_out(j, carry):
        chunks = [j * RET_UNROLL + i for i in range(RET_UNROLL)]
        scores = []
        for c in chunks:
            rows = chunk_rows(c)
            a = lax.dot_general(q_ref[rows, :], k_ref[rows, :], (((1,), (1,)), ((), ())),
                                preferred_element_type=F32) * dec
            scores.append(a.astype(BF16))
        for c, a in zip(chunks, scores):
            rows = chunk_rows(c)
            o = jnp.dot(a, values(rows), preferred_element_type=F32)
            o = o + jnp.dot(q_ref[rows, :], state_scr[c], preferred_element_type=F32) * xi2
            o_ref[0, rows, :] = o[:, :LANES].astype(o_ref.dtype)
            o_ref[1, rows, :] = o[:, LANES:].astype(o_ref.dtype)
        return carry

    lax.fori_loop(0, n_chunks // RET_UNROLL, chunk_group_out, 0)


def _retention(rest, tabs, batch, seq):
    dec, zeta, xi, cd = tabs
    vs = RET_VALUE_DIM // LANES
    tab_spec = pl.BlockSpec((None, RET_CHUNK, LANES), lambda b, h: (h, 0, 0))
    in_specs = [
        pl.BlockSpec((None, seq, LANES), lambda b, h: (REST_QR + h, b, 0)),
        pl.BlockSpec((None, seq, LANES), lambda b, h: (REST_KR + h, b, 0)),
        pl.BlockSpec((vs, seq, LANES), lambda b, h: (REST_VR // vs + h, b, 0)),
        tab_spec, tab_spec, tab_spec,
        pl.BlockSpec((None, 1, LANES), lambda b, h: (h, 0, 0)),
    ]
    return pl.pallas_call(
        _retention_kernel,
        grid=(batch, RET_HEADS),
        in_specs=in_specs,
        out_specs=pl.BlockSpec((vs, seq, LANES), lambda b, h: (h, b, 0)),
        out_shape=jax.ShapeDtypeStruct((RET_V_WIDTH // LANES, batch * seq, LANES), BF16),
        scratch_shapes=[pltpu.VMEM((seq // RET_CHUNK, RET_KEY_DIM, RET_VALUE_DIM), F32),
                        pltpu.VMEM((seq // RET_CHUNK, RET_KEY_DIM, RET_VALUE_DIM), BF16)],
        compiler_params=pltpu.CompilerParams(
            dimension_semantics=("arbitrary", "arbitrary"), vmem_limit_bytes=V7X_VMEM_LIMIT_BYTES),
        name="retention",
    )(rest, rest, rest, dec, zeta, xi, cd)


def _slabs(ref, lo=0, hi=None):
    hi = ref.shape[0] if hi is None else hi
    return jnp.concatenate([ref[k] for k in range(lo, hi)], axis=1)


def _tail_kernel(oa_ref, ret_ref, sg_ref, ga_ref, gr_ref, x_ref, wa_ref, wr_ref, wo_ref, g2_ref,
                 wg_ref, wu_ref, wd_ref, gf_ref, out_ref, ret_scr, act_scr):
    vs = RET_VALUE_DIM // LANES
    for h in range(RET_HEADS):
        o = _slabs(ret_ref, h * vs, (h + 1) * vs).astype(F32)
        oc = o - jnp.mean(o, axis=-1, keepdims=True)
        var = jnp.mean(oc * oc, axis=-1, keepdims=True)
        gate = _slabs(sg_ref, h * vs, (h + 1) * vs).astype(F32)
        gated = oc * lax.rsqrt(var + NORM_EPS) * (gate * jax.nn.sigmoid(gate))
        ret_scr[:, h * RET_VALUE_DIM:(h + 1) * RET_VALUE_DIM] = gated.astype(BF16)
    ya = jnp.dot(_slabs(oa_ref), wa_ref[...], preferred_element_type=F32)
    yr = jnp.dot(ret_scr[...], wr_ref[...], preferred_element_type=F32)
    merged = (jax.nn.sigmoid(_slabs(ga_ref).astype(F32)) * ya
              + jax.nn.sigmoid(_slabs(gr_ref).astype(F32)) * yr)
    x1 = x_ref[...] + jnp.dot(merged.astype(BF16), wo_ref[...], preferred_element_type=F32)
    h2 = _rms(x1, g2_ref[...]).astype(BF16)
    for c in range(0, FFN_HIDDEN, FFN_CHUNK):
        gate = jnp.dot(h2, wg_ref[:, c:c + FFN_CHUNK], preferred_element_type=F32)
        up = jnp.dot(h2, wu_ref[:, c:c + FFN_CHUNK], preferred_element_type=F32)
        act_scr[:, c:c + FFN_CHUNK] = (gate * jax.nn.sigmoid(gate) * up).astype(BF16)
    down = jnp.dot(act_scr[...], wd_ref[...], preferred_element_type=F32)
    out_ref[...] = _rms(x1 + down, gf_ref[...])


def _tail(oa, ret, rest, x2d, wa, wr, wo, g2, wg, wu, wd, gf):
    t = x2d.shape[0]
    tm = TM_OUT
    gs = D_MODEL // LANES
    full = lambda shape: pl.BlockSpec(shape, lambda i: (0,) * len(shape), pipeline_mode=pl.Buffered(1))
    vec = pl.BlockSpec((1, D_MODEL), lambda i: (0, 0))
    row = pl.BlockSpec((tm, D_MODEL), lambda i: (i, 0))
    in_specs = [
        pl.BlockSpec((PAIRS, tm, LANES), lambda i: (0, i, 0)),
        pl.BlockSpec((gs, tm, LANES), lambda i: (0, i, 0)),
        pl.BlockSpec((gs, tm, LANES), lambda i: (REST_GR // gs, i, 0)),
        pl.BlockSpec((gs, tm, LANES), lambda i: (REST_GA // gs, i, 0)),
        pl.BlockSpec((gs, tm, LANES), lambda i: (REST_GRT // gs, i, 0)),
        row,
        full((GROUP_WIDTH, D_MODEL)), full((RET_V_WIDTH, D_MODEL)), full((D_MODEL, D_MODEL)), vec,
        full((D_MODEL, FFN_HIDDEN)), full((D_MODEL, FFN_HIDDEN)), full((FFN_HIDDEN, D_MODEL)), vec,
    ]
    return pl.pallas_call(
        _tail_kernel,
        grid=(t // tm,),
        in_specs=in_specs,
        out_specs=row,
        out_shape=jax.ShapeDtypeStruct((t, D_MODEL), F32),
        scratch_shapes=[pltpu.VMEM((tm, RET_V_WIDTH), BF16), pltpu.VMEM((tm, FFN_HIDDEN), BF16)],
        compiler_params=pltpu.CompilerParams(
            dimension_semantics=("arbitrary",), vmem_limit_bytes=V7X_VMEM_LIMIT_BYTES),
        name="merge_ffn",
    )(oa, ret, rest, rest, rest, x2d, wa, wr, wo, g2, wg, wu, wd, gf)


def _prep_w_in(w):
    half = HEAD_DIM // 2

    def pair_layout(wq):
        wq = wq.reshape(D_MODEL, len(DILATIONS), PAIRS, 2, 2, half)
        return wq.transpose(0, 1, 2, 4, 3, 5).reshape(D_MODEL, ATT_WIDTH)

    def even_odd(wq):
        wq = wq.reshape(D_MODEL, RET_HEADS, RET_KEY_DIM // 2, 2)
        return wq.transpose(0, 1, 3, 2).reshape(D_MODEL, RET_QK_WIDTH)

    o = 0
    qa = pair_layout(w[:, o:o + ATT_WIDTH]) * (HEAD_DIM ** -0.5 * LOG2_E); o += ATT_WIDTH
    ka = pair_layout(w[:, o:o + ATT_WIDTH]); o += ATT_WIDTH
    va = w[:, o:o + ATT_WIDTH]; o += ATT_WIDTH
    qr = even_odd(w[:, o:o + RET_QK_WIDTH]); o += RET_QK_WIDTH
    kr = even_odd(w[:, o:o + RET_QK_WIDTH]); o += RET_QK_WIDTH
    rest = w[:, o:]
    return [p.astype(BF16) for p in (qa, ka, va, qr, kr, rest)]


def _position_tables(seq):
    pos = np.arange(seq, dtype=np.float64)
    inv = ROPE_THETA ** (-np.arange(0, HEAD_DIM, 2, dtype=np.float64) / HEAD_DIM)
    ang = pos[:, None] * inv[None, :]
    c, s = np.cos(ang), np.sin(ang)
    cos_a = np.concatenate([c, c, c, c], axis=1)
    sin_a = np.concatenate([-s, -s, s, s], axis=1)
    base = 1.0 / (ROPE_THETA ** np.linspace(0.0, 1.0, RET_KEY_DIM // 2, dtype=np.float64))
    ang_r = pos[:, None] * base[None, :]
    cr, sr = np.cos(ang_r), np.sin(ang_r)
    cos_r = np.concatenate([cr, cr], axis=1)
    sin_r = np.concatenate([-sr, sr], axis=1)
    return [jnp.asarray(t.astype(np.float32)) for t in (cos_a, sin_a, cos_r, sin_r)]


def _retention_tables():
    c = RET_CHUNK
    log_g = np.log1p(-(2.0 ** (-5.0 - np.arange(RET_HEADS, dtype=np.float64))))
    idx = np.arange(c, dtype=np.float64)
    diff = idx[:, None] - idx[None, :]
    decay = np.where(diff[None] >= 0, np.exp(np.maximum(diff, 0.0)[None] * log_g[:, None, None]), 0.0)
    zeta = np.exp((c - 1 - idx)[None, :] * log_g[:, None])
    xi = np.exp((idx + 1.0)[None, :] * log_g[:, None])
    chunk_decay = np.exp(c * log_g)
    bcast = lambda v: np.broadcast_to(v[:, :, None], (RET_HEADS, c, LANES))
    cd = np.broadcast_to(chunk_decay[:, None, None], (RET_HEADS, 1, LANES))
    return tuple(jnp.asarray(np.ascontiguousarray(t).astype(np.float32))
                 for t in (decay, bcast(zeta), bcast(xi), cd))


def kernel(x, norm_mix_g, w_in, w_out_attn, w_out_ret, w_out, norm_ffn_g, w_ffn_gate, w_ffn_up,
           w_ffn_down, norm_final_g):
    batch, seq, _ = x.shape
    assert w_in.shape[0] == 1, "single-layer problem"
    assert seq % (TM_IN * 1) == 0 and TM_IN % (DILATIONS[-1] * 16) == 0
    x2d = x.reshape(batch * seq, D_MODEL)
    a0, a1, a2, rest = _in_projection(
        x2d, norm_mix_g[0][None, :], _prep_w_in(w_in[0]), _position_tables(seq), batch, seq)
    oa = _dilated_attention(a0, a1, a2, batch, seq)
    ret = _retention(rest, _retention_tables(), batch, seq)
    out = _tail(oa, ret, rest, x2d, w_out_attn[0].astype(BF16), w_out_ret[0].astype(BF16),
                w_out[0].astype(BF16), norm_ffn_g[0][None, :], w_ffn_gate[0].astype(BF16),
                w_ffn_up[0].astype(BF16), w_ffn_down[0].astype(BF16), norm_final_g[None, :])
    return out.reshape(batch, seq, D_MODEL)
```

```python
import functools

import jax
import jax.numpy as jnp
import numpy as np
from jax import lax
from jax.experimental import pallas as pl
from jax.experimental.pallas import tpu as pltpu

F32 = jnp.float32
BF16 = jnp.bfloat16

LANES = 128
V7X_VMEM_LIMIT_BYTES = 60 * 1024 * 1024

D_MODEL = 1024
HEAD_DIM = 64
HEADS_PER_GROUP = 8
DILATIONS = (1, 4, 16)
BAND = 128
GROUP_WIDTH = HEADS_PER_GROUP * HEAD_DIM
ATT_WIDTH = len(DILATIONS) * GROUP_WIDTH
PAIRS = GROUP_WIDTH // LANES
ROPE_THETA = 10000.0

RET_HEADS = 4
RET_KEY_DIM = 128
RET_VALUE_DIM = 256
RET_QK_WIDTH = RET_HEADS * RET_KEY_DIM
RET_V_WIDTH = RET_HEADS * RET_VALUE_DIM
RET_CHUNK = 128

FFN_HIDDEN = 2816
NORM_EPS = 1e-6
MASK_VALUE = -1e30

IN_WIDTH = 3 * ATT_WIDTH + 2 * RET_QK_WIDTH + 2 * RET_V_WIDTH + 2 * D_MODEL
COL_TILE = 512
ATT_TILES = 3 * ATT_WIDTH // COL_TILE
REST_SLABS = (IN_WIDTH - 3 * ATT_WIDTH) // LANES
REST_QR, REST_KR, REST_VR, REST_GR, REST_GA, REST_GRT = 0, 4, 8, 16, 24, 32

TM_IN = 512
TM_OUT = 512
FFN_CHUNK = 256
ATT_BODY_BLOCKS = 16
MERGE_ROWS = 64
ATT_SKEW = 2
LOG2_E = 1.4426950408889634
RET_UNROLL = 16


def _rms(x, g):
    ms = jnp.mean(x * x, axis=-1, keepdims=True)
    return x * lax.rsqrt(ms + NORM_EPS) * g


def _rotate(t, cos, sin_signed, span):
    lane = lax.broadcasted_iota(jnp.int32, (1, LANES), 1)
    low = (lane % (2 * span)) < span
    partner = jnp.where(low, pltpu.roll(t, LANES - span, 1), pltpu.roll(t, span, 1))
    return t * cos + partner * sin_signed


def _inproj_kernel(x_ref, g_ref, w_ref, ca, sa, cr, sr,
                   a0_ref, a1_ref, a2_ref, rest_ref, y_scr, h_scr, tab_scr):
    tm = x_ref.shape[0]
    n_slabs = D_MODEL // LANES

    def stage_rows(x_ref, dst):
        y = _rms(x_ref[...], g_ref[...])
        dst[0] = y.astype(BF16)
        for k in range(n_slabs):
            y_scr[k] = y[:, k * LANES:(k + 1) * LANES]
        for gi, d in enumerate(DILATIONS):
            if d == 1:
                continue
            n = tm // d
            for r in range(d):
                for k in range(n_slabs):
                    dst[gi, r * n:(r + 1) * n, k * LANES:(k + 1) * LANES] = (
                        y_scr[k, pl.ds(r, n, stride=d), :].astype(BF16))

    stage_rows(x_ref, h_scr)

    for gi, d in enumerate(DILATIONS):
        if d == 1:
            continue
        n = tm // d
        for r in range(d):
            tab_scr[gi - 1, 0, r * n:(r + 1) * n, :] = ca[pl.ds(r, n, stride=d), :]
            tab_scr[gi - 1, 1, r * n:(r + 1) * n, :] = sa[pl.ds(r, n, stride=d), :]

    def att_tables(g):
        if g == 0:
            return ca[...], sa[...]
        return tab_scr[g - 1, 0], tab_scr[g - 1, 1]

    def write_att(g, slab, val):
        v = val.astype(BF16)
        d = DILATIONS[g]
        n = tm // d
        if g == 0:
            a0_ref[slab] = v
        else:
            out = a1_ref if g == 1 else a2_ref
            for r in range(d):
                out[slab, r] = v[r * n:(r + 1) * n, :]

    def att_tile(c):
        kind, g = divmod(c, 3)
        res = jnp.dot(h_scr[g], w_ref[:, c * COL_TILE:(c + 1) * COL_TILE],
                      preferred_element_type=F32)
        for k in range(PAIRS):
            s = res[:, k * LANES:(k + 1) * LANES]
            if kind < 2:
                s = _rotate(s, *att_tables(g), span=HEAD_DIM // 2)
            write_att(g, kind * PAIRS + k, s)

    def rest_tile(c):
        col = 3 * ATT_WIDTH + c * COL_TILE
        res = jnp.dot(h_scr[0], w_ref[:, col:col + COL_TILE], preferred_element_type=F32)
        for k in range(COL_TILE // LANES):
            slab = c * (COL_TILE // LANES) + k
            s = res[:, k * LANES:(k + 1) * LANES]
            if slab < REST_VR:
                s = _rotate(s, cr[...], sr[...], span=1)
                if slab >= REST_KR:
                    s = s * (RET_KEY_DIM ** -0.5)
            rest_ref[slab] = s.astype(BF16)

    for c in range(REST_SLABS * LANES // COL_TILE):
        rest_tile(c)
    for g in range(len(DILATIONS)):
        for kind in range(3):
            att_tile(kind * 3 + g)


def _in_projection(x2d, g, w, tabs, batch, seq):
    t = x2d.shape[0]
    tm = TM_IN
    nt = seq // tm
    n_steps = t // tm
    tab_spec = pl.BlockSpec((tm, LANES), lambda i: (i % nt, 0))
    in_specs = [
        pl.BlockSpec((tm, D_MODEL), lambda i: (i, 0)),
        pl.BlockSpec((1, D_MODEL), lambda i: (0, 0)),
        pl.BlockSpec((D_MODEL, IN_WIDTH), lambda i: (0, 0), pipeline_mode=pl.Buffered(1)),
    ] + [tab_spec] * 4
    n_att = 3 * PAIRS
    d1, d2 = DILATIONS[1], DILATIONS[2]
    out_shape = [
        jax.ShapeDtypeStruct((n_att, t, LANES), BF16),
        jax.ShapeDtypeStruct((n_att, batch, d1, seq // d1, LANES), BF16),
        jax.ShapeDtypeStruct((n_att, batch, d2, seq // d2, LANES), BF16),
        jax.ShapeDtypeStruct((REST_SLABS, t, LANES), BF16),
    ]
    out_specs = [
        pl.BlockSpec((n_att, tm, LANES), lambda i: (0, i, 0)),
        pl.BlockSpec((n_att, None, d1, tm // d1, LANES), lambda i: (0, i // nt, 0, i % nt, 0)),
        pl.BlockSpec((n_att, None, d2, tm // d2, LANES), lambda i: (0, i // nt, 0, i % nt, 0)),
        pl.BlockSpec((REST_SLABS, tm, LANES), lambda i: (0, i, 0)),
    ]
    return pl.pallas_call(
        _inproj_kernel,
        grid=(n_steps,),
        in_specs=in_specs,
        out_specs=out_specs,
        out_shape=out_shape,
        scratch_shapes=[
            pltpu.VMEM((D_MODEL // LANES, tm, LANES), F32),
            pltpu.VMEM((len(DILATIONS), tm, D_MODEL), BF16),
            pltpu.VMEM((len(DILATIONS) - 1, 2, tm, LANES), F32),
        ],
        compiler_params=pltpu.CompilerParams(
            dimension_semantics=("arbitrary",), vmem_limit_bytes=V7X_VMEM_LIMIT_BYTES),
        name="in_projection",
    )(x2d, g, w, *tabs)


def _pitch(d):
    return d + 1 if d % 8 == 0 else d


def _attn_scores(q, kk, bias, qk_first_head):
    zero = jnp.zeros_like(q)
    q2 = jnp.concatenate([jnp.where(qk_first_head, q, zero), jnp.where(qk_first_head, zero, q)], axis=0)
    return lax.dot_general(q2, kk, (((1,), (1,)), ((), ())), preferred_element_type=F32) + bias


def _attn_partials(s, vv, v_first_head):
    nq = s.shape[0] // 2
    m = jnp.max(s, axis=1, keepdims=True)
    p = jnp.exp2(s - m)
    l = jnp.sum(p, axis=1, keepdims=True)
    pv = jnp.dot(p.astype(BF16), vv, preferred_element_type=F32)
    acc = jnp.where(v_first_head, pv[:nq], pv[nq:])
    den = jnp.where(v_first_head, l[:nq], l[nq:])
    mm = jnp.where(v_first_head, m[:nq], m[nq:])
    return acc, den, mm


def _attn_kernel(q0, k0, v0, q1, k1, v1, q2, k2, v2, o_ref,
                 acc0, den0, max0, acc1, den1, max1, acc2, den2, max2):
    blk = BAND
    seq = q0.shape[0]
    lane = lax.broadcasted_iota(jnp.int32, (1, LANES), 1)
    v_first = lane < HEAD_DIM
    qk_first = v_first
    qi = lax.broadcasted_iota(jnp.int32, (2 * blk, 2 * blk), 0) & (blk - 1)
    kj = lax.broadcasted_iota(jnp.int32, (2 * blk, 2 * blk), 1)
    band_bias = jnp.where((kj >= qi) & (kj <= qi + BAND), 0.0, MASK_VALUE).astype(F32)
    qi1 = lax.broadcasted_iota(jnp.int32, (2 * blk, blk), 0) & (blk - 1)
    kj1 = lax.broadcasted_iota(jnp.int32, (2 * blk, blk), 1)
    first_bias = jnp.where(kj1 <= qi1, 0.0, MASK_VALUE).astype(F32)

    def key_rows(n):
        if isinstance(n, int) and n == 0:
            return pl.ds(0, blk), first_bias
        st = n * blk if isinstance(n, int) else pl.multiple_of(n * blk, blk)
        return pl.ds(st - blk, 2 * blk), band_bias

    def query_rows(n):
        return pl.ds(n * blk if isinstance(n, int) else pl.multiple_of(n * blk, blk), blk)

    def run_blocks(blocks, emit):
        scores = []
        for i in range(len(blocks) + ATT_SKEW):
            if i < len(blocks):
                qr, kr, _, n = blocks[i]
                rows, bias = key_rows(n)
                scores.append(_attn_scores(qr[query_rows(n), :], kr[rows, :], bias, qk_first))
            if i >= ATT_SKEW:
                j = i - ATT_SKEW
                _, _, vr, n = blocks[j]
                rows, _ = key_rows(n)
                emit(j, _attn_partials(scores[j], vr[rows, :], v_first))
                scores[j] = None

    def dilated_group(qr, kr, vr, outs, d):
        nb = seq // d // blk
        subs = max(1, ATT_BODY_BLOCKS // nb)

        def body(j, carry):
            blocks, where = [], []
            for ri in range(subs):
                r = j * subs + ri
                for n in range(nb):
                    blocks.append((qr.at[r], kr.at[r], vr.at[r], n))
                    where.append(pl.ds(n * (blk * _pitch(d)) + r, blk, stride=_pitch(d)))

            def emit(i, parts):
                for ref, val in zip(outs, parts):
                    ref[where[i], :] = val

            run_blocks(blocks, emit)
            return carry

        lax.fori_loop(0, d // subs, body, 0)

    dilated_group(q1, k1, v1, (acc1, den1, max1), DILATIONS[1])
    dilated_group(q2, k2, v2, (acc2, den2, max2), DILATIONS[2])

    def dense_body(j, carry):
        ns = [j * ATT_BODY_BLOCKS + i for i in range(ATT_BODY_BLOCKS)]

        def emit(i, parts):
            for ref, val in zip((acc0, den0, max0), parts):
                ref[query_rows(ns[i]), :] = val

        run_blocks([(q0, k0, v0, n) for n in ns], emit)
        return carry

    dense_body(0, 0)
    lax.fori_loop(1, seq // blk // ATT_BODY_BLOCKS, dense_body, 0)

    def merge(t, carry):
        rows = pl.ds(pl.multiple_of(t * MERGE_ROWS, MERGE_ROWS), MERGE_ROWS)

        def padded(ref, d):
            if _pitch(d) == d:
                return ref[rows, :]
            per = MERGE_ROWS // d
            return jnp.concatenate(
                [ref[pl.ds((t * per + i) * _pitch(d), d), :] for i in range(per)], axis=0)

        d2 = DILATIONS[2]
        m0, m1, m2 = max0[rows, :], max1[rows, :], padded(max2, d2)
        mx = jnp.maximum(m0, jnp.maximum(m1, m2))
        e0, e1, e2 = jnp.exp2(m0 - mx), jnp.exp2(m1 - mx), jnp.exp2(m2 - mx)
        num = e0 * acc0[rows, :] + e1 * acc1[rows, :] + e2 * padded(acc2, d2)
        den = e0 * den0[rows, :] + e1 * den1[rows, :] + e2 * padded(den2, d2)
        o_ref[rows, :] = (num * (1.0 / den)).astype(o_ref.dtype)
        return carry

    lax.fori_loop(0, seq // MERGE_ROWS, merge, 0, unroll=2)


def _dilated_attention(a0, a1, a2, batch, seq):
    d1, d2 = DILATIONS[1], DILATIONS[2]
    specs0 =[pl.BlockSpec((None, seq, LANES), functools.partial(lambda b, p, k: (k * PAIRS + p, b, 0), k=k))
              for k in range(3)]
    specs1 = [pl.BlockSpec((None, None, d1, seq // d1, LANES),
                           functools.partial(lambda b, p, k: (k * PAIRS + p, b, 0, 0, 0), k=k))
              for k in range(3)]
    specs2 = [pl.BlockSpec((None, None, d2, seq // d2, LANES),
                           functools.partial(lambda b, p, k: (k * PAIRS + p, b, 0, 0, 0), k=k))
              for k in range(3)]
    in_specs = specs0 + specs1 + specs2
    return pl.pallas_call(
        _attn_kernel,
        grid=(batch, PAIRS),
        in_specs=in_specs,
        out_specs=pl.BlockSpec((None, seq, LANES), lambda b, p: (p, b, 0)),
        out_shape=jax.ShapeDtypeStruct((PAIRS, batch * seq, LANES), BF16),
        scratch_shapes=[pltpu.VMEM((seq // d * _pitch(d), LANES), F32) for d in DILATIONS for _ in range(3)],
        compiler_params=pltpu.CompilerParams(
            dimension_semantics=("arbitrary", "arbitrary"), vmem_limit_bytes=V7X_VMEM_LIMIT_BYTES),
        name="dilated_attention",
    )(a0, a0, a0, a1, a1, a1, a2, a2, a2)


def _retention_kernel(q_ref, k_ref, v_ref, dec_ref, zeta_ref, xi_ref, cd_ref, o_ref, kv_scr, state_scr):
    c_len = RET_CHUNK
    n_chunks = q_ref.shape[0] // c_len
    dec = dec_ref[...]
    zeta = zeta_ref[...]
    xi = xi_ref[...]
    xi2 = jnp.concatenate([xi, xi], axis=1)
    cd = cd_ref[...]
    cd2 = jnp.concatenate([cd, cd], axis=1)

    def chunk_rows(c):
        return pl.ds(pl.multiple_of(c * c_len, c_len), c_len)

    def values(rows):
        return jnp.concatenate([v_ref[0, rows, :], v_ref[1, rows, :]], axis=1)

    def chunk_kv(c, carry):
        rows = chunk_rows(c)
        kz = (k_ref[rows, :].astype(F32) * zeta).astype(BF16)
        kv_scr[c] = lax.dot_general(kz, values(rows), (((0,), (0,)), ((), ())),
                                    preferred_element_type=F32)
        return carry

    lax.fori_loop(0, n_chunks, chunk_kv, 0, unroll=RET_UNROLL)

    def chunk_state(c, state):
        state_scr[c] = state.astype(BF16)
        return state * cd2 + kv_scr[c]

    lax.fori_loop(0, n_chunks, chunk_state, jnp.zeros((RET_KEY_DIM, RET_VALUE_DIM), F32))

    def chunk_group_out(j, carry):
        chunks = [j * RET_UNROLL + i for i in range(RET_UNROLL)]
        scores = []
        for c in chunks:
            rows = chunk_rows(c)
            a = lax.dot_general(q_ref[rows, :], k_ref[rows, :], (((1,), (1,)), ((), ())),
                                preferred_element_type=F32) * dec
            scores.append(a.astype(BF16))
        for c, a in zip(chunks, scores):
            rows = chunk_rows(c)
            o = jnp.dot(a, values(rows), preferred_element_type=F32)
            o = o + jnp.dot(q_ref[rows, :], state_scr[c], preferred_element_type=F32) * xi2
            o_ref[0, rows, :] = o[:, :LANES].astype(o_ref.dtype)
            o_ref[1, rows, :] = o[:, LANES:].astype(o_ref.dtype)
        return carry

    lax.fori_loop(0, n_chunks // RET_UNROLL, chunk_group_out, 0)


def _retention(rest, tabs, batch, seq):
    dec, zeta, xi, cd = tabs
    vs = RET_VALUE_DIM // LANES
    tab_spec = pl.BlockSpec((None, RET_CHUNK, LANES), lambda b, h: (h, 0, 0))
    in_specs = [
        pl.BlockSpec((None, seq, LANES), lambda b, h: (REST_QR + h, b, 0)),
        pl.BlockSpec((None, seq, LANES), lambda b, h: (REST_KR + h, b, 0)),
        pl.BlockSpec((vs, seq, LANES), lambda b, h: (REST_VR // vs + h, b, 0)),
        tab_spec, tab_spec, tab_spec,
        pl.BlockSpec((None, 1, LANES), lambda b, h: (h, 0, 0)),
    ]
    return pl.pallas_call(
        _retention_kernel,
        grid=(batch, RET_HEADS),
        in_specs=in_specs,
        out_specs=pl.BlockSpec((vs, seq, LANES), lambda b, h: (h, b, 0)),
        out_shape=jax.ShapeDtypeStruct((RET_V_WIDTH // LANES, batch * seq, LANES), BF16),
        scratch_shapes=[pltpu.VMEM((seq // RET_CHUNK, RET_KEY_DIM, RET_VALUE_DIM), F32),
                        pltpu.VMEM((seq // RET_CHUNK, RET_KEY_DIM, RET_VALUE_DIM), BF16)],
        compiler_params=pltpu.CompilerParams(
            dimension_semantics=("arbitrary", "arbitrary"), vmem_limit_bytes=V7X_VMEM_LIMIT_BYTES),
        name="retention",
    )(rest, rest, rest, dec, zeta, xi, cd)


def _slabs(ref, lo=0, hi=None):
    hi = ref.shape[0] if hi is None else hi
    return jnp.concatenate([ref[k] for k in range(lo, hi)], axis=1)


def _tail_kernel(oa_ref, ret_ref, sg_ref, ga_ref, gr_ref, x_ref, wa_ref, wr_ref, wo_ref, g2_ref,
                 wg_ref, wu_ref, wd_ref, gf_ref, out_ref, ret_scr, act_scr):
    vs = RET_VALUE_DIM // LANES
    for h in range(RET_HEADS):
        o = _slabs(ret_ref, h * vs, (h + 1) * vs).astype(F32)
        oc = o - jnp.mean(o, axis=-1, keepdims=True)
        var = jnp.mean(oc * oc, axis=-1, keepdims=True)
        gate = _slabs(sg_ref, h * vs, (h + 1) * vs).astype(F32)
        gated = oc * lax.rsqrt(var + NORM_EPS) * (gate * jax.nn.sigmoid(gate))
        ret_scr[:, h * RET_VALUE_DIM:(h + 1) * RET_VALUE_DIM] = gated.astype(BF16)
    ya = jnp.dot(_slabs(oa_ref), wa_ref[...], preferred_element_type=F32)
    yr = jnp.dot(ret_scr[...], wr_ref[...], preferred_element_type=F32)
    merged = (jax.nn.sigmoid(_slabs(ga_ref).astype(F32)) * ya
              + jax.nn.sigmoid(_slabs(gr_ref).astype(F32)) * yr)
    x1 = x_ref[...] + jnp.dot(merged.astype(BF16), wo_ref[...], preferred_element_type=F32)
    h2 = _rms(x1, g2_ref[...]).astype(BF16)
    for c in range(0, FFN_HIDDEN, FFN_CHUNK):
        gate = jnp.dot(h2, wg_ref[:, c:c + FFN_CHUNK], preferred_element_type=F32)
        up = jnp.dot(h2, wu_ref[:, c:c + FFN_CHUNK], preferred_element_type=F32)
        act_scr[:, c:c + FFN_CHUNK] = (gate * jax.nn.sigmoid(gate) * up).astype(BF16)
    down = jnp.dot(act_scr[...], wd_ref[...], preferred_element_type=F32)
    out_ref[...] = _rms(x1 + down, gf_ref[...])


def _tail(oa, ret, rest, x2d, wa, wr, wo, g2, wg, wu, wd, gf):
    t = x2d.shape[0]
    tm = TM_OUT
    gs = D_MODEL // LANES
    full = lambda shape: pl.BlockSpec(shape, lambda i: (0,) * len(shape), pipeline_mode=pl.Buffered(1))
    vec = pl.BlockSpec((1, D_MODEL), lambda i: (0, 0))
    row = pl.BlockSpec((tm, D_MODEL), lambda i: (i, 0))
    in_specs = [
        pl.BlockSpec((PAIRS, tm, LANES), lambda i: (0, i, 0)),
        pl.BlockSpec((gs, tm, LANES), lambda i: (0, i, 0)),
        pl.BlockSpec((gs, tm, LANES), lambda i: (REST_GR // gs, i, 0)),
        pl.BlockSpec((gs, tm, LANES), lambda i: (REST_GA // gs, i, 0)),
        pl.BlockSpec((gs, tm, LANES), lambda i: (REST_GRT // gs, i, 0)),
        row,
        full((GROUP_WIDTH, D_MODEL)), full((RET_V_WIDTH, D_MODEL)), full((D_MODEL, D_MODEL)), vec,
        full((D_MODEL, FFN_HIDDEN)), full((D_MODEL, FFN_HIDDEN)), full((FFN_HIDDEN, D_MODEL)), vec,
    ]
    return pl.pallas_call(
        _tail_kernel,
        grid=(t // tm,),
        in_specs=in_specs,
        out_specs=row,
        out_shape=jax.ShapeDtypeStruct((t, D_MODEL), F32),
        scratch_shapes=[pltpu.VMEM((tm, RET_V_WIDTH), BF16), pltpu.VMEM((tm, FFN_HIDDEN), BF16)],
        compiler_params=pltpu.CompilerParams(
            dimension_semantics=("arbitrary",), vmem_limit_bytes=V7X_VMEM_LIMIT_BYTES),
        name="merge_ffn",
    )(oa, ret, rest, rest, rest, x2d, wa, wr, wo, g2, wg, wu, wd, gf)


def _prep_w_in(w):
    scale = np.ones((1, IN_WIDTH), np.float32)
    scale[:, :ATT_WIDTH] = HEAD_DIM ** -0.5 * LOG2_E
    return (w * scale).astype(BF16)


def _position_tables(seq):
    pos = np.arange(seq, dtype=np.float64)
    inv = ROPE_THETA ** (-np.arange(0, HEAD_DIM, 2, dtype=np.float64) / HEAD_DIM)
    ang = pos[:, None] * inv[None, :]
    c, s = np.cos(ang), np.sin(ang)
    cos_a = np.concatenate([c, c, c, c], axis=1)
    sin_a = np.concatenate([-s, s, -s, s], axis=1)
    base = 1.0 / (ROPE_THETA ** np.linspace(0.0, 1.0, RET_KEY_DIM // 2, dtype=np.float64))
    ang_r = pos[:, None] * base[None, :]
    cr, sr = np.cos(ang_r), np.sin(ang_r)
    cos_r = np.repeat(cr, 2, axis=1)
    sin_r = np.repeat(sr, 2, axis=1) * np.tile([-1.0, 1.0], RET_KEY_DIM // 2)[None, :]
    return [jnp.asarray(t.astype(np.float32)) for t in (cos_a, sin_a, cos_r, sin_r)]


def _retention_tables():
    c = RET_CHUNK
    log_g = np.log1p(-(2.0 ** (-5.0 - np.arange(RET_HEADS, dtype=np.float64))))
    idx = np.arange(c, dtype=np.float64)
    diff = idx[:, None] - idx[None, :]
    decay = np.where(diff[None] >= 0, np.exp(np.maximum(diff, 0.0)[None] * log_g[:, None, None]), 0.0)
    zeta = np.exp((c - 1 - idx)[None, :] * log_g[:, None])
    xi = np.exp((idx + 1.0)[None, :] * log_g[:, None])
    chunk_decay = np.exp(c * log_g)
    bcast = lambda v: np.broadcast_to(v[:, :, None], (RET_HEADS, c, LANES))
    cd = np.broadcast_to(chunk_decay[:, None, None], (RET_HEADS, 1, LANES))
    return tuple(jnp.asarray(np.ascontiguousarray(t).astype(np.float32))
                 for t in (decay, bcast(zeta), bcast(xi), cd))


def kernel(x, norm_mix_g, w_in, w_out_attn, w_out_ret, w_out, norm_ffn_g, w_ffn_gate, w_ffn_up,
           w_ffn_down, norm_final_g):
    batch, seq, _ = x.shape
    assert w_in.shape[0] == 1, "single-layer problem"
    assert seq % (TM_IN * 1) == 0 and TM_IN % (DILATIONS[-1] * 16) == 0
    x2d = x.reshape(batch * seq, D_MODEL)
    a0, a1, a2, rest = _in_projection(
        x2d, norm_mix_g[0][None, :], _prep_w_in(w_in[0]), _position_tables(seq), batch, seq)
    oa = _dilated_attention(a0, a1, a2, batch, seq)
    ret = _retention(rest, _retention_tables(), batch, seq)
    out = _tail(oa, ret, rest, x2d, w_out_attn[0].astype(BF16), w_out_ret[0].astype(BF16),
                w_out[0].astype(BF16), norm_ffn_g[0][None, :], w_ffn_gate[0].astype(BF16),
                w_ffn_up[0].astype(BF16), w_ffn_down[0].astype(BF16), norm_final_g[None, :])
    return out.reshape(batch, seq, D_MODEL)
```

```python
import functools

import jax
import jax.numpy as jnp
import numpy as np
from jax import lax
from jax.experimental import pallas as pl
from jax.experimental.pallas import tpu as pltpu

F32 = jnp.float32
BF16 = jnp.bfloat16

LANES = 128
V7X_VMEM_LIMIT_BYTES = 60 * 1024 * 1024

D_MODEL = 1024
HEAD_DIM = 64
HEADS_PER_GROUP = 8
DILATIONS = (1, 4, 16)
BAND = 128
GROUP_WIDTH = HEADS_PER_GROUP * HEAD_DIM
ATT_WIDTH = len(DILATIONS) * GROUP_WIDTH
PAIRS = GROUP_WIDTH // LANES
ROPE_THETA = 10000.0

RET_HEADS = 4
RET_KEY_DIM = 128
RET_VALUE_DIM = 256
RET_QK_WIDTH = RET_HEADS * RET_KEY_DIM
RET_V_WIDTH = RET_HEADS * RET_VALUE_DIM
RET_CHUNK = 128

FFN_HIDDEN = 2816
NORM_EPS = 1e-6
MASK_VALUE = -1e30

IN_WIDTH = 3 * ATT_WIDTH + 2 * RET_QK_WIDTH + 2 * RET_V_WIDTH + 2 * D_MODEL
COL_TILE = 512
ATT_TILES = 3 * ATT_WIDTH // COL_TILE
REST_SLABS = (IN_WIDTH - 3 * ATT_WIDTH) // LANES
REST_QR, REST_KR, REST_VR, REST_GR, REST_GA, REST_GRT = 0, 4, 8, 16, 24, 32

TM_IN = 512
TM_OUT = 512
FFN_CHUNK = 256
ATT_BODY_BLOCKS = 16
MERGE_ROWS = 64
ATT_SKEW = 2
LOG2_E = 1.4426950408889634
RET_UNROLL = 16


def _rms(x, g):
    ms = jnp.mean(x * x, axis=-1, keepdims=True)
    return x * lax.rsqrt(ms + NORM_EPS) * g


def _rotate(t, cos, sin_signed, span):
    lane = lax.broadcasted_iota(jnp.int32, (1, LANES), 1)
    low = (lane % (2 * span)) < span
    partner = jnp.where(low, pltpu.roll(t, LANES - span, 1), pltpu.roll(t, span, 1))
    return t * cos + partner * sin_signed


def _inproj_kernel(x_ref, g_ref, w_ref, ca, sa, cr, sr,
                   a0_ref, a1_ref, a2_ref, rest_ref, y_scr, h_scr, tab_scr):
    tm = x_ref.shape[0]
    n_slabs = D_MODEL // LANES

    def stage_rows(x_ref, dst):
        y = _rms(x_ref[...], g_ref[...])
        dst[0] = y.astype(BF16)
        for k in range(n_slabs):
            y_scr[k] = y[:, k * LANES:(k + 1) * LANES]
        for gi, d in enumerate(DILATIONS):
            if d == 1:
                continue
            n = tm // d
            for r in range(d):
                for k in range(n_slabs):
                    dst[gi, r * n:(r + 1) * n, k * LANES:(k + 1) * LANES] = (
                        y_scr[k, pl.ds(r, n, stride=d), :].astype(BF16))

    stage_rows(x_ref, h_scr)

    for gi, d in enumerate(DILATIONS):
        if d == 1:
            continue
        n = tm // d
        for r in range(d):
            tab_scr[gi - 1, 0, r * n:(r + 1) * n, :] = ca[pl.ds(r, n, stride=d), :]
            tab_scr[gi - 1, 1, r * n:(r + 1) * n, :] = sa[pl.ds(r, n, stride=d), :]

    def att_tables(g):
        if g == 0:
            return ca[...], sa[...]
        return tab_scr[g - 1, 0], tab_scr[g - 1, 1]

    def write_att(g, slab, val):
        v = val.astype(BF16)
        d = DILATIONS[g]
        n = tm // d
        if g == 0:
            a0_ref[slab] = v
        else:
            out = a1_ref if g == 1 else a2_ref
            for r in range(d):
                out[slab, r] = v[r * n:(r + 1) * n, :]

    def att_tile(c):
        kind, g = divmod(c, 3)
        res = jnp.dot(h_scr[g], w_ref[:, c * COL_TILE:(c + 1) * COL_TILE],
                      preferred_element_type=F32)
        for k in range(PAIRS):
            s = res[:, k * LANES:(k + 1) * LANES]
            if kind < 2:
                s = _rotate(s, *att_tables(g), span=HEAD_DIM // 2)
            write_att(g, kind * PAIRS + k, s)

    def rest_tile(c):
        col = 3 * ATT_WIDTH + c * COL_TILE
        res = jnp.dot(h_scr[0], w_ref[:, col:col + COL_TILE], preferred_element_type=F32)
        for k in range(COL_TILE // LANES):
            slab = c * (COL_TILE // LANES) + k
            s = res[:, k * LANES:(k + 1) * LANES]
            if slab < REST_VR:
                s = _rotate(s, cr[...], sr[...], span=1)
                if slab >= REST_KR:
                    s = s * (RET_KEY_DIM ** -0.5)
            rest_ref[slab] = s.astype(BF16)

    for c in range(REST_SLABS * LANES // COL_TILE):
        rest_tile(c)
    for g in range(len(DILATIONS)):
        for kind in range(3):
            att_tile(kind * 3 + g)


def _in_projection(x2d, g, w, tabs, batch, seq):
    t = x2d.shape[0]
    tm = TM_IN
    nt = seq // tm
    n_steps = t // tm
    tab_spec = pl.BlockSpec((tm, LANES), lambda i: (i % nt, 0))
    in_specs = [
        pl.BlockSpec((tm, D_MODEL), lambda i: (i, 0)),
        pl.BlockSpec((1, D_MODEL), lambda i: (0, 0)),
        pl.BlockSpec((D_MODEL, IN_WIDTH), lambda i: (0, 0), pipeline_mode=pl.Buffered(1)),
    ] + [tab_spec] * 4
    n_att = 3 * PAIRS
    d1, d2 = DILATIONS[1], DILATIONS[2]
    out_shape = [
        jax.ShapeDtypeStruct((n_att, t, LANES), BF16),
        jax.ShapeDtypeStruct((n_att, batch, d1, seq // d1, LANES), BF16),
        jax.ShapeDtypeStruct((n_att, batch, d2, seq // d2, LANES), BF16),
        jax.ShapeDtypeStruct((REST_SLABS, t, LANES), BF16),
    ]
    out_specs = [
        pl.BlockSpec((n_att, tm, LANES), lambda i: (0, i, 0)),
        pl.BlockSpec((n_att, None, d1, tm // d1, LANES), lambda i: (0, i // nt, 0, i % nt, 0)),
        pl.BlockSpec((n_att, None, d2, tm // d2, LANES), lambda i: (0, i // nt, 0, i % nt, 0)),
        pl.BlockSpec((REST_SLABS, tm, LANES), lambda i: (0, i, 0)),
    ]
    return pl.pallas_call(
        _inproj_kernel,
        grid=(n_steps,),
        in_specs=in_specs,
        out_specs=out_specs,
        out_shape=out_shape,
        scratch_shapes=[
            pltpu.VMEM((D_MODEL // LANES, tm, LANES), F32),
            pltpu.VMEM((len(DILATIONS), tm, D_MODEL), BF16),
            pltpu.VMEM((len(DILATIONS) - 1, 2, tm, LANES), F32),
        ],
        compiler_params=pltpu.CompilerParams(
            dimension_semantics=("arbitrary",), vmem_limit_bytes=V7X_VMEM_LIMIT_BYTES),
        name="in_projection",
    )(x2d, g, w, *tabs)


def _pitch(d):
    return d + 1 if d % 8 == 0 else d


def _attn_scores(q, kk, bias, qk_first_head):
    zero = jnp.zeros_like(q)
    q2 = jnp.concatenate([jnp.where(qk_first_head, q, zero), jnp.where(qk_first_head, zero, q)], axis=0)
    return lax.dot_general(q2, kk, (((1,), (1,)), ((), ())), preferred_element_type=F32) + bias


def _attn_partials(s, vv, v_first_head):
    nq = s.shape[0] // 2
    m = jnp.max(s, axis=1, keepdims=True)
    p = jnp.exp2(s - m)
    l = jnp.sum(p, axis=1, keepdims=True)
    pv = jnp.dot(p.astype(BF16), vv, preferred_element_type=F32)
    acc = jnp.where(v_first_head, pv[:nq], pv[nq:])
    den = jnp.where(v_first_head, l[:nq], l[nq:])
    mm = jnp.where(v_first_head, m[:nq], m[nq:])
    return acc, den, mm


def _attn_kernel(q0, k0, v0, q1, k1, v1, q2, k2, v2, o_ref,
                 acc0, den0, max0, acc1, den1, max1, acc2, den2, max2):
    blk = BAND
    seq = q0.shape[0]
    lane = lax.broadcasted_iota(jnp.int32, (1, LANES), 1)
    v_first = lane < HEAD_DIM
    qk_first = v_first
    qi = lax.broadcasted_iota(jnp.int32, (2 * blk, 2 * blk), 0) & (blk - 1)
    kj = lax.broadcasted_iota(jnp.int32, (2 * blk, 2 * blk), 1)
    band_bias = jnp.where((kj >= qi) & (kj <= qi + BAND), 0.0, MASK_VALUE).astype(F32)
    qi1 = lax.broadcasted_iota(jnp.int32, (2 * blk, blk), 0) & (blk - 1)
    kj1 = lax.broadcasted_iota(jnp.int32, (2 * blk, blk), 1)
    first_bias = jnp.where(kj1 <= qi1, 0.0, MASK_VALUE).astype(F32)

    def key_rows(n):
        if isinstance(n, int) and n == 0:
            return pl.ds(0, blk), first_bias
        st = n * blk if isinstance(n, int) else pl.multiple_of(n * blk, blk)
        return pl.ds(st - blk, 2 * blk), band_bias

    def query_rows(n):
        return pl.ds(n * blk if isinstance(n, int) else pl.multiple_of(n * blk, blk), blk)

    def run_blocks(blocks, emit):
        scores = []
        for i in range(len(blocks) + ATT_SKEW):
            if i < len(blocks):
                qr, kr, _, n = blocks[i]
                rows, bias = key_rows(n)
                scores.append(_attn_scores(qr[query_rows(n), :], kr[rows, :], bias, qk_first))
            if i >= ATT_SKEW:
                j = i - ATT_SKEW
                _, _, vr, n = blocks[j]
                rows, _ = key_rows(n)
                emit(j, _attn_partials(scores[j], vr[rows, :], v_first))
                scores[j] = None

    def dilated_group(qr, kr, vr, outs, d):
        nb = seq // d // blk
        subs = max(1, ATT_BODY_BLOCKS // nb)

        def body(j, carry):
            blocks, where = [], []
            for ri in range(subs):
                r = j * subs + ri
                for n in range(nb):
                    blocks.append((qr.at[r], kr.at[r], vr.at[r], n))
                    where.append(pl.ds(n * (blk * _pitch(d)) + r, blk, stride=_pitch(d)))

            def emit(i, parts):
                for ref, val in zip(outs, parts):
                    ref[where[i], :] = val

            run_blocks(blocks, emit)
            return carry

        lax.fori_loop(0, d // subs, body, 0)

    dilated_group(q1, k1, v1, (acc1, den1, max1), DILATIONS[1])
    dilated_group(q2, k2, v2, (acc2, den2, max2), DILATIONS[2])

    def dense_body(j, carry):
        ns = [j * ATT_BODY_BLOCKS + i for i in range(ATT_BODY_BLOCKS)]

        def emit(i, parts):
            for ref, val in zip((acc0, den0, max0), parts):
                ref[query_rows(ns[i]), :] = val

        run_blocks([(q0, k0, v0, n) for n in ns], emit)
        return carry

    dense_body(0, 0)
    lax.fori_loop(1, seq // blk // ATT_BODY_BLOCKS, dense_body, 0)

    def merge(t, carry):
        rows = pl.ds(pl.multiple_of(t * MERGE_ROWS, MERGE_ROWS), MERGE_ROWS)

        def padded(ref, d):
            if _pitch(d) == d:
                return ref[rows, :]
            per = MERGE_ROWS // d
            return jnp.concatenate(
                [ref[pl.ds((t * per + i) * _pitch(d), d), :] for i in range(per)], axis=0)

        d2 = DILATIONS[2]
        m0, m1, m2 = max0[rows, :], max1[rows, :], padded(max2, d2)
        mx = jnp.maximum(m0, jnp.maximum(m1, m2))
        e0, e1, e2 = jnp.exp2(m0 - mx), jnp.exp2(m1 - mx), jnp.exp2(m2 - mx)
        num = e0 * acc0[rows, :] + e1 * acc1[rows, :] + e2 * padded(acc2, d2)
        den = e0 * den0[rows, :] + e1 * den1[rows, :] + e2 * padded(den2, d2)
        o_ref[rows, :] = (num * (1.0 / den)).astype(o_ref.dtype)
        return carry

    lax.fori_loop(0, seq // MERGE_ROWS, merge, 0, unroll=2)


def _dilated_attention(a0, a1, a2, batch, seq):
    d1, d2 = DILATIONS[1], DILATIONS[2]
    specs0 =[pl.BlockSpec((None, seq, LANES), functools.partial(lambda b, p, k: (k * PAIRS + p, b, 0), k=k))
              for k in range(3)]
    specs1 = [pl.BlockSpec((None, None, d1, seq // d1, LANES),
                           functools.partial(lambda b, p, k: (k * PAIRS + p, b, 0, 0, 0), k=k))
              for k in range(3)]
    specs2 = [pl.BlockSpec((None, None, d2, seq // d2, LANES),
                           functools.partial(lambda b, p, k: (k * PAIRS + p, b, 0, 0, 0), k=k))
              for k in range(3)]
    in_specs = specs0 + specs1 + specs2
    return pl.pallas_call(
        _attn_kernel,
        grid=(batch, PAIRS),
        in_specs=in_specs,
        out_specs=pl.BlockSpec((None, seq, LANES), lambda b, p: (p, b, 0)),
        out_shape=jax.ShapeDtypeStruct((PAIRS, batch * seq, LANES), BF16),
        scratch_shapes=[pltpu.VMEM((seq // d * _pitch(d), LANES), F32) for d in DILATIONS for _ in range(3)],
        compiler_params=pltpu.CompilerParams(
            dimension_semantics=("arbitrary", "arbitrary"), vmem_limit_bytes=V7X_VMEM_LIMIT_BYTES),
        name="dilated_attention",
    )(a0, a0, a0, a1, a1, a1, a2, a2, a2)


def _retention_kernel(q_ref, k_ref, v_ref, dec_ref, zeta_ref, xi_ref, cd_ref, o_ref, kv_scr, state_scr):
    c_len = RET_CHUNK
    n_chunks = q_ref.shape[0] // c_len
    dec = dec_ref[...]
    zeta = zeta_ref[...]
    xi = xi_ref[...]
    xi2 = jnp.concatenate([xi, xi], axis=1)
    cd = cd_ref[...]
    cd2 = jnp.concatenate([cd, cd], axis=1)

    def chunk_rows(c):
        return pl.ds(pl.multiple_of(c * c_len, c_len), c_len)

    def values(rows):
        return jnp.concatenate([v_ref[0, rows, :], v_ref[1, rows, :]], axis=1)

    def chunk_kv(c, carry):
        rows = chunk_rows(c)
        kz = (k_ref[rows, :].astype(F32) * zeta).astype(BF16)
        kv_scr[c] = lax.dot_general(kz, values(rows), (((0,), (0,)), ((), ())),
                                    preferred_element_type=F32)
        return carry

    lax.fori_loop(0, n_chunks, chunk_kv, 0, unroll=RET_UNROLL)

    def chunk_state(c, state):
        state_scr[c] = state.astype(BF16)
        return state * cd2 + kv_scr[c]

    lax.fori_loop(0, n_chunks, chunk_state, jnp.zeros((RET_KEY_DIM, RET_VALUE_DIM), F32))

    def chunk_group_out(j, carry):
        chunks = [j * RET_UNROLL + i for i in range(RET_UNROLL)]
        scores = []
        for c in chunks:
            rows = chunk_rows(c)
            a = lax.dot_general(q_ref[rows, :], k_ref[rows, :], (((1,), (1,)), ((), ())),
                                preferred_element_type=F32) * dec
            scores.append(a.astype(BF16))
        for c, a in zip(chunks, scores):
            rows = chunk_rows(c)
            lhs = jnp.concatenate([a, q_ref[rows, :]], axis=1)
            rhs = jnp.concatenate([values(rows), state_scr[c]], axis=0)
            o = jnp.dot(lhs, rhs, preferred_element_type=F32) * xi2
            o_ref[0, rows, :] = o[:, :LANES].astype(o_ref.dtype)
            o_ref[1, rows, :] = o[:, LANES:].astype(o_ref.dtype)
        return carry

    lax.fori_loop(0, n_chunks // RET_UNROLL, chunk_group_out, 0)


def _retention(rest, tabs, batch, seq):
    dec, zeta, xi, cd = tabs
    vs = RET_VALUE_DIM // LANES
    tab_spec = pl.BlockSpec((None, RET_CHUNK, LANES), lambda b, h: (h, 0, 0))
    in_specs = [
        pl.BlockSpec((None, seq, LANES), lambda b, h: (REST_QR + h, b, 0)),
        pl.BlockSpec((None, seq, LANES), lambda b, h: (REST_KR + h, b, 0)),
        pl.BlockSpec((vs, seq, LANES), lambda b, h: (REST_VR // vs + h, b, 0)),
        tab_spec, tab_spec, tab_spec,
        pl.BlockSpec((None, 1, LANES), lambda b, h: (h, 0, 0)),
    ]
    return pl.pallas_call(
        _retention_kernel,
        grid=(batch, RET_HEADS),
        in_specs=in_specs,
        out_specs=pl.BlockSpec((vs, seq, LANES), lambda b, h: (h, b, 0)),
        out_shape=jax.ShapeDtypeStruct((RET_V_WIDTH // LANES, batch * seq, LANES), BF16),
        scratch_shapes=[pltpu.VMEM((seq // RET_CHUNK, RET_KEY_DIM, RET_VALUE_DIM), F32),
                        pltpu.VMEM((seq // RET_CHUNK, RET_KEY_DIM, RET_VALUE_DIM), BF16)],
        compiler_params=pltpu.CompilerParams(
            dimension_semantics=("arbitrary", "arbitrary"), vmem_limit_bytes=V7X_VMEM_LIMIT_BYTES),
        name="retention",
    )(rest, rest, rest, dec, zeta, xi, cd)


def _slabs(ref, lo=0, hi=None):
    hi = ref.shape[0] if hi is None else hi
    return jnp.concatenate([ref[k] for k in range(lo, hi)], axis=1)


def _tail_kernel(oa_ref, ret_ref, sg_ref, ga_ref, gr_ref, x_ref, wa_ref, wr_ref, wo_ref, g2_ref,
                 wg_ref, wu_ref, wd_ref, gf_ref, out_ref, ret_scr, act_scr):
    vs = RET_VALUE_DIM // LANES
    for h in range(RET_HEADS):
        o = _slabs(ret_ref, h * vs, (h + 1) * vs).astype(F32)
        oc = o - jnp.mean(o, axis=-1, keepdims=True)
        var = jnp.mean(oc * oc, axis=-1, keepdims=True)
        gate = _slabs(sg_ref, h * vs, (h + 1) * vs).astype(F32)
        gated = oc * lax.rsqrt(var + NORM_EPS) * (gate * jax.nn.sigmoid(gate))
        ret_scr[:, h * RET_VALUE_DIM:(h + 1) * RET_VALUE_DIM] = gated.astype(BF16)
    ya = jnp.dot(_slabs(oa_ref), wa_ref[...], preferred_element_type=F32)
    yr = jnp.dot(ret_scr[...], wr_ref[...], preferred_element_type=F32)
    merged = (jax.nn.sigmoid(_slabs(ga_ref).astype(F32)) * ya
              + jax.nn.sigmoid(_slabs(gr_ref).astype(F32)) * yr)
    x1 = x_ref[...] + jnp.dot(merged.astype(BF16), wo_ref[...], preferred_element_type=F32)
    h2 = _rms(x1, g2_ref[...]).astype(BF16)
    for c in range(0, FFN_HIDDEN, FFN_CHUNK):
        gate = jnp.dot(h2, wg_ref[:, c:c + FFN_CHUNK], preferred_element_type=F32)
        up = jnp.dot(h2, wu_ref[:, c:c + FFN_CHUNK], preferred_element_type=F32)
        act_scr[:, c:c + FFN_CHUNK] = (gate * jax.nn.sigmoid(gate) * up).astype(BF16)
    down = jnp.dot(act_scr[...], wd_ref[...], preferred_element_type=F32)
    out_ref[...] = _rms(x1 + down, gf_ref[...])


def _tail(oa, ret, rest, x2d, wa, wr, wo, g2, wg, wu, wd, gf):
    t = x2d.shape[0]
    tm = TM_OUT
    gs = D_MODEL // LANES
    full = lambda shape: pl.BlockSpec(shape, lambda i: (0,) * len(shape), pipeline_mode=pl.Buffered(1))
    vec = pl.BlockSpec((1, D_MODEL), lambda i: (0, 0))
    row = pl.BlockSpec((tm, D_MODEL), lambda i: (i, 0))
    in_specs = [
        pl.BlockSpec((PAIRS, tm, LANES), lambda i: (0, i, 0)),
        pl.BlockSpec((gs, tm, LANES), lambda i: (0, i, 0)),
        pl.BlockSpec((gs, tm, LANES), lambda i: (REST_GR // gs, i, 0)),
        pl.BlockSpec((gs, tm, LANES), lambda i: (REST_GA // gs, i, 0)),
        pl.BlockSpec((gs, tm, LANES), lambda i: (REST_GRT // gs, i, 0)),
        row,
        full((GROUP_WIDTH, D_MODEL)), full((RET_V_WIDTH, D_MODEL)), full((D_MODEL, D_MODEL)), vec,
        full((D_MODEL, FFN_HIDDEN)), full((D_MODEL, FFN_HIDDEN)), full((FFN_HIDDEN, D_MODEL)), vec,
    ]
    return pl.pallas_call(
        _tail_kernel,
        grid=(t // tm,),
        in_specs=in_specs,
        out_specs=row,
        out_shape=jax.ShapeDtypeStruct((t, D_MODEL), F32),
        scratch_shapes=[pltpu.VMEM((tm, RET_V_WIDTH), BF16), pltpu.VMEM((tm, FFN_HIDDEN), BF16)],
        compiler_params=pltpu.CompilerParams(
            dimension_semantics=("arbitrary",), vmem_limit_bytes=V7X_VMEM_LIMIT_BYTES),
        name="merge_ffn",
    )(oa, ret, rest, rest, rest, x2d, wa, wr, wo, g2, wg, wu, wd, gf)


def _prep_w_in(w):
    scale = np.ones((1, IN_WIDTH), np.float32)
    scale[:, :ATT_WIDTH] = HEAD_DIM ** -0.5 * LOG2_E
    return (w * scale).astype(BF16)


def _position_tables(seq):
    pos = np.arange(seq, dtype=np.float64)
    inv = ROPE_THETA ** (-np.arange(0, HEAD_DIM, 2, dtype=np.float64) / HEAD_DIM)
    ang = pos[:, None] * inv[None, :]
    c, s = np.cos(ang), np.sin(ang)
    cos_a = np.concatenate([c, c, c, c], axis=1)
    sin_a = np.concatenate([-s, s, -s, s], axis=1)
    base = 1.0 / (ROPE_THETA ** np.linspace(0.0, 1.0, RET_KEY_DIM // 2, dtype=np.float64))
    ang_r = pos[:, None] * base[None, :]
    cr, sr = np.cos(ang_r), np.sin(ang_r)
    cos_r = np.repeat(cr, 2, axis=1)
    sin_r = np.repeat(sr, 2, axis=1) * np.tile([-1.0, 1.0], RET_KEY_DIM // 2)[None, :]
    return [jnp.asarray(t.astype(np.float32)) for t in (cos_a, sin_a, cos_r, sin_r)]


def _retention_tables():
    c = RET_CHUNK
    log_g = np.log1p(-(2.0 ** (-5.0 - np.arange(RET_HEADS, dtype=np.float64))))
    idx = np.arange(c, dtype=np.float64)
    diff = idx[:, None] - idx[None, :]
    decay = np.where(diff[None] >= 0, np.exp(-(idx + 1.0)[None, None, :] * log_g[:, None, None]), 0.0)
    zeta = np.exp((c - 1 - idx)[None, :] * log_g[:, None])
    xi = np.exp((idx + 1.0)[None, :] * log_g[:, None])
    chunk_decay = np.exp(c * log_g)
    bcast = lambda v: np.broadcast_to(v[:, :, None], (RET_HEADS, c, LANES))
    cd = np.broadcast_to(chunk_decay[:, None, None], (RET_HEADS, 1, LANES))
    return tuple(jnp.asarray(np.ascontiguousarray(t).astype(np.float32))
                 for t in (decay, bcast(zeta), bcast(xi), cd))


def kernel(x, norm_mix_g, w_in, w_out_attn, w_out_ret, w_out, norm_ffn_g, w_ffn_gate, w_ffn_up,
           w_ffn_down, norm_final_g):
    batch, seq, _ = x.shape
    assert w_in.shape[0] == 1, "single-layer problem"
    assert seq % (TM_IN * 1) == 0 and TM_IN % (DILATIONS[-1] * 16) == 0
    x2d = x.reshape(batch * seq, D_MODEL)
    a0, a1, a2, rest = _in_projection(
        x2d, norm_mix_g[0][None, :], _prep_w_in(w_in[0]), _position_tables(seq), batch, seq)
    oa = _dilated_attention(a0, a1, a2, batch, seq)
    ret = _retention(rest, _retention_tables(), batch, seq)
    out = _tail(oa, ret, rest, x2d, w_out_attn[0].astype(BF16), w_out_ret[0].astype(BF16),
                w_out[0].astype(BF16), norm_ffn_g[0][None, :], w_ffn_gate[0].astype(BF16),
                w_ffn_up[0].astype(BF16), w_ffn_down[0].astype(BF16), norm_final_g[None, :])
    return out.reshape(batch, seq, D_MODEL)
```

```python
import functools

import jax
import jax.numpy as jnp
import numpy as np
from jax import lax
from jax.experimental import pallas as pl
from jax.experimental.pallas import tpu as pltpu

F32 = jnp.float32
BF16 = jnp.bfloat16

LANES = 128
V7X_VMEM_LIMIT_BYTES = 60 * 1024 * 1024

D_MODEL = 1024
HEAD_DIM = 64
HEADS_PER_GROUP = 8
DILATIONS = (1, 4, 16)
BAND = 128
GROUP_WIDTH = HEADS_PER_GROUP * HEAD_DIM
ATT_WIDTH = len(DILATIONS) * GROUP_WIDTH
PAIRS = GROUP_WIDTH // LANES
ROPE_THETA = 10000.0

RET_HEADS = 4
RET_KEY_DIM = 128
RET_VALUE_DIM = 256
RET_QK_WIDTH = RET_HEADS * RET_KEY_DIM
RET_V_WIDTH = RET_HEADS * RET_VALUE_DIM
RET_CHUNK = 128

FFN_HIDDEN = 2816
NORM_EPS = 1e-6
MASK_VALUE = -1e30

IN_WIDTH = 3 * ATT_WIDTH + 2 * RET_QK_WIDTH + 2 * RET_V_WIDTH + 2 * D_MODEL
COL_TILE = 512
ATT_TILES = 3 * ATT_WIDTH // COL_TILE
REST_SLABS = (IN_WIDTH - 3 * ATT_WIDTH) // LANES
REST_QR, REST_KR, REST_VR, REST_GR, REST_GA, REST_GRT = 0, 4, 8, 16, 24, 32

TM_IN = 512
TM_OUT = 512
FFN_CHUNK = 256
ATT_BODY_BLOCKS = 16
MERGE_ROWS = 64
ATT_SKEW = 2
LOG2_E = 1.4426950408889634
RET_UNROLL = 16


def _rms(x, g):
    ms = jnp.mean(x * x, axis=-1, keepdims=True)
    return x * lax.rsqrt(ms + NORM_EPS) * g


def _rotate(t, cos, sin_signed, span):
    lane = lax.broadcasted_iota(jnp.int32, (1, LANES), 1)
    low = (lane % (2 * span)) < span
    partner = jnp.where(low, pltpu.roll(t, LANES - span, 1), pltpu.roll(t, span, 1))
    return t * cos + partner * sin_signed


def _inproj_kernel(x_ref, w_ref, ca, sa, cr, sr,
                   a0_ref, a1_ref, a2_ref, rest_ref, y_scr, h_scr, tab_scr, rs_scr):
    tm = x_ref.shape[0]
    n_slabs = D_MODEL // LANES

    x = x_ref[...]
    h_scr[0] = x.astype(BF16)
    rs_scr[0] = jnp.broadcast_to(lax.rsqrt(jnp.mean(x * x, axis=-1, keepdims=True) + NORM_EPS),
                                 (tm, LANES))
    for k in range(n_slabs):
        y_scr[k] = x[:, k * LANES:(k + 1) * LANES]
    for gi, d in enumerate(DILATIONS):
        if d == 1:
            continue
        n = tm // d
        for r in range(d):
            rs_scr[gi, r * n:(r + 1) * n, :] = rs_scr[0, pl.ds(r, n, stride=d), :]
            for k in range(n_slabs):
                h_scr[gi, r * n:(r + 1) * n, k * LANES:(k + 1) * LANES] = (
                    y_scr[k, pl.ds(r, n, stride=d), :].astype(BF16))

    for gi, d in enumerate(DILATIONS):
        if d == 1:
            continue
        n = tm // d
        for r in range(d):
            tab_scr[gi - 1, 0, r * n:(r + 1) * n, :] = ca[pl.ds(r, n, stride=d), :]
            tab_scr[gi - 1, 1, r * n:(r + 1) * n, :] = sa[pl.ds(r, n, stride=d), :]

    def att_tables(g):
        if g == 0:
            return ca[...], sa[...]
        return tab_scr[g - 1, 0], tab_scr[g - 1, 1]

    def write_att(g, slab, val):
        v = val.astype(BF16)
        d = DILATIONS[g]
        n = tm // d
        if g == 0:
            a0_ref[slab] = v
        else:
            out = a1_ref if g == 1 else a2_ref
            for r in range(d):
                out[slab, r] = v[r * n:(r + 1) * n, :]

    def att_tile(c):
        kind, g = divmod(c, 3)
        res = jnp.dot(h_scr[g], w_ref[:, c * COL_TILE:(c + 1) * COL_TILE],
                      preferred_element_type=F32)
        for k in range(PAIRS):
            s = res[:, k * LANES:(k + 1) * LANES] * rs_scr[g]
            if kind < 2:
                s = _rotate(s, *att_tables(g), span=HEAD_DIM // 2)
            write_att(g, kind * PAIRS + k, s)

    def rest_tile(c):
        col = 3 * ATT_WIDTH + c * COL_TILE
        res = jnp.dot(h_scr[0], w_ref[:, col:col + COL_TILE], preferred_element_type=F32)
        for k in range(COL_TILE // LANES):
            slab = c * (COL_TILE // LANES) + k
            s = res[:, k * LANES:(k + 1) * LANES] * rs_scr[0]
            if slab < REST_VR:
                s = _rotate(s, cr[...], sr[...], span=1)
                if slab >= REST_KR:
                    s = s * (RET_KEY_DIM ** -0.5)
            rest_ref[slab] = s.astype(BF16)

    for c in range(REST_SLABS * LANES // COL_TILE):
        rest_tile(c)
    for g in range(len(DILATIONS)):
        for kind in range(3):
            att_tile(kind * 3 + g)


def _in_projection(x2d, w, tabs, batch, seq):
    t = x2d.shape[0]
    tm = TM_IN
    nt = seq // tm
    n_steps = t // tm
    tab_spec = pl.BlockSpec((tm, LANES), lambda i: (i % nt, 0))
    in_specs = [
        pl.BlockSpec((tm, D_MODEL), lambda i: (i, 0)),
        pl.BlockSpec((D_MODEL, IN_WIDTH), lambda i: (0, 0), pipeline_mode=pl.Buffered(1)),
    ] + [tab_spec] * 4
    n_att = 3 * PAIRS
    d1, d2 = DILATIONS[1], DILATIONS[2]
    out_shape = [
        jax.ShapeDtypeStruct((n_att, t, LANES), BF16),
        jax.ShapeDtypeStruct((n_att, batch, d1, seq // d1, LANES), BF16),
        jax.ShapeDtypeStruct((n_att, batch, d2, seq // d2, LANES), BF16),
        jax.ShapeDtypeStruct((REST_SLABS, t, LANES), BF16),
    ]
    out_specs = [
        pl.BlockSpec((n_att, tm, LANES), lambda i: (0, i, 0)),
        pl.BlockSpec((n_att, None, d1, tm // d1, LANES), lambda i: (0, i // nt, 0, i % nt, 0)),
        pl.BlockSpec((n_att, None, d2, tm // d2, LANES), lambda i: (0, i // nt, 0, i % nt, 0)),
        pl.BlockSpec((REST_SLABS, tm, LANES), lambda i: (0, i, 0)),
    ]
    return pl.pallas_call(
        _inproj_kernel,
        grid=(n_steps,),
        in_specs=in_specs,
        out_specs=out_specs,
        out_shape=out_shape,
        scratch_shapes=[
            pltpu.VMEM((D_MODEL // LANES, tm, LANES), F32),
            pltpu.VMEM((len(DILATIONS), tm, D_MODEL), BF16),
            pltpu.VMEM((len(DILATIONS) - 1, 2, tm, LANES), F32),
            pltpu.VMEM((len(DILATIONS), tm, LANES), F32),
        ],
        compiler_params=pltpu.CompilerParams(
            dimension_semantics=("arbitrary",), vmem_limit_bytes=V7X_VMEM_LIMIT_BYTES),
        name="in_projection",
    )(x2d, w, *tabs)


def _pitch(d):
    return d + 1 if d % 8 == 0 else d


def _attn_scores(q, kk, bias, qk_first_head):
    zero = jnp.zeros_like(q)
    q2 = jnp.concatenate([jnp.where(qk_first_head, q, zero), jnp.where(qk_first_head, zero, q)], axis=0)
    return lax.dot_general(q2, kk, (((1,), (1,)), ((), ())), preferred_element_type=F32) + bias


def _attn_partials(s, vv, v_first_head):
    nq = s.shape[0] // 2
    m = jnp.max(s, axis=1, keepdims=True)
    p = jnp.exp2(s - m)
    l = jnp.sum(p, axis=1, keepdims=True)
    pv = jnp.dot(p.astype(BF16), vv, preferred_element_type=F32)
    acc = jnp.where(v_first_head, pv[:nq], pv[nq:])
    den = jnp.where(v_first_head, l[:nq], l[nq:])
    mm = jnp.where(v_first_head, m[:nq], m[nq:])
    return acc, den, mm


def _attn_kernel(q0, k0, v0, q1, k1, v1, q2, k2, v2, o_ref,
                 acc0, den0, max0, acc1, den1, max1, acc2, den2, max2):
    blk = BAND
    seq = q0.shape[0]
    lane = lax.broadcasted_iota(jnp.int32, (1, LANES), 1)
    v_first = lane < HEAD_DIM
    qk_first = v_first
    qi = lax.broadcasted_iota(jnp.int32, (2 * blk, 2 * blk), 0) & (blk - 1)
    kj = lax.broadcasted_iota(jnp.int32, (2 * blk, 2 * blk), 1)
    band_bias = jnp.where((kj >= qi) & (kj <= qi + BAND), 0.0, MASK_VALUE).astype(F32)
    qi1 = lax.broadcasted_iota(jnp.int32, (2 * blk, blk), 0) & (blk - 1)
    kj1 = lax.broadcasted_iota(jnp.int32, (2 * blk, blk), 1)
    first_bias = jnp.where(kj1 <= qi1, 0.0, MASK_VALUE).astype(F32)

    def key_rows(n):
        if isinstance(n, int) and n == 0:
            return pl.ds(0, blk), first_bias
        st = n * blk if isinstance(n, int) else pl.multiple_of(n * blk, blk)
        return pl.ds(st - blk, 2 * blk), band_bias

    def query_rows(n):
        return pl.ds(n * blk if isinstance(n, int) else pl.multiple_of(n * blk, blk), blk)

    def run_blocks(blocks, emit):
        scores = []
        for i in range(len(blocks) + ATT_SKEW):
            if i < len(blocks):
                qr, kr, _, n = blocks[i]
                rows, bias = key_rows(n)
                scores.append(_attn_scores(qr[query_rows(n), :], kr[rows, :], bias, qk_first))
            if i >= ATT_SKEW:
                j = i - ATT_SKEW
                _, _, vr, n = blocks[j]
                rows, _ = key_rows(n)
                emit(j, _attn_partials(scores[j], vr[rows, :], v_first))
                scores[j] = None

    def dilated_group(qr, kr, vr, outs, d):
        nb = seq // d // blk
        subs = max(1, ATT_BODY_BLOCKS // nb)

        def body(j, carry):
            blocks, where = [], []
            for ri in range(subs):
                r = j * subs + ri
                for n in range(nb):
                    blocks.append((qr.at[r], kr.at[r], vr.at[r], n))
                    where.append(pl.ds(n * (blk * _pitch(d)) + r, blk, stride=_pitch(d)))

            def emit(i, parts):
                for ref, val in zip(outs, parts):
                    ref[where[i], :] = val

            run_blocks(blocks, emit)
            return carry

        lax.fori_loop(0, d // subs, body, 0)

    dilated_group(q1, k1, v1, (acc1, den1, max1), DILATIONS[1])
    dilated_group(q2, k2, v2, (acc2, den2, max2), DILATIONS[2])

    def dense_body(j, carry):
        ns = [j * ATT_BODY_BLOCKS + i for i in range(ATT_BODY_BLOCKS)]

        def emit(i, parts):
            for ref, val in zip((acc0, den0, max0), parts):
                ref[query_rows(ns[i]), :] = val

        run_blocks([(q0, k0, v0, n) for n in ns], emit)
        return carry

    dense_body(0, 0)
    lax.fori_loop(1, seq // blk // ATT_BODY_BLOCKS, dense_body, 0)

    def merge(t, carry):
        rows = pl.ds(pl.multiple_of(t * MERGE_ROWS, MERGE_ROWS), MERGE_ROWS)

        def padded(ref, d):
            if _pitch(d) == d:
                return ref[rows, :]
            per = MERGE_ROWS // d
            return jnp.concatenate(
                [ref[pl.ds((t * per + i) * _pitch(d), d), :] for i in range(per)], axis=0)

        d2 = DILATIONS[2]
        m0, m1, m2 = max0[rows, :], max1[rows, :], padded(max2, d2)
        mx = jnp.maximum(m0, jnp.maximum(m1, m2))
        e0, e1, e2 = jnp.exp2(m0 - mx), jnp.exp2(m1 - mx), jnp.exp2(m2 - mx)
        num = e0 * acc0[rows, :] + e1 * acc1[rows, :] + e2 * padded(acc2, d2)
        den = e0 * den0[rows, :] + e1 * den1[rows, :] + e2 * padded(den2, d2)
        o_ref[rows, :] = (num * (1.0 / den)).astype(o_ref.dtype)
        return carry

    lax.fori_loop(0, seq // MERGE_ROWS, merge, 0, unroll=2)


def _dilated_attention(a0, a1, a2, batch, seq):
    d1, d2 = DILATIONS[1], DILATIONS[2]
    specs0 =[pl.BlockSpec((None, seq, LANES), functools.partial(lambda b, p, k: (k * PAIRS + p, b, 0), k=k))
              for k in range(3)]
    specs1 = [pl.BlockSpec((None, None, d1, seq // d1, LANES),
                           functools.partial(lambda b, p, k: (k * PAIRS + p, b, 0, 0, 0), k=k))
              for k in range(3)]
    specs2 = [pl.BlockSpec((None, None, d2, seq // d2, LANES),
                           functools.partial(lambda b, p, k: (k * PAIRS + p, b, 0, 0, 0), k=k))
              for k in range(3)]
    in_specs = specs0 + specs1 + specs2
    return pl.pallas_call(
        _attn_kernel,
        grid=(batch, PAIRS),
        in_specs=in_specs,
        out_specs=pl.BlockSpec((None, seq, LANES), lambda b, p: (p, b, 0)),
        out_shape=jax.ShapeDtypeStruct((PAIRS, batch * seq, LANES), BF16),
        scratch_shapes=[pltpu.VMEM((seq // d * _pitch(d), LANES), F32) for d in DILATIONS for _ in range(3)],
        compiler_params=pltpu.CompilerParams(
            dimension_semantics=("arbitrary", "arbitrary"), vmem_limit_bytes=V7X_VMEM_LIMIT_BYTES),
        name="dilated_attention",
    )(a0, a0, a0, a1, a1, a1, a2, a2, a2)


def _retention_kernel(q_ref, k_ref, v_ref, dec_ref, zeta_ref, xi_ref, cd_ref, o_ref, kv_scr, state_scr):
    c_len = RET_CHUNK
    n_chunks = q_ref.shape[0] // c_len
    dec = dec_ref[...]
    zeta = zeta_ref[...]
    xi = xi_ref[...]
    xi2 = jnp.concatenate([xi, xi], axis=1)
    cd = cd_ref[...]
    cd2 = jnp.concatenate([cd, cd], axis=1)

    def chunk_rows(c):
        return pl.ds(pl.multiple_of(c * c_len, c_len), c_len)

    def values(rows):
        return jnp.concatenate([v_ref[0, rows, :], v_ref[1, rows, :]], axis=1)

    def chunk_kv(c, carry):
        rows = chunk_rows(c)
        kz = (k_ref[rows, :].astype(F32) * zeta).astype(BF16)
        kv_scr[c] = lax.dot_general(kz, values(rows), (((0,), (0,)), ((), ())),
                                    preferred_element_type=F32)
        return carry

    lax.fori_loop(0, n_chunks, chunk_kv, 0, unroll=RET_UNROLL)

    def chunk_state(c, state):
        state_scr[c] = state.astype(BF16)
        return state * cd2 + kv_scr[c]

    lax.fori_loop(0, n_chunks, chunk_state, jnp.zeros((RET_KEY_DIM, RET_VALUE_DIM), F32))

    def chunk_group_out(j, carry):
        chunks = [j * RET_UNROLL + i for i in range(RET_UNROLL)]
        scores = []
        for c in chunks:
            rows = chunk_rows(c)
            a = lax.dot_general(q_ref[rows, :], k_ref[rows, :], (((1,), (1,)), ((), ())),
                                preferred_element_type=F32) * dec
            scores.append(a.astype(BF16))
        for c, a in zip(chunks, scores):
            rows = chunk_rows(c)
            lhs = jnp.concatenate([a, q_ref[rows, :]], axis=1)
            rhs = jnp.concatenate([values(rows), state_scr[c]], axis=0)
            o = jnp.dot(lhs, rhs, preferred_element_type=F32) * xi2
            o_ref[0, rows, :] = o[:, :LANES].astype(o_ref.dtype)
            o_ref[1, rows, :] = o[:, LANES:].astype(o_ref.dtype)
        return carry

    lax.fori_loop(0, n_chunks // RET_UNROLL, chunk_group_out, 0)


def _retention(rest, tabs, batch, seq):
    dec, zeta, xi, cd = tabs
    vs = RET_VALUE_DIM // LANES
    tab_spec = pl.BlockSpec((None, RET_CHUNK, LANES), lambda b, h: (h, 0, 0))
    in_specs = [
        pl.BlockSpec((None, seq, LANES), lambda b, h: (REST_QR + h, b, 0)),
        pl.BlockSpec((None, seq, LANES), lambda b, h: (REST_KR + h, b, 0)),
        pl.BlockSpec((vs, seq, LANES), lambda b, h: (REST_VR // vs + h, b, 0)),
        tab_spec, tab_spec, tab_spec,
        pl.BlockSpec((None, 1, LANES), lambda b, h: (h, 0, 0)),
    ]
    return pl.pallas_call(
        _retention_kernel,
        grid=(batch, RET_HEADS),
        in_specs=in_specs,
        out_specs=pl.BlockSpec((vs, seq, LANES), lambda b, h: (h, b, 0)),
        out_shape=jax.ShapeDtypeStruct((RET_V_WIDTH // LANES, batch * seq, LANES), BF16),
        scratch_shapes=[pltpu.VMEM((seq // RET_CHUNK, RET_KEY_DIM, RET_VALUE_DIM), F32),
                        pltpu.VMEM((seq // RET_CHUNK, RET_KEY_DIM, RET_VALUE_DIM), BF16)],
        compiler_params=pltpu.CompilerParams(
            dimension_semantics=("arbitrary", "arbitrary"), vmem_limit_bytes=V7X_VMEM_LIMIT_BYTES),
        name="retention",
    )(rest, rest, rest, dec, zeta, xi, cd)


def _slabs(ref, lo=0, hi=None):
    hi = ref.shape[0] if hi is None else hi
    return jnp.concatenate([ref[k] for k in range(lo, hi)], axis=1)


def _tail_kernel(oa_ref, ret_ref, sg_ref, ga_ref, gr_ref, x_ref, wa_ref, wr_ref, wo_ref,
                 wg_ref, wu_ref, wd_ref, gf_ref, out_ref, ret_scr, act_scr):
    vs = RET_VALUE_DIM // LANES
    for h in range(RET_HEADS):
        o = _slabs(ret_ref, h * vs, (h + 1) * vs).astype(F32)
        oc = o - jnp.mean(o, axis=-1, keepdims=True)
        var = jnp.mean(oc * oc, axis=-1, keepdims=True)
        gate = _slabs(sg_ref, h * vs, (h + 1) * vs).astype(F32)
        gated = oc * lax.rsqrt(var + NORM_EPS) * (gate * jax.nn.sigmoid(gate))
        ret_scr[:, h * RET_VALUE_DIM:(h + 1) * RET_VALUE_DIM] = gated.astype(BF16)
    ya = jnp.dot(_slabs(oa_ref), wa_ref[...], preferred_element_type=F32)
    yr = jnp.dot(ret_scr[...], wr_ref[...], preferred_element_type=F32)
    merged = (jax.nn.sigmoid(_slabs(ga_ref).astype(F32)) * ya
              + jax.nn.sigmoid(_slabs(gr_ref).astype(F32)) * yr)
    x1 = x_ref[...] + jnp.dot(merged.astype(BF16), wo_ref[...], preferred_element_type=F32)
    scale = jnp.broadcast_to(lax.rsqrt(jnp.mean(x1 * x1, axis=-1, keepdims=True) + NORM_EPS),
                             (x1.shape[0], FFN_CHUNK))
    h2 = x1.astype(BF16)
    for c in range(0, FFN_HIDDEN, FFN_CHUNK):
        gate = jnp.dot(h2, wg_ref[:, c:c + FFN_CHUNK], preferred_element_type=F32) * scale
        up = jnp.dot(h2, wu_ref[:, c:c + FFN_CHUNK], preferred_element_type=F32) * scale
        act_scr[:, c:c + FFN_CHUNK] = (gate * jax.nn.sigmoid(gate) * up).astype(BF16)
    down = jnp.dot(act_scr[...], wd_ref[...], preferred_element_type=F32)
    out_ref[...] = _rms(x1 + down, gf_ref[...])


def _tail(oa, ret, rest, x2d, wa, wr, wo, wg, wu, wd, gf):
    t = x2d.shape[0]
    tm = TM_OUT
    gs = D_MODEL // LANES
    full = lambda shape: pl.BlockSpec(shape, lambda i: (0,) * len(shape), pipeline_mode=pl.Buffered(1))
    vec = pl.BlockSpec((1, D_MODEL), lambda i: (0, 0))
    row = pl.BlockSpec((tm, D_MODEL), lambda i: (i, 0))
    in_specs = [
        pl.BlockSpec((PAIRS, tm, LANES), lambda i: (0, i, 0)),
        pl.BlockSpec((gs, tm, LANES), lambda i: (0, i, 0)),
        pl.BlockSpec((gs, tm, LANES), lambda i: (REST_GR // gs, i, 0)),
        pl.BlockSpec((gs, tm, LANES), lambda i: (REST_GA // gs, i, 0)),
        pl.BlockSpec((gs, tm, LANES), lambda i: (REST_GRT // gs, i, 0)),
        row,
        full((GROUP_WIDTH, D_MODEL)), full((RET_V_WIDTH, D_MODEL)), full((D_MODEL, D_MODEL)),
        full((D_MODEL, FFN_HIDDEN)), full((D_MODEL, FFN_HIDDEN)), full((FFN_HIDDEN, D_MODEL)), vec,
    ]
    return pl.pallas_call(
        _tail_kernel,
        grid=(t // tm,),
        in_specs=in_specs,
        out_specs=row,
        out_shape=jax.ShapeDtypeStruct((t, D_MODEL), F32),
        scratch_shapes=[pltpu.VMEM((tm, RET_V_WIDTH), BF16), pltpu.VMEM((tm, FFN_HIDDEN), BF16)],
        compiler_params=pltpu.CompilerParams(
            dimension_semantics=("arbitrary",), vmem_limit_bytes=V7X_VMEM_LIMIT_BYTES),
        name="merge_ffn",
    )(oa, ret, rest, rest, rest, x2d, wa, wr, wo, wg, wu, wd, gf)


def _prep_w_in(w, gain):
    scale = np.ones((1, IN_WIDTH), np.float32)
    scale[:, :ATT_WIDTH] = HEAD_DIM ** -0.5 * LOG2_E
    return (w * gain[:, None] * scale).astype(BF16)


def _position_tables(seq):
    pos = np.arange(seq, dtype=np.float64)
    inv = ROPE_THETA ** (-np.arange(0, HEAD_DIM, 2, dtype=np.float64) / HEAD_DIM)
    ang = pos[:, None] * inv[None, :]
    c, s = np.cos(ang), np.sin(ang)
    cos_a = np.concatenate([c, c, c, c], axis=1)
    sin_a = np.concatenate([-s, s, -s, s], axis=1)
    base = 1.0 / (ROPE_THETA ** np.linspace(0.0, 1.0, RET_KEY_DIM // 2, dtype=np.float64))
    ang_r = pos[:, None] * base[None, :]
    cr, sr = np.cos(ang_r), np.sin(ang_r)
    cos_r = np.repeat(cr, 2, axis=1)
    sin_r = np.repeat(sr, 2, axis=1) * np.tile([-1.0, 1.0], RET_KEY_DIM // 2)[None, :]
    return [jnp.asarray(t.astype(np.float32)) for t in (cos_a, sin_a, cos_r, sin_r)]


def _retention_tables():
    c = RET_CHUNK
    log_g = np.log1p(-(2.0 ** (-5.0 - np.arange(RET_HEADS, dtype=np.float64))))
    idx = np.arange(c, dtype=np.float64)
    diff = idx[:, None] - idx[None, :]
    decay = np.where(diff[None] >= 0, np.exp(-(idx + 1.0)[None, None, :] * log_g[:, None, None]), 0.0)
    zeta = np.exp((c - 1 - idx)[None, :] * log_g[:, None])
    xi = np.exp((idx + 1.0)[None, :] * log_g[:, None])
    chunk_decay = np.exp(c * log_g)
    bcast = lambda v: np.broadcast_to(v[:, :, None], (RET_HEADS, c, LANES))
    cd = np.broadcast_to(chunk_decay[:, None, None], (RET_HEADS, 1, LANES))
    return tuple(jnp.asarray(np.ascontiguousarray(t).astype(np.float32))
                 for t in (decay, bcast(zeta), bcast(xi), cd))


def kernel(x, norm_mix_g, w_in, w_out_attn, w_out_ret, w_out, norm_ffn_g, w_ffn_gate, w_ffn_up,
           w_ffn_down, norm_final_g):
    batch, seq, _ = x.shape
    assert w_in.shape[0] == 1, "single-layer problem"
    assert seq % (TM_IN * 1) == 0 and TM_IN % (DILATIONS[-1] * 16) == 0
    x2d = x.reshape(batch * seq, D_MODEL)
    a0, a1, a2, rest = _in_projection(
        x2d, _prep_w_in(w_in[0], norm_mix_g[0]), _position_tables(seq), batch, seq)
    oa = _dilated_attention(a0, a1, a2, batch, seq)
    ret = _retention(rest, _retention_tables(), batch, seq)
    g2 = norm_ffn_g[0][:, None]
    out = _tail(oa, ret, rest, x2d, w_out_attn[0].astype(BF16), w_out_ret[0].astype(BF16),
                w_out[0].astype(BF16), (w_ffn_gate[0] * g2).astype(BF16),
                (w_ffn_up[0] * g2).astype(BF16), w_ffn_down[0].astype(BF16), norm_final_g[None, :])
    return out.reshape(batch, seq, D_MODEL)
```

```python
import functools

import jax
import jax.numpy as jnp
import numpy as np
from jax import lax
from jax.experimental import pallas as pl
from jax.experimental.pallas import tpu as pltpu

F32 = jnp.float32
BF16 = jnp.bfloat16

LANES = 128
V7X_VMEM_LIMIT_BYTES = 60 * 1024 * 1024

D_MODEL = 1024
HEAD_DIM = 64
HEADS_PER_GROUP = 8
DILATIONS = (1, 4, 16)
BAND = 128
GROUP_WIDTH = HEADS_PER_GROUP * HEAD_DIM
ATT_WIDTH = len(DILATIONS) * GROUP_WIDTH
PAIRS = GROUP_WIDTH // LANES
ROPE_THETA = 10000.0

RET_HEADS = 4
RET_KEY_DIM = 128
RET_VALUE_DIM = 256
RET_QK_WIDTH = RET_HEADS * RET_KEY_DIM
RET_V_WIDTH = RET_HEADS * RET_VALUE_DIM
RET_CHUNK = 128

FFN_HIDDEN = 2816
NORM_EPS = 1e-6
MASK_VALUE = -1e30

IN_WIDTH = 3 * ATT_WIDTH + 2 * RET_QK_WIDTH + 2 * RET_V_WIDTH + 2 * D_MODEL
COL_TILE = 512
ATT_TILES = 3 * ATT_WIDTH // COL_TILE
REST_SLABS = (IN_WIDTH - 3 * ATT_WIDTH) // LANES
REST_QR, REST_KR, REST_VR, REST_GR, REST_GA, REST_GRT = 0, 4, 8, 16, 24, 32

TM_IN = 512
TM_OUT = 512
FFN_CHUNK = 256
ATT_BODY_BLOCKS = 32
MERGE_ROWS = 64
ATT_SKEW = 2
LOG2_E = 1.4426950408889634
RET_UNROLL = 16


def _rms(x, g):
    ms = jnp.mean(x * x, axis=-1, keepdims=True)
    return x * lax.rsqrt(ms + NORM_EPS) * g


def _rotate(t, cos, sin_signed, span):
    lane = lax.broadcasted_iota(jnp.int32, (1, LANES), 1)
    low = (lane % (2 * span)) < span
    partner = jnp.where(low, pltpu.roll(t, LANES - span, 1), pltpu.roll(t, span, 1))
    return t * cos + partner * sin_signed


def _inproj_kernel(x_ref, w_ref, ca, sa, cr, sr,
                   a0_ref, a1_ref, a2_ref, rest_ref, y_scr, h_scr, tab_scr, rs_scr):
    tm = x_ref.shape[0]
    n_slabs = D_MODEL // LANES

    x = x_ref[...]
    h_scr[0] = x.astype(BF16)
    rs_scr[0] = jnp.broadcast_to(lax.rsqrt(jnp.mean(x * x, axis=-1, keepdims=True) + NORM_EPS),
                                 (tm, LANES))
    for k in range(n_slabs):
        y_scr[k] = x[:, k * LANES:(k + 1) * LANES]
    for gi, d in enumerate(DILATIONS):
        if d == 1:
            continue
        n = tm // d
        for r in range(d):
            rs_scr[gi, r * n:(r + 1) * n, :] = rs_scr[0, pl.ds(r, n, stride=d), :]
            for k in range(n_slabs):
                h_scr[gi, r * n:(r + 1) * n, k * LANES:(k + 1) * LANES] = (
                    y_scr[k, pl.ds(r, n, stride=d), :].astype(BF16))

    for gi, d in enumerate(DILATIONS):
        if d == 1:
            continue
        n = tm // d
        for r in range(d):
            tab_scr[gi - 1, 0, r * n:(r + 1) * n, :] = ca[pl.ds(r, n, stride=d), :]
            tab_scr[gi - 1, 1, r * n:(r + 1) * n, :] = sa[pl.ds(r, n, stride=d), :]

    def att_tables(g):
        if g == 0:
            return ca[...], sa[...]
        return tab_scr[g - 1, 0], tab_scr[g - 1, 1]

    def write_att(g, slab, val):
        v = val.astype(BF16)
        d = DILATIONS[g]
        n = tm // d
        if g == 0:
            a0_ref[slab] = v
        else:
            out = a1_ref if g == 1 else a2_ref
            for r in range(d):
                out[slab, r] = v[r * n:(r + 1) * n, :]

    def att_tile(c):
        kind, g = divmod(c, 3)
        res = jnp.dot(h_scr[g], w_ref[:, c * COL_TILE:(c + 1) * COL_TILE],
                      preferred_element_type=F32)
        for k in range(PAIRS):
            s = res[:, k * LANES:(k + 1) * LANES] * rs_scr[g]
            if kind < 2:
                s = _rotate(s, *att_tables(g), span=HEAD_DIM // 2)
            write_att(g, kind * PAIRS + k, s)

    def rest_tile(c):
        col = 3 * ATT_WIDTH + c * COL_TILE
        res = jnp.dot(h_scr[0], w_ref[:, col:col + COL_TILE], preferred_element_type=F32)
        for k in range(COL_TILE // LANES):
            slab = c * (COL_TILE // LANES) + k
            s = res[:, k * LANES:(k + 1) * LANES] * rs_scr[0]
            if slab < REST_VR:
                s = _rotate(s, cr[...], sr[...], span=1)
                if slab >= REST_KR:
                    s = s * (RET_KEY_DIM ** -0.5)
            rest_ref[slab] = s.astype(BF16)

    for c in range(REST_SLABS * LANES // COL_TILE):
        rest_tile(c)
    for g in range(len(DILATIONS)):
        for kind in range(3):
            att_tile(kind * 3 + g)


def _in_projection(x2d, w, tabs, batch, seq):
    t = x2d.shape[0]
    tm = TM_IN
    nt = seq // tm
    n_steps = t // tm
    tab_spec = pl.BlockSpec((tm, LANES), lambda i: (i % nt, 0))
    in_specs = [
        pl.BlockSpec((tm, D_MODEL), lambda i: (i, 0)),
        pl.BlockSpec((D_MODEL, IN_WIDTH), lambda i: (0, 0), pipeline_mode=pl.Buffered(1)),
    ] + [tab_spec] * 4
    n_att = 3 * PAIRS
    d1, d2 = DILATIONS[1], DILATIONS[2]
    out_shape = [
        jax.ShapeDtypeStruct((n_att, t, LANES), BF16),
        jax.ShapeDtypeStruct((n_att, batch, d1, seq // d1, LANES), BF16),
        jax.ShapeDtypeStruct((n_att, batch, d2, seq // d2, LANES), BF16),
        jax.ShapeDtypeStruct((REST_SLABS, t, LANES), BF16),
    ]
    out_specs = [
        pl.BlockSpec((n_att, tm, LANES), lambda i: (0, i, 0)),
        pl.BlockSpec((n_att, None, d1, tm // d1, LANES), lambda i: (0, i // nt, 0, i % nt, 0)),
        pl.BlockSpec((n_att, None, d2, tm // d2, LANES), lambda i: (0, i // nt, 0, i % nt, 0)),
        pl.BlockSpec((REST_SLABS, tm, LANES), lambda i: (0, i, 0)),
    ]
    return pl.pallas_call(
        _inproj_kernel,
        grid=(n_steps,),
        in_specs=in_specs,
        out_specs=out_specs,
        out_shape=out_shape,
        scratch_shapes=[
            pltpu.VMEM((D_MODEL // LANES, tm, LANES), F32),
            pltpu.VMEM((len(DILATIONS), tm, D_MODEL), BF16),
            pltpu.VMEM((len(DILATIONS) - 1, 2, tm, LANES), F32),
            pltpu.VMEM((len(DILATIONS), tm, LANES), F32),
        ],
        compiler_params=pltpu.CompilerParams(
            dimension_semantics=("arbitrary",), vmem_limit_bytes=V7X_VMEM_LIMIT_BYTES),
        name="in_projection",
    )(x2d, w, *tabs)


def _pitch(d):
    return d + 1 if d % 8 == 0 else d


def _attn_scores(q, kk, bias, qk_first_head):
    zero = jnp.zeros_like(q)
    q2 = jnp.concatenate([jnp.where(qk_first_head, q, zero), jnp.where(qk_first_head, zero, q)], axis=0)
    return lax.dot_general(q2, kk, (((1,), (1,)), ((), ())), preferred_element_type=F32) + bias


def _attn_partials(s, vv, v_first_head):
    nq = s.shape[0] // 2
    m = jnp.max(s, axis=1, keepdims=True)
    p = jnp.exp2(s - m)
    l = jnp.sum(p, axis=1, keepdims=True)
    pv = jnp.dot(p.astype(BF16), vv, preferred_element_type=F32)
    acc = jnp.where(v_first_head, pv[:nq], pv[nq:])
    den = jnp.where(v_first_head, l[:nq], l[nq:])
    mm = jnp.where(v_first_head, m[:nq], m[nq:])
    return acc, den, mm


def _attn_kernel(q0, k0, v0, q1, k1, v1, q2, k2, v2, o_ref,
                 acc0, den0, max0, acc1, den1, max1, acc2, den2, max2):
    blk = BAND
    seq = q0.shape[0]
    lane = lax.broadcasted_iota(jnp.int32, (1, LANES), 1)
    v_first = lane < HEAD_DIM
    qk_first = v_first
    qi = lax.broadcasted_iota(jnp.int32, (2 * blk, 2 * blk), 0) & (blk - 1)
    kj = lax.broadcasted_iota(jnp.int32, (2 * blk, 2 * blk), 1)
    band_bias = jnp.where((kj >= qi) & (kj <= qi + BAND), 0.0, MASK_VALUE).astype(F32)
    qi1 = lax.broadcasted_iota(jnp.int32, (2 * blk, blk), 0) & (blk - 1)
    kj1 = lax.broadcasted_iota(jnp.int32, (2 * blk, blk), 1)
    first_bias = jnp.where(kj1 <= qi1, 0.0, MASK_VALUE).astype(F32)

    def key_rows(n):
        if isinstance(n, int) and n == 0:
            return pl.ds(0, blk), first_bias
        st = n * blk if isinstance(n, int) else pl.multiple_of(n * blk, blk)
        return pl.ds(st - blk, 2 * blk), band_bias

    def query_rows(n):
        return pl.ds(n * blk if isinstance(n, int) else pl.multiple_of(n * blk, blk), blk)

    def run_blocks(blocks, emit):
        scores = []
        for i in range(len(blocks) + ATT_SKEW):
            if i < len(blocks):
                qr, kr, _, n = blocks[i]
                rows, bias = key_rows(n)
                scores.append(_attn_scores(qr[query_rows(n), :], kr[rows, :], bias, qk_first))
            if i >= ATT_SKEW:
                j = i - ATT_SKEW
                _, _, vr, n = blocks[j]
                rows, _ = key_rows(n)
                emit(j, _attn_partials(scores[j], vr[rows, :], v_first))
                scores[j] = None

    def dilated_group(qr, kr, vr, outs, d):
        nb = seq // d // blk
        subs = max(1, ATT_BODY_BLOCKS // nb)

        def body(j, carry):
            blocks, where = [], []
            for ri in range(subs):
                r = j * subs + ri
                for n in range(nb):
                    blocks.append((qr.at[r], kr.at[r], vr.at[r], n))
                    where.append(pl.ds(n * (blk * _pitch(d)) + r, blk, stride=_pitch(d)))

            def emit(i, parts):
                for ref, val in zip(outs, parts):
                    ref[where[i], :] = val

            run_blocks(blocks, emit)
            return carry

        lax.fori_loop(0, d // subs, body, 0)

    dilated_group(q1, k1, v1, (acc1, den1, max1), DILATIONS[1])
    dilated_group(q2, k2, v2, (acc2, den2, max2), DILATIONS[2])

    def dense_body(j, carry):
        ns = [j * ATT_BODY_BLOCKS + i for i in range(ATT_BODY_BLOCKS)]

        def emit(i, parts):
            for ref, val in zip((acc0, den0, max0), parts):
                ref[query_rows(ns[i]), :] = val

        run_blocks([(q0, k0, v0, n) for n in ns], emit)
        return carry

    dense_body(0, 0)
    lax.fori_loop(1, seq // blk // ATT_BODY_BLOCKS, dense_body, 0)

    def merge(t, carry):
        rows = pl.ds(pl.multiple_of(t * MERGE_ROWS, MERGE_ROWS), MERGE_ROWS)

        def padded(ref, d):
            if _pitch(d) == d:
                return ref[rows, :]
            per = MERGE_ROWS // d
            return jnp.concatenate(
                [ref[pl.ds((t * per + i) * _pitch(d), d), :] for i in range(per)], axis=0)

        d2 = DILATIONS[2]
        m0, m1, m2 = max0[rows, :], max1[rows, :], padded(max2, d2)
        mx = jnp.maximum(m0, jnp.maximum(m1, m2))
        e0, e1, e2 = jnp.exp2(m0 - mx), jnp.exp2(m1 - mx), jnp.exp2(m2 - mx)
        num = e0 * acc0[rows, :] + e1 * acc1[rows, :] + e2 * padded(acc2, d2)
        den = e0 * den0[rows, :] + e1 * den1[rows, :] + e2 * padded(den2, d2)
        o_ref[rows, :] = (num * (1.0 / den)).astype(o_ref.dtype)
        return carry

    lax.fori_loop(0, seq // MERGE_ROWS, merge, 0, unroll=8)


def _dilated_attention(a0, a1, a2, batch, seq):
    d1, d2 = DILATIONS[1], DILATIONS[2]
    specs0 =[pl.BlockSpec((None, seq, LANES), functools.partial(lambda b, p, k: (k * PAIRS + p, b, 0), k=k))
              for k in range(3)]
    specs1 = [pl.BlockSpec((None, None, d1, seq // d1, LANES),
                           functools.partial(lambda b, p, k: (k * PAIRS + p, b, 0, 0, 0), k=k))
              for k in range(3)]
    specs2 = [pl.BlockSpec((None, None, d2, seq // d2, LANES),
                           functools.partial(lambda b, p, k: (k * PAIRS + p, b, 0, 0, 0), k=k))
              for k in range(3)]
    in_specs = specs0 + specs1 + specs2
    return pl.pallas_call(
        _attn_kernel,
        grid=(batch, PAIRS),
        in_specs=in_specs,
        out_specs=pl.BlockSpec((None, seq, LANES), lambda b, p: (p, b, 0)),
        out_shape=jax.ShapeDtypeStruct((PAIRS, batch * seq, LANES), BF16),
        scratch_shapes=[pltpu.VMEM((seq // d * _pitch(d), LANES), F32) for d in DILATIONS for _ in range(3)],
        compiler_params=pltpu.CompilerParams(
            dimension_semantics=("arbitrary", "arbitrary"), vmem_limit_bytes=V7X_VMEM_LIMIT_BYTES),
        name="dilated_attention",
    )(a0, a0, a0, a1, a1, a1, a2, a2, a2)


def _retention_kernel(q_ref, k_ref, v_ref, dec_ref, zeta_ref, xi_ref, cd_ref, o_ref, kv_scr, state_scr):
    c_len = RET_CHUNK
    n_chunks = q_ref.shape[0] // c_len
    dec = dec_ref[...]
    zeta = zeta_ref[...]
    xi = xi_ref[...]
    xi2 = jnp.concatenate([xi, xi], axis=1)
    cd = cd_ref[...]
    cd2 = jnp.concatenate([cd, cd], axis=1)

    def chunk_rows(c):
        return pl.ds(pl.multiple_of(c * c_len, c_len), c_len)

    def values(rows):
        return jnp.concatenate([v_ref[0, rows, :], v_ref[1, rows, :]], axis=1)

    def chunk_kv(c, carry):
        rows = chunk_rows(c)
        kz = (k_ref[rows, :].astype(F32) * zeta).astype(BF16)
        kv_scr[c] = lax.dot_general(kz, values(rows), (((0,), (0,)), ((), ())),
                                    preferred_element_type=F32)
        return carry

    lax.fori_loop(0, n_chunks, chunk_kv, 0, unroll=RET_UNROLL)

    def chunk_state(c, state):
        state_scr[c] = state.astype(BF16)
        return state * cd2 + kv_scr[c]

    lax.fori_loop(0, n_chunks, chunk_state, jnp.zeros((RET_KEY_DIM, RET_VALUE_DIM), F32))

    def chunk_group_out(j, carry):
        chunks = [j * RET_UNROLL + i for i in range(RET_UNROLL)]
        scores = []
        for c in chunks:
            rows = chunk_rows(c)
            a = lax.dot_general(q_ref[rows, :], k_ref[rows, :], (((1,), (1,)), ((), ())),
                                preferred_element_type=F32) * dec
            scores.append(a.astype(BF16))
        for c, a in zip(chunks, scores):
            rows = chunk_rows(c)
            lhs = jnp.concatenate([a, q_ref[rows, :]], axis=1)
            rhs = jnp.concatenate([values(rows), state_scr[c]], axis=0)
            o = jnp.dot(lhs, rhs, preferred_element_type=F32) * xi2
            o_ref[0, rows, :] = o[:, :LANES].astype(o_ref.dtype)
            o_ref[1, rows, :] = o[:, LANES:].astype(o_ref.dtype)
        return carry

    lax.fori_loop(0, n_chunks // RET_UNROLL, chunk_group_out, 0)


def _retention(rest, tabs, batch, seq):
    dec, zeta, xi, cd = tabs
    vs = RET_VALUE_DIM // LANES
    tab_spec = pl.BlockSpec((None, RET_CHUNK, LANES), lambda b, h: (h, 0, 0))
    in_specs = [
        pl.BlockSpec((None, seq, LANES), lambda b, h: (REST_QR + h, b, 0)),
        pl.BlockSpec((None, seq, LANES), lambda b, h: (REST_KR + h, b, 0)),
        pl.BlockSpec((vs, seq, LANES), lambda b, h: (REST_VR // vs + h, b, 0)),
        tab_spec, tab_spec, tab_spec,
        pl.BlockSpec((None, 1, LANES), lambda b, h: (h, 0, 0)),
    ]
    return pl.pallas_call(
        _retention_kernel,
        grid=(batch, RET_HEADS),
        in_specs=in_specs,
        out_specs=pl.BlockSpec((vs, seq, LANES), lambda b, h: (h, b, 0)),
        out_shape=jax.ShapeDtypeStruct((RET_V_WIDTH // LANES, batch * seq, LANES), BF16),
        scratch_shapes=[pltpu.VMEM((seq // RET_CHUNK, RET_KEY_DIM, RET_VALUE_DIM), F32),
                        pltpu.VMEM((seq // RET_CHUNK, RET_KEY_DIM, RET_VALUE_DIM), BF16)],
        compiler_params=pltpu.CompilerParams(
            dimension_semantics=("arbitrary", "arbitrary"), vmem_limit_bytes=V7X_VMEM_LIMIT_BYTES),
        name="retention",
    )(rest, rest, rest, dec, zeta, xi, cd)


def _slabs(ref, lo=0, hi=None):
    hi = ref.shape[0] if hi is None else hi
    return jnp.concatenate([ref[k] for k in range(lo, hi)], axis=1)


def _tail_kernel(oa_ref, ret_ref, sg_ref, ga_ref, gr_ref, x_ref, wa_ref, wr_ref, wo_ref,
                 wg_ref, wu_ref, wd_ref, gf_ref, out_ref, ret_scr, act_scr):
    vs = RET_VALUE_DIM // LANES
    for h in range(RET_HEADS):
        o = _slabs(ret_ref, h * vs, (h + 1) * vs).astype(F32)
        oc = o - jnp.mean(o, axis=-1, keepdims=True)
        var = jnp.mean(oc * oc, axis=-1, keepdims=True)
        gate = _slabs(sg_ref, h * vs, (h + 1) * vs).astype(F32)
        gated = oc * lax.rsqrt(var + NORM_EPS) * (gate * jax.nn.sigmoid(gate))
        ret_scr[:, h * RET_VALUE_DIM:(h + 1) * RET_VALUE_DIM] = gated.astype(BF16)
    ya = jnp.dot(_slabs(oa_ref), wa_ref[...], preferred_element_type=F32)
    yr = jnp.dot(ret_scr[...], wr_ref[...], preferred_element_type=F32)
    merged = (jax.nn.sigmoid(_slabs(ga_ref).astype(F32)) * ya
              + jax.nn.sigmoid(_slabs(gr_ref).astype(F32)) * yr)
    x1 = x_ref[...] + jnp.dot(merged.astype(BF16), wo_ref[...], preferred_element_type=F32)
    scale = jnp.broadcast_to(lax.rsqrt(jnp.mean(x1 * x1, axis=-1, keepdims=True) + NORM_EPS),
                             (x1.shape[0], FFN_CHUNK))
    h2 = x1.astype(BF16)
    for c in range(0, FFN_HIDDEN, FFN_CHUNK):
        gate = jnp.dot(h2, wg_ref[:, c:c + FFN_CHUNK], preferred_element_type=F32) * scale
        up = jnp.dot(h2, wu_ref[:, c:c + FFN_CHUNK], preferred_element_type=F32) * scale
        act_scr[:, c:c + FFN_CHUNK] = (gate * jax.nn.sigmoid(gate) * up).astype(BF16)
    down = jnp.dot(act_scr[...], wd_ref[...], preferred_element_type=F32)
    out_ref[...] = _rms(x1 + down, gf_ref[...])


def _tail(oa, ret, rest, x2d, wa, wr, wo, wg, wu, wd, gf):
    t = x2d.shape[0]
    tm = TM_OUT
    gs = D_MODEL // LANES
    full = lambda shape: pl.BlockSpec(shape, lambda i: (0,) * len(shape), pipeline_mode=pl.Buffered(1))
    vec = pl.BlockSpec((1, D_MODEL), lambda i: (0, 0))
    row = pl.BlockSpec((tm, D_MODEL), lambda i: (i, 0))
    in_specs = [
        pl.BlockSpec((PAIRS, tm, LANES), lambda i: (0, i, 0)),
        pl.BlockSpec((gs, tm, LANES), lambda i: (0, i, 0)),
        pl.BlockSpec((gs, tm, LANES), lambda i: (REST_GR // gs, i, 0)),
        pl.BlockSpec((gs, tm, LANES), lambda i: (REST_GA // gs, i, 0)),
        pl.BlockSpec((gs, tm, LANES), lambda i: (REST_GRT // gs, i, 0)),
        row,
        full((GROUP_WIDTH, D_MODEL)), full((RET_V_WIDTH, D_MODEL)), full((D_MODEL, D_MODEL)),
        full((D_MODEL, FFN_HIDDEN)), full((D_MODEL, FFN_HIDDEN)), full((FFN_HIDDEN, D_MODEL)), vec,
    ]
    return pl.pallas_call(
        _tail_kernel,
        grid=(t // tm,),
        in_specs=in_specs,
        out_specs=row,
        out_shape=jax.ShapeDtypeStruct((t, D_MODEL), F32),
        scratch_shapes=[pltpu.VMEM((tm, RET_V_WIDTH), BF16), pltpu.VMEM((tm, FFN_HIDDEN), BF16)],
        compiler_params=pltpu.CompilerParams(
            dimension_semantics=("arbitrary",), vmem_limit_bytes=V7X_VMEM_LIMIT_BYTES),
        name="merge_ffn",
    )(oa, ret, rest, rest, rest, x2d, wa, wr, wo, wg, wu, wd, gf)


def _prep_w_in(w, gain):
    scale = np.ones((1, IN_WIDTH), np.float32)
    scale[:, :ATT_WIDTH] = HEAD_DIM ** -0.5 * LOG2_E
    return (w * gain[:, None] * scale).astype(BF16)


def _position_tables(seq):
    pos = np.arange(seq, dtype=np.float64)
    inv = ROPE_THETA ** (-np.arange(0, HEAD_DIM, 2, dtype=np.float64) / HEAD_DIM)
    ang = pos[:, None] * inv[None, :]
    c, s = np.cos(ang), np.sin(ang)
    cos_a = np.concatenate([c, c, c, c], axis=1)
    sin_a = np.concatenate([-s, s, -s, s], axis=1)
    base = 1.0 / (ROPE_THETA ** np.linspace(0.0, 1.0, RET_KEY_DIM // 2, dtype=np.float64))
    ang_r = pos[:, None] * base[None, :]
    cr, sr = np.cos(ang_r), np.sin(ang_r)
    cos_r = np.repeat(cr, 2, axis=1)
    sin_r = np.repeat(sr, 2, axis=1) * np.tile([-1.0, 1.0], RET_KEY_DIM // 2)[None, :]
    return [jnp.asarray(t.astype(np.float32)) for t in (cos_a, sin_a, cos_r, sin_r)]


def _retention_tables():
    c = RET_CHUNK
    log_g = np.log1p(-(2.0 ** (-5.0 - np.arange(RET_HEADS, dtype=np.float64))))
    idx = np.arange(c, dtype=np.float64)
    diff = idx[:, None] - idx[None, :]
    decay = np.where(diff[None] >= 0, np.exp(-(idx + 1.0)[None, None, :] * log_g[:, None, None]), 0.0)
    zeta = np.exp((c - 1 - idx)[None, :] * log_g[:, None])
    xi = np.exp((idx + 1.0)[None, :] * log_g[:, None])
    chunk_decay = np.exp(c * log_g)
    bcast = lambda v: np.broadcast_to(v[:, :, None], (RET_HEADS, c, LANES))
    cd = np.broadcast_to(chunk_decay[:, None, None], (RET_HEADS, 1, LANES))
    return tuple(jnp.asarray(np.ascontiguousarray(t).astype(np.float32))
                 for t in (decay, bcast(zeta), bcast(xi), cd))


def kernel(x, norm_mix_g, w_in, w_out_attn, w_out_ret, w_out, norm_ffn_g, w_ffn_gate, w_ffn_up,
           w_ffn_down, norm_final_g):
    batch, seq, _ = x.shape
    assert w_in.shape[0] == 1, "single-layer problem"
    assert seq % (TM_IN * 1) == 0 and TM_IN % (DILATIONS[-1] * 16) == 0
    x2d = x.reshape(batch * seq, D_MODEL)
    a0, a1, a2, rest = _in_projection(
        x2d, _prep_w_in(w_in[0], norm_mix_g[0]), _position_tables(seq), batch, seq)
    oa = _dilated_attention(a0, a1, a2, batch, seq)
    ret = _retention(rest, _retention_tables(), batch, seq)
    g2 = norm_ffn_g[0][:, None]
    out = _tail(oa, ret, rest, x2d, w_out_attn[0].astype(BF16), w_out_ret[0].astype(BF16),
                w_out[0].astype(BF16), (w_ffn_gate[0] * g2).astype(BF16),
                (w_ffn_up[0] * g2).astype(BF16), w_ffn_down[0].astype(BF16), norm_final_g[None, :])
    return out.reshape(batch, seq, D_MODEL)
```

```python
import functools

import jax
import jax.numpy as jnp
import numpy as np
from jax import lax
from jax.experimental import pallas as pl
from jax.experimental.pallas import tpu as pltpu

F32 = jnp.float32
BF16 = jnp.bfloat16

LANES = 128
V7X_VMEM_LIMIT_BYTES = 60 * 1024 * 1024

D_MODEL = 1024
HEAD_DIM = 64
HEADS_PER_GROUP = 8
DILATIONS = (1, 4, 16)
BAND = 128
GROUP_WIDTH = HEADS_PER_GROUP * HEAD_DIM
ATT_WIDTH = len(DILATIONS) * GROUP_WIDTH
PAIRS = GROUP_WIDTH // LANES
ROPE_THETA = 10000.0

RET_HEADS = 4
RET_KEY_DIM = 128
RET_VALUE_DIM = 256
RET_QK_WIDTH = RET_HEADS * RET_KEY_DIM
RET_V_WIDTH = RET_HEADS * RET_VALUE_DIM
RET_CHUNK = 128

FFN_HIDDEN = 2816
NORM_EPS = 1e-6
MASK_VALUE = -1e30

IN_WIDTH = 3 * ATT_WIDTH + 2 * RET_QK_WIDTH + 2 * RET_V_WIDTH + 2 * D_MODEL
COL_TILE = 512
ATT_TILES = 3 * ATT_WIDTH // COL_TILE
REST_SLABS = (IN_WIDTH - 3 * ATT_WIDTH) // LANES
REST_QR, REST_KR, REST_VR, REST_GR, REST_GA, REST_GRT = 0, 4, 8, 16, 24, 32

TM_IN = 512
TM_OUT = 512
FFN_CHUNK = 256
ATT_BODY_BLOCKS = 32
MERGE_ROWS = 64
ATT_SKEW = 2
LOG2_E = 1.4426950408889634
RET_UNROLL = 16


def _rms(x, g):
    ms = jnp.mean(x * x, axis=-1, keepdims=True)
    return x * lax.rsqrt(ms + NORM_EPS) * g


def _rotate(t, cos, sin_signed, span):
    lane = lax.broadcasted_iota(jnp.int32, (1, LANES), 1)
    low = (lane % (2 * span)) < span
    partner = jnp.where(low, pltpu.roll(t, LANES - span, 1), pltpu.roll(t, span, 1))
    return t * cos + partner * sin_signed


W_STAGE_ROWS, W_STAGE_COLS = 1024, 512


def _weight_tiles(shape):
    rows, cols = shape
    return [(r, min(W_STAGE_ROWS, rows - r), c, min(W_STAGE_COLS, cols - c))
            for r in range(0, rows, W_STAGE_ROWS) for c in range(0, cols, W_STAGE_COLS)]


def _load_weights_bf16(jobs, stage, sems):
    def copy(k):
        src, _, (r0, nr, c0, nc), _, _ = jobs[k]
        return pltpu.make_async_copy(src.at[pl.ds(r0, nr), pl.ds(c0, nc)],
                                     stage.at[k % 2, pl.ds(0, nr), pl.ds(0, nc)], sems.at[k % 2])

    copy(0).start()
    for k, (_, dst, (r0, nr, c0, nc), row_gain, scale) in enumerate(jobs):
        if k + 1 < len(jobs):
            copy(k + 1).start()
        copy(k).wait()
        for c in range(0, nc, LANES):
            tile = stage[k % 2, 0:nr, c:c + LANES]
            if row_gain is not None:
                tile = tile * row_gain[r0:r0 + nr, :]
            if scale != 1.0:
                tile = tile * scale
            dst[r0:r0 + nr, c0 + c:c0 + c + LANES] = tile.astype(BF16)


def _inproj_kernel(x_ref, w_hbm, gain_ref, ca, sa, cr, sr,
                   a0_ref, a1_ref, a2_ref, rest_ref, y_scr, h_scr, tab_scr, rs_scr,
                   w_ref, w_stage, w_sems):
    tm = x_ref.shape[0]
    n_slabs = D_MODEL // LANES

    @pl.when(pl.program_id(0) == 0)
    def _():
        q_scale = HEAD_DIM ** -0.5 * LOG2_E
        _load_weights_bf16(
            [(w_hbm, w_ref, t, gain_ref, q_scale if t[2] < ATT_WIDTH else 1.0)
             for t in _weight_tiles((D_MODEL, IN_WIDTH))], w_stage, w_sems)

    x = x_ref[...]
    h_scr[0] = x.astype(BF16)
    rs_scr[0] = jnp.broadcast_to(lax.rsqrt(jnp.mean(x * x, axis=-1, keepdims=True) + NORM_EPS),
                                 (tm, LANES))
    for k in range(n_slabs):
        y_scr[k] = x[:, k * LANES:(k + 1) * LANES]
    for gi, d in enumerate(DILATIONS):
        if d == 1:
            continue
        n = tm // d
        for r in range(d):
            rs_scr[gi, r * n:(r + 1) * n, :] = rs_scr[0, pl.ds(r, n, stride=d), :]
            for k in range(n_slabs):
                h_scr[gi, r * n:(r + 1) * n, k * LANES:(k + 1) * LANES] = (
                    y_scr[k, pl.ds(r, n, stride=d), :].astype(BF16))

    for gi, d in enumerate(DILATIONS):
        if d == 1:
            continue
        n = tm // d
        for r in range(d):
            tab_scr[gi - 1, 0, r * n:(r + 1) * n, :] = ca[pl.ds(r, n, stride=d), :]
            tab_scr[gi - 1, 1, r * n:(r + 1) * n, :] = sa[pl.ds(r, n, stride=d), :]

    def att_tables(g):
        if g == 0:
            return ca[...], sa[...]
        return tab_scr[g - 1, 0], tab_scr[g - 1, 1]

    def write_att(g, slab, val):
        v = val.astype(BF16)
        d = DILATIONS[g]
        n = tm // d
        if g == 0:
            a0_ref[slab] = v
        else:
            out = a1_ref if g == 1 else a2_ref
            for r in range(d):
                out[slab, r] = v[r * n:(r + 1) * n, :]

    def att_tile(c):
        kind, g = divmod(c, 3)
        res = jnp.dot(h_scr[g], w_ref[:, c * COL_TILE:(c + 1) * COL_TILE],
                      preferred_element_type=F32)
        for k in range(PAIRS):
            s = res[:, k * LANES:(k + 1) * LANES] * rs_scr[g]
            if kind < 2:
                s = _rotate(s, *att_tables(g), span=HEAD_DIM // 2)
            write_att(g, kind * PAIRS + k, s)

    def rest_tile(c):
        col = 3 * ATT_WIDTH + c * COL_TILE
        res = jnp.dot(h_scr[0], w_ref[:, col:col + COL_TILE], preferred_element_type=F32)
        for k in range(COL_TILE // LANES):
            slab = c * (COL_TILE // LANES) + k
            s = res[:, k * LANES:(k + 1) * LANES] * rs_scr[0]
            if slab < REST_VR:
                s = _rotate(s, cr[...], sr[...], span=1)
                if slab >= REST_KR:
                    s = s * (RET_KEY_DIM ** -0.5)
            rest_ref[slab] = s.astype(BF16)

    for c in range(REST_SLABS * LANES // COL_TILE):
        rest_tile(c)
    for g in range(len(DILATIONS)):
        for kind in range(3):
            att_tile(kind * 3 + g)


def _in_projection(x2d, w, gain, tabs, batch, seq):
    t = x2d.shape[0]
    tm = TM_IN
    nt = seq // tm
    n_steps = t // tm
    tab_spec = pl.BlockSpec((tm, LANES), lambda i: (i % nt, 0))
    in_specs = [
        pl.BlockSpec((tm, D_MODEL), lambda i: (i, 0)),
        pl.BlockSpec(memory_space=pl.ANY),
        pl.BlockSpec((D_MODEL, LANES), lambda i: (0, 0), pipeline_mode=pl.Buffered(1)),
    ] + [tab_spec] * 4
    n_att = 3 * PAIRS
    d1, d2 = DILATIONS[1], DILATIONS[2]
    out_shape = [
        jax.ShapeDtypeStruct((n_att, t, LANES), BF16),
        jax.ShapeDtypeStruct((n_att, batch, d1, seq // d1, LANES), BF16),
        jax.ShapeDtypeStruct((n_att, batch, d2, seq // d2, LANES), BF16),
        jax.ShapeDtypeStruct((REST_SLABS, t, LANES), BF16),
    ]
    out_specs = [
        pl.BlockSpec((n_att, tm, LANES), lambda i: (0, i, 0)),
        pl.BlockSpec((n_att, None, d1, tm // d1, LANES), lambda i: (0, i // nt, 0, i % nt, 0)),
        pl.BlockSpec((n_att, None, d2, tm // d2, LANES), lambda i: (0, i // nt, 0, i % nt, 0)),
        pl.BlockSpec((REST_SLABS, tm, LANES), lambda i: (0, i, 0)),
    ]
    return pl.pallas_call(
        _inproj_kernel,
        grid=(n_steps,),
        in_specs=in_specs,
        out_specs=out_specs,
        out_shape=out_shape,
        scratch_shapes=[
            pltpu.VMEM((D_MODEL // LANES, tm, LANES), F32),
            pltpu.VMEM((len(DILATIONS), tm, D_MODEL), BF16),
            pltpu.VMEM((len(DILATIONS) - 1, 2, tm, LANES), F32),
            pltpu.VMEM((len(DILATIONS), tm, LANES), F32),
            pltpu.VMEM((D_MODEL, IN_WIDTH), BF16),
            pltpu.VMEM((2, W_STAGE_ROWS, W_STAGE_COLS), F32),
            pltpu.SemaphoreType.DMA((2,)),
        ],
        compiler_params=pltpu.CompilerParams(
            dimension_semantics=("arbitrary",), vmem_limit_bytes=V7X_VMEM_LIMIT_BYTES),
        name="in_projection",
    )(x2d, w, gain, *tabs)


def _pitch(d):
    return d + 1 if d % 8 == 0 else d


def _attn_scores(q, kk, bias, qk_first_head):
    zero = jnp.zeros_like(q)
    q2 = jnp.concatenate([jnp.where(qk_first_head, q, zero), jnp.where(qk_first_head, zero, q)], axis=0)
    return lax.dot_general(q2, kk, (((1,), (1,)), ((), ())), preferred_element_type=F32) + bias


def _attn_partials(s, vv, v_first_head):
    nq = s.shape[0] // 2
    m = jnp.max(s, axis=1, keepdims=True)
    p = jnp.exp2(s - m)
    l = jnp.sum(p, axis=1, keepdims=True)
    pv = jnp.dot(p.astype(BF16), vv, preferred_element_type=F32)
    acc = jnp.where(v_first_head, pv[:nq], pv[nq:])
    den = jnp.where(v_first_head, l[:nq], l[nq:])
    mm = jnp.where(v_first_head, m[:nq], m[nq:])
    return acc, den, mm


def _attn_kernel(q0, k0, v0, q1, k1, v1, q2, k2, v2, o_ref,
                 acc0, den0, max0, acc1, den1, max1, acc2, den2, max2):
    blk = BAND
    seq = q0.shape[0]
    lane = lax.broadcasted_iota(jnp.int32, (1, LANES), 1)
    v_first = lane < HEAD_DIM
    qk_first = v_first
    qi = lax.broadcasted_iota(jnp.int32, (2 * blk, 2 * blk), 0) & (blk - 1)
    kj = lax.broadcasted_iota(jnp.int32, (2 * blk, 2 * blk), 1)
    band_bias = jnp.where((kj >= qi) & (kj <= qi + BAND), 0.0, MASK_VALUE).astype(F32)
    qi1 = lax.broadcasted_iota(jnp.int32, (2 * blk, blk), 0) & (blk - 1)
    kj1 = lax.broadcasted_iota(jnp.int32, (2 * blk, blk), 1)
    first_bias = jnp.where(kj1 <= qi1, 0.0, MASK_VALUE).astype(F32)

    def key_rows(n):
        if isinstance(n, int) and n == 0:
            return pl.ds(0, blk), first_bias
        st = n * blk if isinstance(n, int) else pl.multiple_of(n * blk, blk)
        return pl.ds(st - blk, 2 * blk), band_bias

    def query_rows(n):
        return pl.ds(n * blk if isinstance(n, int) else pl.multiple_of(n * blk, blk), blk)

    def run_blocks(blocks, emit):
        scores = []
        for i in range(len(blocks) + ATT_SKEW):
            if i < len(blocks):
                qr, kr, _, n = blocks[i]
                rows, bias = key_rows(n)
                scores.append(_attn_scores(qr[query_rows(n), :], kr[rows, :], bias, qk_first))
            if i >= ATT_SKEW:
                j = i - ATT_SKEW
                _, _, vr, n = blocks[j]
                rows, _ = key_rows(n)
                emit(j, _attn_partials(scores[j], vr[rows, :], v_first))
                scores[j] = None

    def dilated_group(qr, kr, vr, outs, d):
        nb = seq // d // blk
        subs = max(1, ATT_BODY_BLOCKS // nb)

        def body(j, carry):
            blocks, where = [], []
            for ri in range(subs):
                r = j * subs + ri
                for n in range(nb):
                    blocks.append((qr.at[r], kr.at[r], vr.at[r], n))
                    where.append(pl.ds(n * (blk * _pitch(d)) + r, blk, stride=_pitch(d)))

            def emit(i, parts):
                for ref, val in zip(outs, parts):
                    ref[where[i], :] = val

            run_blocks(blocks, emit)
            return carry

        lax.fori_loop(0, d // subs, body, 0)

    dilated_group(q1, k1, v1, (acc1, den1, max1), DILATIONS[1])
    dilated_group(q2, k2, v2, (acc2, den2, max2), DILATIONS[2])

    def dense_body(j, carry):
        ns = [j * ATT_BODY_BLOCKS + i for i in range(ATT_BODY_BLOCKS)]

        def emit(i, parts):
            for ref, val in zip((acc0, den0, max0), parts):
                ref[query_rows(ns[i]), :] = val

        run_blocks([(q0, k0, v0, n) for n in ns], emit)
        return carry

    dense_body(0, 0)
    lax.fori_loop(1, seq // blk // ATT_BODY_BLOCKS, dense_body, 0)

    def merge(t, carry):
        rows = pl.ds(pl.multiple_of(t * MERGE_ROWS, MERGE_ROWS), MERGE_ROWS)

        def padded(ref, d):
            if _pitch(d) == d:
                return ref[rows, :]
            per = MERGE_ROWS // d
            return jnp.concatenate(
                [ref[pl.ds((t * per + i) * _pitch(d), d), :] for i in range(per)], axis=0)

        d2 = DILATIONS[2]
        m0, m1, m2 = max0[rows, :], max1[rows, :], padded(max2, d2)
        mx = jnp.maximum(m0, jnp.maximum(m1, m2))
        e0, e1, e2 = jnp.exp2(m0 - mx), jnp.exp2(m1 - mx), jnp.exp2(m2 - mx)
        num = e0 * acc0[rows, :] + e1 * acc1[rows, :] + e2 * padded(acc2, d2)
        den = e0 * den0[rows, :] + e1 * den1[rows, :] + e2 * padded(den2, d2)
        o_ref[rows, :] = (num * (1.0 / den)).astype(o_ref.dtype)
        return carry

    lax.fori_loop(0, seq // MERGE_ROWS, merge, 0, unroll=8)


def _dilated_attention(a0, a1, a2, batch, seq):
    d1, d2 = DILATIONS[1], DILATIONS[2]
    specs0 =[pl.BlockSpec((None, seq, LANES), functools.partial(lambda b, p, k: (k * PAIRS + p, b, 0), k=k))
              for k in range(3)]
    specs1 = [pl.BlockSpec((None, None, d1, seq // d1, LANES),
                           functools.partial(lambda b, p, k: (k * PAIRS + p, b, 0, 0, 0), k=k))
              for k in range(3)]
    specs2 = [pl.BlockSpec((None, None, d2, seq // d2, LANES),
                           functools.partial(lambda b, p, k: (k * PAIRS + p, b, 0, 0, 0), k=k))
              for k in range(3)]
    in_specs = specs0 + specs1 + specs2
    return pl.pallas_call(
        _attn_kernel,
        grid=(batch, PAIRS),
        in_specs=in_specs,
        out_specs=pl.BlockSpec((None, seq, LANES), lambda b, p: (p, b, 0)),
        out_shape=jax.ShapeDtypeStruct((PAIRS, batch * seq, LANES), BF16),
        scratch_shapes=[pltpu.VMEM((seq // d * _pitch(d), LANES), F32) for d in DILATIONS for _ in range(3)],
        compiler_params=pltpu.CompilerParams(
            dimension_semantics=("arbitrary", "arbitrary"), vmem_limit_bytes=V7X_VMEM_LIMIT_BYTES),
        name="dilated_attention",
    )(a0, a0, a0, a1, a1, a1, a2, a2, a2)


def _retention_kernel(q_ref, k_ref, v_ref, dec_ref, zeta_ref, xi_ref, cd_ref, o_ref, kv_scr, state_scr):
    c_len = RET_CHUNK
    n_chunks = q_ref.shape[0] // c_len
    dec = dec_ref[...]
    zeta = zeta_ref[...]
    xi = xi_ref[...]
    xi2 = jnp.concatenate([xi, xi], axis=1)
    cd = cd_ref[...]
    cd2 = jnp.concatenate([cd, cd], axis=1)

    def chunk_rows(c):
        return pl.ds(pl.multiple_of(c * c_len, c_len), c_len)

    def values(rows):
        return jnp.concatenate([v_ref[0, rows, :], v_ref[1, rows, :]], axis=1)

    def chunk_kv(c, carry):
        rows = chunk_rows(c)
        kz = (k_ref[rows, :].astype(F32) * zeta).astype(BF16)
        kv_scr[c] = lax.dot_general(kz, values(rows), (((0,), (0,)), ((), ())),
                                    preferred_element_type=F32)
        return carry

    lax.fori_loop(0, n_chunks, chunk_kv, 0, unroll=RET_UNROLL)

    def chunk_state(c, state):
        state_scr[c] = state.astype(BF16)
        return state * cd2 + kv_scr[c]

    lax.fori_loop(0, n_chunks, chunk_state, jnp.zeros((RET_KEY_DIM, RET_VALUE_DIM), F32))

    def chunk_group_out(j, carry):
        chunks = [j * RET_UNROLL + i for i in range(RET_UNROLL)]
        scores = []
        for c in chunks:
            rows = chunk_rows(c)
            a = lax.dot_general(q_ref[rows, :], k_ref[rows, :], (((1,), (1,)), ((), ())),
                                preferred_element_type=F32) * dec
            scores.append(a.astype(BF16))
        for c, a in zip(chunks, scores):
            rows = chunk_rows(c)
            lhs = jnp.concatenate([a, q_ref[rows, :]], axis=1)
            rhs = jnp.concatenate([values(rows), state_scr[c]], axis=0)
            o = jnp.dot(lhs, rhs, preferred_element_type=F32) * xi2
            o_ref[0, rows, :] = o[:, :LANES].astype(o_ref.dtype)
            o_ref[1, rows, :] = o[:, LANES:].astype(o_ref.dtype)
        return carry

    lax.fori_loop(0, n_chunks // RET_UNROLL, chunk_group_out, 0)


def _retention(rest, tabs, batch, seq):
    dec, zeta, xi, cd = tabs
    vs = RET_VALUE_DIM // LANES
    tab_spec = pl.BlockSpec((None, RET_CHUNK, LANES), lambda b, h: (h, 0, 0))
    in_specs = [
        pl.BlockSpec((None, seq, LANES), lambda b, h: (REST_QR + h, b, 0)),
        pl.BlockSpec((None, seq, LANES), lambda b, h: (REST_KR + h, b, 0)),
        pl.BlockSpec((vs, seq, LANES), lambda b, h: (REST_VR // vs + h, b, 0)),
        tab_spec, tab_spec, tab_spec,
        pl.BlockSpec((None, 1, LANES), lambda b, h: (h, 0, 0)),
    ]
    return pl.pallas_call(
        _retention_kernel,
        grid=(batch, RET_HEADS),
        in_specs=in_specs,
        out_specs=pl.BlockSpec((vs, seq, LANES), lambda b, h: (h, b, 0)),
        out_shape=jax.ShapeDtypeStruct((RET_V_WIDTH // LANES, batch * seq, LANES), BF16),
        scratch_shapes=[pltpu.VMEM((seq // RET_CHUNK, RET_KEY_DIM, RET_VALUE_DIM), F32),
                        pltpu.VMEM((seq // RET_CHUNK, RET_KEY_DIM, RET_VALUE_DIM), BF16)],
        compiler_params=pltpu.CompilerParams(
            dimension_semantics=("arbitrary", "arbitrary"), vmem_limit_bytes=V7X_VMEM_LIMIT_BYTES),
        name="retention",
    )(rest, rest, rest, dec, zeta, xi, cd)


def _slabs(ref, lo=0, hi=None):
    hi = ref.shape[0] if hi is None else hi
    return jnp.concatenate([ref[k] for k in range(lo, hi)], axis=1)


def _tail_kernel(oa_ref, ret_ref, sg_ref, ga_ref, gr_ref, x_ref, wa_hbm, wr_hbm, wo_hbm,
                 wg_hbm, wu_hbm, wd_hbm, gain_ref, gf_ref, out_ref, ret_scr, act_scr,
                 wa_ref, wr_ref, wo_ref, wg_ref, wu_ref, wd_ref, w_stage, w_sems):
    @pl.when(pl.program_id(0) == 0)
    def _():
        jobs = []
        for src, dst, gain in ((wa_hbm, wa_ref, None), (wr_hbm, wr_ref, None), (wo_hbm, wo_ref, None),
                               (wg_hbm, wg_ref, gain_ref), (wu_hbm, wu_ref, gain_ref),
                               (wd_hbm, wd_ref, None)):
            jobs += [(src, dst, t, gain, 1.0) for t in _weight_tiles(dst.shape)]
        _load_weights_bf16(jobs, w_stage, w_sems)

    vs = RET_VALUE_DIM // LANES
    for h in range(RET_HEADS):
        o = _slabs(ret_ref, h * vs, (h + 1) * vs).astype(F32)
        oc = o - jnp.mean(o, axis=-1, keepdims=True)
        var = jnp.mean(oc * oc, axis=-1, keepdims=True)
        gate = _slabs(sg_ref, h * vs, (h + 1) * vs).astype(F32)
        gated = oc * lax.rsqrt(var + NORM_EPS) * (gate * jax.nn.sigmoid(gate))
        ret_scr[:, h * RET_VALUE_DIM:(h + 1) * RET_VALUE_DIM] = gated.astype(BF16)
    ya = jnp.dot(_slabs(oa_ref), wa_ref[...], preferred_element_type=F32)
    yr = jnp.dot(ret_scr[...], wr_ref[...], preferred_element_type=F32)
    merged = (jax.nn.sigmoid(_slabs(ga_ref).astype(F32)) * ya
              + jax.nn.sigmoid(_slabs(gr_ref).astype(F32)) * yr)
    x1 = x_ref[...] + jnp.dot(merged.astype(BF16), wo_ref[...], preferred_element_type=F32)
    scale = jnp.broadcast_to(lax.rsqrt(jnp.mean(x1 * x1, axis=-1, keepdims=True) + NORM_EPS),
                             (x1.shape[0], FFN_CHUNK))
    h2 = x1.astype(BF16)
    for c in range(0, FFN_HIDDEN, FFN_CHUNK):
        gate = jnp.dot(h2, wg_ref[:, c:c + FFN_CHUNK], preferred_element_type=F32) * scale
        up = jnp.dot(h2, wu_ref[:, c:c + FFN_CHUNK], preferred_element_type=F32) * scale
        act_scr[:, c:c + FFN_CHUNK] = (gate * jax.nn.sigmoid(gate) * up).astype(BF16)
    down = jnp.dot(act_scr[...], wd_ref[...], preferred_element_type=F32)
    out_ref[...] = _rms(x1 + down, gf_ref[...])


def _tail(oa, ret, rest, x2d, wa, wr, wo, wg, wu, wd, gain, gf):
    t = x2d.shape[0]
    tm = TM_OUT
    gs = D_MODEL // LANES
    full = lambda shape: pl.BlockSpec(shape, lambda i: (0,) * len(shape), pipeline_mode=pl.Buffered(1))
    vec = pl.BlockSpec((1, D_MODEL), lambda i: (0, 0))
    row = pl.BlockSpec((tm, D_MODEL), lambda i: (i, 0))
    in_specs = [
        pl.BlockSpec((PAIRS, tm, LANES), lambda i: (0, i, 0)),
        pl.BlockSpec((gs, tm, LANES), lambda i: (0, i, 0)),
        pl.BlockSpec((gs, tm, LANES), lambda i: (REST_GR // gs, i, 0)),
        pl.BlockSpec((gs, tm, LANES), lambda i: (REST_GA // gs, i, 0)),
        pl.BlockSpec((gs, tm, LANES), lambda i: (REST_GRT // gs, i, 0)),
        row,
    ] + [pl.BlockSpec(memory_space=pl.ANY)] * 6 + [full((D_MODEL, LANES)), vec]
    weight_shapes = [(GROUP_WIDTH, D_MODEL), (RET_V_WIDTH, D_MODEL), (D_MODEL, D_MODEL),
                     (D_MODEL, FFN_HIDDEN), (D_MODEL, FFN_HIDDEN), (FFN_HIDDEN, D_MODEL)]
    return pl.pallas_call(
        _tail_kernel,
        grid=(t // tm,),
        in_specs=in_specs,
        out_specs=row,
        out_shape=jax.ShapeDtypeStruct((t, D_MODEL), F32),
        scratch_shapes=[pltpu.VMEM((tm, RET_V_WIDTH), BF16), pltpu.VMEM((tm, FFN_HIDDEN), BF16)]
        + [pltpu.VMEM(shape, BF16) for shape in weight_shapes]
        + [pltpu.VMEM((2, W_STAGE_ROWS, W_STAGE_COLS), F32), pltpu.SemaphoreType.DMA((2,))],
        compiler_params=pltpu.CompilerParams(
            dimension_semantics=("arbitrary",), vmem_limit_bytes=V7X_VMEM_LIMIT_BYTES),
        name="merge_ffn",
    )(oa, ret, rest, rest, rest, x2d, wa, wr, wo, wg, wu, wd, gain, gf)


def _row_gain(g):
    return jnp.broadcast_to(g[:, None], (g.shape[0], LANES))


def _position_tables(seq):
    pos = np.arange(seq, dtype=np.float64)
    inv = ROPE_THETA ** (-np.arange(0, HEAD_DIM, 2, dtype=np.float64) / HEAD_DIM)
    ang = pos[:, None] * inv[None, :]
    c, s = np.cos(ang), np.sin(ang)
    cos_a = np.concatenate([c, c, c, c], axis=1)
    sin_a = np.concatenate([-s, s, -s, s], axis=1)
    base = 1.0 / (ROPE_THETA ** np.linspace(0.0, 1.0, RET_KEY_DIM // 2, dtype=np.float64))
    ang_r = pos[:, None] * base[None, :]
    cr, sr = np.cos(ang_r), np.sin(ang_r)
    cos_r = np.repeat(cr, 2, axis=1)
    sin_r = np.repeat(sr, 2, axis=1) * np.tile([-1.0, 1.0], RET_KEY_DIM // 2)[None, :]
    return [jnp.asarray(t.astype(np.float32)) for t in (cos_a, sin_a, cos_r, sin_r)]


def _retention_tables():
    c = RET_CHUNK
    log_g = np.log1p(-(2.0 ** (-5.0 - np.arange(RET_HEADS, dtype=np.float64))))
    idx = np.arange(c, dtype=np.float64)
    diff = idx[:, None] - idx[None, :]
    decay = np.where(diff[None] >= 0, np.exp(-(idx + 1.0)[None, None, :] * log_g[:, None, None]), 0.0)
    zeta = np.exp((c - 1 - idx)[None, :] * log_g[:, None])
    xi = np.exp((idx + 1.0)[None, :] * log_g[:, None])
    chunk_decay = np.exp(c * log_g)
    bcast = lambda v: np.broadcast_to(v[:, :, None], (RET_HEADS, c, LANES))
    cd = np.broadcast_to(chunk_decay[:, None, None], (RET_HEADS, 1, LANES))
    return tuple(jnp.asarray(np.ascontiguousarray(t).astype(np.float32))
                 for t in (decay, bcast(zeta), bcast(xi), cd))


def kernel(x, norm_mix_g, w_in, w_out_attn, w_out_ret, w_out, norm_ffn_g, w_ffn_gate, w_ffn_up,
           w_ffn_down, norm_final_g):
    batch, seq, _ = x.shape
    assert w_in.shape[0] == 1, "single-layer problem"
    assert seq % (TM_IN * 1) == 0 and TM_IN % (DILATIONS[-1] * 16) == 0
    x2d = x.reshape(batch * seq, D_MODEL)
    a0, a1, a2, rest = _in_projection(
        x2d, w_in[0], _row_gain(norm_mix_g[0]), _position_tables(seq), batch, seq)
    oa = _dilated_attention(a0, a1, a2, batch, seq)
    ret = _retention(rest, _retention_tables(), batch, seq)
    out = _tail(oa, ret, rest, x2d, w_out_attn[0], w_out_ret[0], w_out[0], w_ffn_gate[0],
                w_ffn_up[0], w_ffn_down[0], _row_gain(norm_ffn_g[0]), norm_final_g[None, :])
    return out.reshape(batch, seq, D_MODEL)
```

```python
import functools

import jax
import jax.numpy as jnp
import numpy as np
from jax import lax
from jax.experimental import pallas as pl
from jax.experimental.pallas import tpu as pltpu

F32 = jnp.float32
BF16 = jnp.bfloat16

LANES = 128
V7X_VMEM_LIMIT_BYTES = 60 * 1024 * 1024

D_MODEL = 1024
HEAD_DIM = 64
HEADS_PER_GROUP = 8
DILATIONS = (1, 4, 16)
BAND = 128
GROUP_WIDTH = HEADS_PER_GROUP * HEAD_DIM
ATT_WIDTH = len(DILATIONS) * GROUP_WIDTH
PAIRS = GROUP_WIDTH // LANES
ROPE_THETA = 10000.0

RET_HEADS = 4
RET_KEY_DIM = 128
RET_VALUE_DIM = 256
RET_QK_WIDTH = RET_HEADS * RET_KEY_DIM
RET_V_WIDTH = RET_HEADS * RET_VALUE_DIM
RET_CHUNK = 128

FFN_HIDDEN = 2816
NORM_EPS = 1e-6
MASK_VALUE = -1e30

IN_WIDTH = 3 * ATT_WIDTH + 2 * RET_QK_WIDTH + 2 * RET_V_WIDTH + 2 * D_MODEL
COL_TILE = 512
ATT_TILES = 3 * ATT_WIDTH // COL_TILE
REST_SLABS = (IN_WIDTH - 3 * ATT_WIDTH) // LANES
REST_QR, REST_KR, REST_VR, REST_GR, REST_GA, REST_GRT = 0, 4, 8, 16, 24, 32

TM_IN = 512
TM_OUT = 512
FFN_CHUNK = 256
ATT_BODY_BLOCKS = 32
MERGE_ROWS = 64
ATT_SKEW = 2
LOG2_E = 1.4426950408889634
RET_UNROLL = 16


def _rms(x, g):
    ms = jnp.mean(x * x, axis=-1, keepdims=True)
    return x * lax.rsqrt(ms + NORM_EPS) * g


def _rotate(t, cos, sin_signed, span):
    lane = lax.broadcasted_iota(jnp.int32, (1, LANES), 1)
    low = (lane % (2 * span)) < span
    partner = jnp.where(low, pltpu.roll(t, LANES - span, 1), pltpu.roll(t, span, 1))
    return t * cos + partner * sin_signed


W_STAGE_ROWS, W_STAGE_COLS = 1024, 256
W_STAGE_SLOTS = 4


def _weight_tiles(shape):
    rows, cols = shape
    return [(r, min(W_STAGE_ROWS, rows - r), c, min(W_STAGE_COLS, cols - c))
            for r in range(0, rows, W_STAGE_ROWS) for c in range(0, cols, W_STAGE_COLS)]


def _load_weights_bf16(jobs, stage, sems):
    slots = stage.shape[0]

    def copy(k):
        src, _, (r0, nr, c0, nc), _, _ = jobs[k]
        return pltpu.make_async_copy(src.at[pl.ds(r0, nr), pl.ds(c0, nc)],
                                     stage.at[k % slots, pl.ds(0, nr), pl.ds(0, nc)],
                                     sems.at[k % slots])

    for k in range(min(slots - 1, len(jobs))):
        copy(k).start()
    for k, (_, dst, (r0, nr, c0, nc), row_gain, scale) in enumerate(jobs):
        if k + slots - 1 < len(jobs):
            copy(k + slots - 1).start()
        copy(k).wait()
        for c in range(0, nc, LANES):
            tile = stage[k % slots, 0:nr, c:c + LANES]
            if row_gain is not None:
                tile = tile * row_gain[r0:r0 + nr, :]
            if scale != 1.0:
                tile = tile * scale
            dst[r0:r0 + nr, c0 + c:c0 + c + LANES] = tile.astype(BF16)


def _inproj_kernel(x_ref, w_hbm, gain_ref, ca, sa, cr, sr,
                   a0_ref, a1_ref, a2_ref, rest_ref, y_scr, h_scr, tab_scr, rs_scr,
                   w_ref, w_stage, w_sems):
    tm = x_ref.shape[0]
    n_slabs = D_MODEL // LANES

    @pl.when(pl.program_id(0) == 0)
    def _():
        q_scale = HEAD_DIM ** -0.5 * LOG2_E
        _load_weights_bf16(
            [(w_hbm, w_ref, t, gain_ref, q_scale if t[2] < ATT_WIDTH else 1.0)
             for t in _weight_tiles((D_MODEL, IN_WIDTH))], w_stage, w_sems)

    x = x_ref[...]
    h_scr[0] = x.astype(BF16)
    rs_scr[0] = jnp.broadcast_to(lax.rsqrt(jnp.mean(x * x, axis=-1, keepdims=True) + NORM_EPS),
                                 (tm, LANES))
    for k in range(n_slabs):
        y_scr[k] = x[:, k * LANES:(k + 1) * LANES]
    for gi, d in enumerate(DILATIONS):
        if d == 1:
            continue
        n = tm // d
        for r in range(d):
            rs_scr[gi, r * n:(r + 1) * n, :] = rs_scr[0, pl.ds(r, n, stride=d), :]
            for k in range(n_slabs):
                h_scr[gi, r * n:(r + 1) * n, k * LANES:(k + 1) * LANES] = (
                    y_scr[k, pl.ds(r, n, stride=d), :].astype(BF16))

    for gi, d in enumerate(DILATIONS):
        if d == 1:
            continue
        n = tm // d
        for r in range(d):
            tab_scr[gi - 1, 0, r * n:(r + 1) * n, :] = ca[pl.ds(r, n, stride=d), :]
            tab_scr[gi - 1, 1, r * n:(r + 1) * n, :] = sa[pl.ds(r, n, stride=d), :]

    def att_tables(g):
        if g == 0:
            return ca[...], sa[...]
        return tab_scr[g - 1, 0], tab_scr[g - 1, 1]

    def write_att(g, slab, val):
        v = val.astype(BF16)
        d = DILATIONS[g]
        n = tm // d
        if g == 0:
            a0_ref[slab] = v
        else:
            out = a1_ref if g == 1 else a2_ref
            for r in range(d):
                out[slab, r] = v[r * n:(r + 1) * n, :]

    def att_tile(c):
        kind, g = divmod(c, 3)
        res = jnp.dot(h_scr[g], w_ref[:, c * COL_TILE:(c + 1) * COL_TILE],
                      preferred_element_type=F32)
        for k in range(PAIRS):
            s = res[:, k * LANES:(k + 1) * LANES] * rs_scr[g]
            if kind < 2:
                s = _rotate(s, *att_tables(g), span=HEAD_DIM // 2)
            write_att(g, kind * PAIRS + k, s)

    def rest_tile(c):
        col = 3 * ATT_WIDTH + c * COL_TILE
        res = jnp.dot(h_scr[0], w_ref[:, col:col + COL_TILE], preferred_element_type=F32)
        for k in range(COL_TILE // LANES):
            slab = c * (COL_TILE // LANES) + k
            s = res[:, k * LANES:(k + 1) * LANES] * rs_scr[0]
            if slab < REST_VR:
                s = _rotate(s, cr[...], sr[...], span=1)
                if slab >= REST_KR:
                    s = s * (RET_KEY_DIM ** -0.5)
            rest_ref[slab] = s.astype(BF16)

    for c in range(REST_SLABS * LANES // COL_TILE):
        rest_tile(c)
    for g in range(len(DILATIONS)):
        for kind in range(3):
            att_tile(kind * 3 + g)


def _in_projection(x2d, w, gain, tabs, batch, seq):
    t = x2d.shape[0]
    tm = TM_IN
    nt = seq // tm
    n_steps = t // tm
    tab_spec = pl.BlockSpec((tm, LANES), lambda i: (i % nt, 0))
    in_specs = [
        pl.BlockSpec((tm, D_MODEL), lambda i: (i, 0)),
        pl.BlockSpec(memory_space=pl.ANY),
        pl.BlockSpec((D_MODEL, LANES), lambda i: (0, 0), pipeline_mode=pl.Buffered(1)),
    ] + [tab_spec] * 4
    n_att = 3 * PAIRS
    d1, d2 = DILATIONS[1], DILATIONS[2]
    out_shape = [
        jax.ShapeDtypeStruct((n_att, t, LANES), BF16),
        jax.ShapeDtypeStruct((n_att, batch, d1, seq // d1, LANES), BF16),
        jax.ShapeDtypeStruct((n_att, batch, d2, seq // d2, LANES), BF16),
        jax.ShapeDtypeStruct((REST_SLABS, t, LANES), BF16),
    ]
    out_specs = [
        pl.BlockSpec((n_att, tm, LANES), lambda i: (0, i, 0)),
        pl.BlockSpec((n_att, None, d1, tm // d1, LANES), lambda i: (0, i // nt, 0, i % nt, 0)),
        pl.BlockSpec((n_att, None, d2, tm // d2, LANES), lambda i: (0, i // nt, 0, i % nt, 0)),
        pl.BlockSpec((REST_SLABS, tm, LANES), lambda i: (0, i, 0)),
    ]
    return pl.pallas_call(
        _inproj_kernel,
        grid=(n_steps,),
        in_specs=in_specs,
        out_specs=out_specs,
        out_shape=out_shape,
        scratch_shapes=[
            pltpu.VMEM((D_MODEL // LANES, tm, LANES), F32),
            pltpu.VMEM((len(DILATIONS), tm, D_MODEL), BF16),
            pltpu.VMEM((len(DILATIONS) - 1, 2, tm, LANES), F32),
            pltpu.VMEM((len(DILATIONS), tm, LANES), F32),
            pltpu.VMEM((D_MODEL, IN_WIDTH), BF16),
            pltpu.VMEM((W_STAGE_SLOTS, W_STAGE_ROWS, W_STAGE_COLS), F32),
            pltpu.SemaphoreType.DMA((W_STAGE_SLOTS,)),
        ],
        compiler_params=pltpu.CompilerParams(
            dimension_semantics=("arbitrary",), vmem_limit_bytes=V7X_VMEM_LIMIT_BYTES),
        name="in_projection",
    )(x2d, w, gain, *tabs)


def _pitch(d):
    return d + 1 if d % 8 == 0 else d


def _attn_scores(q, kk, bias, qk_first_head):
    zero = jnp.zeros_like(q)
    q2 = jnp.concatenate([jnp.where(qk_first_head, q, zero), jnp.where(qk_first_head, zero, q)], axis=0)
    return lax.dot_general(q2, kk, (((1,), (1,)), ((), ())), preferred_element_type=F32) + bias


def _attn_partials(s, vv, v_first_head):
    nq = s.shape[0] // 2
    m = jnp.max(s, axis=1, keepdims=True)
    p = jnp.exp2(s - m)
    l = jnp.sum(p, axis=1, keepdims=True)
    pv = jnp.dot(p.astype(BF16), vv, preferred_element_type=F32)
    acc = jnp.where(v_first_head, pv[:nq], pv[nq:])
    den = jnp.where(v_first_head, l[:nq], l[nq:])
    mm = jnp.where(v_first_head, m[:nq], m[nq:])
    return acc, den, mm


def _attn_kernel(q0, k0, v0, q1, k1, v1, q2, k2, v2, o_ref,
                 acc0, den0, max0, acc1, den1, max1, acc2, den2, max2):
    blk = BAND
    seq = q0.shape[0]
    lane = lax.broadcasted_iota(jnp.int32, (1, LANES), 1)
    v_first = lane < HEAD_DIM
    qk_first = v_first
    qi = lax.broadcasted_iota(jnp.int32, (2 * blk, 2 * blk), 0) & (blk - 1)
    kj = lax.broadcasted_iota(jnp.int32, (2 * blk, 2 * blk), 1)
    band_bias = jnp.where((kj >= qi) & (kj <= qi + BAND), 0.0, MASK_VALUE).astype(F32)
    qi1 = lax.broadcasted_iota(jnp.int32, (2 * blk, blk), 0) & (blk - 1)
    kj1 = lax.broadcasted_iota(jnp.int32, (2 * blk, blk), 1)
    first_bias = jnp.where(kj1 <= qi1, 0.0, MASK_VALUE).astype(F32)

    def key_rows(n):
        if isinstance(n, int) and n == 0:
            return pl.ds(0, blk), first_bias
        st = n * blk if isinstance(n, int) else pl.multiple_of(n * blk, blk)
        return pl.ds(st - blk, 2 * blk), band_bias

    def query_rows(n):
        return pl.ds(n * blk if isinstance(n, int) else pl.multiple_of(n * blk, blk), blk)

    def run_blocks(blocks, emit):
        scores = []
        for i in range(len(blocks) + ATT_SKEW):
            if i < len(blocks):
                qr, kr, _, n = blocks[i]
                rows, bias = key_rows(n)
                scores.append(_attn_scores(qr[query_rows(n), :], kr[rows, :], bias, qk_first))
            if i >= ATT_SKEW:
                j = i - ATT_SKEW
                _, _, vr, n = blocks[j]
                rows, _ = key_rows(n)
                emit(j, _attn_partials(scores[j], vr[rows, :], v_first))
                scores[j] = None

    def dilated_group(qr, kr, vr, outs, d):
        nb = seq // d // blk
        subs = max(1, ATT_BODY_BLOCKS // nb)

        def body(j, carry):
            blocks, where = [], []
            for ri in range(subs):
                r = j * subs + ri
                for n in range(nb):
                    blocks.append((qr.at[r], kr.at[r], vr.at[r], n))
                    where.append(pl.ds(n * (blk * _pitch(d)) + r, blk, stride=_pitch(d)))

            def emit(i, parts):
                for ref, val in zip(outs, parts):
                    ref[where[i], :] = val

            run_blocks(blocks, emit)
            return carry

        lax.fori_loop(0, d // subs, body, 0)

    dilated_group(q1, k1, v1, (acc1, den1, max1), DILATIONS[1])
    dilated_group(q2, k2, v2, (acc2, den2, max2), DILATIONS[2])

    def dense_body(j, carry):
        ns = [j * ATT_BODY_BLOCKS + i for i in range(ATT_BODY_BLOCKS)]

        def emit(i, parts):
            for ref, val in zip((acc0, den0, max0), parts):
                ref[query_rows(ns[i]), :] = val

        run_blocks([(q0, k0, v0, n) for n in ns], emit)
        return carry

    dense_body(0, 0)
    lax.fori_loop(1, seq // blk // ATT_BODY_BLOCKS, dense_body, 0)

    def merge(t, carry):
        rows = pl.ds(pl.multiple_of(t * MERGE_ROWS, MERGE_ROWS), MERGE_ROWS)

        def padded(ref, d):
            if _pitch(d) == d:
                return ref[rows, :]
            per = MERGE_ROWS // d
            return jnp.concatenate(
                [ref[pl.ds((t * per + i) * _pitch(d), d), :] for i in range(per)], axis=0)

        d2 = DILATIONS[2]
        m0, m1, m2 = max0[rows, :], max1[rows, :], padded(max2, d2)
        mx = jnp.maximum(m0, jnp.maximum(m1, m2))
        e0, e1, e2 = jnp.exp2(m0 - mx), jnp.exp2(m1 - mx), jnp.exp2(m2 - mx)
        num = e0 * acc0[rows, :] + e1 * acc1[rows, :] + e2 * padded(acc2, d2)
        den = e0 * den0[rows, :] + e1 * den1[rows, :] + e2 * padded(den2, d2)
        o_ref[rows, :] = (num * (1.0 / den)).astype(o_ref.dtype)
        return carry

    lax.fori_loop(0, seq // MERGE_ROWS, merge, 0, unroll=8)


def _dilated_attention(a0, a1, a2, batch, seq):
    d1, d2 = DILATIONS[1], DILATIONS[2]
    specs0 =[pl.BlockSpec((None, seq, LANES), functools.partial(lambda b, p, k: (k * PAIRS + p, b, 0), k=k))
              for k in range(3)]
    specs1 = [pl.BlockSpec((None, None, d1, seq // d1, LANES),
                           functools.partial(lambda b, p, k: (k * PAIRS + p, b, 0, 0, 0), k=k))
              for k in range(3)]
    specs2 = [pl.BlockSpec((None, None, d2, seq // d2, LANES),
                           functools.partial(lambda b, p, k: (k * PAIRS + p, b, 0, 0, 0), k=k))
              for k in range(3)]
    in_specs = specs0 + specs1 + specs2
    return pl.pallas_call(
        _attn_kernel,
        grid=(batch, PAIRS),
        in_specs=in_specs,
        out_specs=pl.BlockSpec((None, seq, LANES), lambda b, p: (p, b, 0)),
        out_shape=jax.ShapeDtypeStruct((PAIRS, batch * seq, LANES), BF16),
        scratch_shapes=[pltpu.VMEM((seq // d * _pitch(d), LANES), F32) for d in DILATIONS for _ in range(3)],
        compiler_params=pltpu.CompilerParams(
            dimension_semantics=("arbitrary", "arbitrary"), vmem_limit_bytes=V7X_VMEM_LIMIT_BYTES),
        name="dilated_attention",
    )(a0, a0, a0, a1, a1, a1, a2, a2, a2)


def _retention_kernel(q_ref, k_ref, v_ref, dec_ref, zeta_ref, xi_ref, cd_ref, o_ref, kv_scr, state_scr):
    c_len = RET_CHUNK
    n_chunks = q_ref.shape[0] // c_len
    dec = dec_ref[...]
    zeta = zeta_ref[...]
    xi = xi_ref[...]
    xi2 = jnp.concatenate([xi, xi], axis=1)
    cd = cd_ref[...]
    cd2 = jnp.concatenate([cd, cd], axis=1)

    def chunk_rows(c):
        return pl.ds(pl.multiple_of(c * c_len, c_len), c_len)

    def values(rows):
        return jnp.concatenate([v_ref[0, rows, :], v_ref[1, rows, :]], axis=1)

    def chunk_kv(c, carry):
        rows = chunk_rows(c)
        kz = (k_ref[rows, :].astype(F32) * zeta).astype(BF16)
        kv_scr[c] = lax.dot_general(kz, values(rows), (((0,), (0,)), ((), ())),
                                    preferred_element_type=F32)
        return carry

    lax.fori_loop(0, n_chunks, chunk_kv, 0, unroll=RET_UNROLL)

    def chunk_state(c, state):
        state_scr[c] = state.astype(BF16)
        return state * cd2 + kv_scr[c]

    lax.fori_loop(0, n_chunks, chunk_state, jnp.zeros((RET_KEY_DIM, RET_VALUE_DIM), F32))

    def chunk_group_out(j, carry):
        chunks = [j * RET_UNROLL + i for i in range(RET_UNROLL)]
        scores = []
        for c in chunks:
            rows = chunk_rows(c)
            a = lax.dot_general(q_ref[rows, :], k_ref[rows, :], (((1,), (1,)), ((), ())),
                                preferred_element_type=F32) * dec
            scores.append(a.astype(BF16))
        for c, a in zip(chunks, scores):
            rows = chunk_rows(c)
            lhs = jnp.concatenate([a, q_ref[rows, :]], axis=1)
            rhs = jnp.concatenate([values(rows), state_scr[c]], axis=0)
            o = jnp.dot(lhs, rhs, preferred_element_type=F32) * xi2
            o_ref[0, rows, :] = o[:, :LANES].astype(o_ref.dtype)
            o_ref[1, rows, :] = o[:, LANES:].astype(o_ref.dtype)
        return carry

    lax.fori_loop(0, n_chunks // RET_UNROLL, chunk_group_out, 0)


def _retention(rest, tabs, batch, seq):
    dec, zeta, xi, cd = tabs
    vs = RET_VALUE_DIM // LANES
    tab_spec = pl.BlockSpec((None, RET_CHUNK, LANES), lambda b, h: (h, 0, 0))
    in_specs = [
        pl.BlockSpec((None, seq, LANES), lambda b, h: (REST_QR + h, b, 0)),
        pl.BlockSpec((None, seq, LANES), lambda b, h: (REST_KR + h, b, 0)),
        pl.BlockSpec((vs, seq, LANES), lambda b, h: (REST_VR // vs + h, b, 0)),
        tab_spec, tab_spec, tab_spec,
        pl.BlockSpec((None, 1, LANES), lambda b, h: (h, 0, 0)),
    ]
    return pl.pallas_call(
        _retention_kernel,
        grid=(batch, RET_HEADS),
        in_specs=in_specs,
        out_specs=pl.BlockSpec((vs, seq, LANES), lambda b, h: (h, b, 0)),
        out_shape=jax.ShapeDtypeStruct((RET_V_WIDTH // LANES, batch * seq, LANES), BF16),
        scratch_shapes=[pltpu.VMEM((seq // RET_CHUNK, RET_KEY_DIM, RET_VALUE_DIM), F32),
                        pltpu.VMEM((seq // RET_CHUNK, RET_KEY_DIM, RET_VALUE_DIM), BF16)],
        compiler_params=pltpu.CompilerParams(
            dimension_semantics=("arbitrary", "arbitrary"), vmem_limit_bytes=V7X_VMEM_LIMIT_BYTES),
        name="retention",
    )(rest, rest, rest, dec, zeta, xi, cd)


def _slabs(ref, lo=0, hi=None):
    hi = ref.shape[0] if hi is None else hi
    return jnp.concatenate([ref[k] for k in range(lo, hi)], axis=1)


def _tail_kernel(oa_ref, ret_ref, sg_ref, ga_ref, gr_ref, x_ref, wa_hbm, wr_hbm, wo_hbm,
                 wg_hbm, wu_hbm, wd_hbm, gain_ref, gf_ref, out_ref, ret_scr, act_scr,
                 wa_ref, wr_ref, wo_ref, wg_ref, wu_ref, wd_ref, w_stage, w_sems):
    @pl.when(pl.program_id(0) == 0)
    def _():
        jobs = []
        for src, dst, gain in ((wa_hbm, wa_ref, None), (wr_hbm, wr_ref, None), (wo_hbm, wo_ref, None),
                               (wg_hbm, wg_ref, gain_ref), (wu_hbm, wu_ref, gain_ref),
                               (wd_hbm, wd_ref, None)):
            jobs += [(src, dst, t, gain, 1.0) for t in _weight_tiles(dst.shape)]
        _load_weights_bf16(jobs, w_stage, w_sems)

    vs = RET_VALUE_DIM // LANES
    for h in range(RET_HEADS):
        o = _slabs(ret_ref, h * vs, (h + 1) * vs).astype(F32)
        oc = o - jnp.mean(o, axis=-1, keepdims=True)
        var = jnp.mean(oc * oc, axis=-1, keepdims=True)
        gate = _slabs(sg_ref, h * vs, (h + 1) * vs).astype(F32)
        gated = oc * lax.rsqrt(var + NORM_EPS) * (gate * jax.nn.sigmoid(gate))
        ret_scr[:, h * RET_VALUE_DIM:(h + 1) * RET_VALUE_DIM] = gated.astype(BF16)
    ya = jnp.dot(_slabs(oa_ref), wa_ref[...], preferred_element_type=F32)
    yr = jnp.dot(ret_scr[...], wr_ref[...], preferred_element_type=F32)
    merged = (jax.nn.sigmoid(_slabs(ga_ref).astype(F32)) * ya
              + jax.nn.sigmoid(_slabs(gr_ref).astype(F32)) * yr)
    x1 = x_ref[...] + jnp.dot(merged.astype(BF16), wo_ref[...], preferred_element_type=F32)
    scale = jnp.broadcast_to(lax.rsqrt(jnp.mean(x1 * x1, axis=-1, keepdims=True) + NORM_EPS),
                             (x1.shape[0], FFN_CHUNK))
    h2 = x1.astype(BF16)
    for c in range(0, FFN_HIDDEN, FFN_CHUNK):
        gate = jnp.dot(h2, wg_ref[:, c:c + FFN_CHUNK], preferred_element_type=F32) * scale
        up = jnp.dot(h2, wu_ref[:, c:c + FFN_CHUNK], preferred_element_type=F32) * scale
        act_scr[:, c:c + FFN_CHUNK] = (gate * jax.nn.sigmoid(gate) * up).astype(BF16)
    down = jnp.dot(act_scr[...], wd_ref[...], preferred_element_type=F32)
    out_ref[...] = _rms(x1 + down, gf_ref[...])


def _tail(oa, ret, rest, x2d, wa, wr, wo, wg, wu, wd, gain, gf):
    t = x2d.shape[0]
    tm = TM_OUT
    gs = D_MODEL // LANES
    full = lambda shape: pl.BlockSpec(shape, lambda i: (0,) * len(shape), pipeline_mode=pl.Buffered(1))
    vec = pl.BlockSpec((1, D_MODEL), lambda i: (0, 0))
    row = pl.BlockSpec((tm, D_MODEL), lambda i: (i, 0))
    in_specs = [
        pl.BlockSpec((PAIRS, tm, LANES), lambda i: (0, i, 0)),
        pl.BlockSpec((gs, tm, LANES), lambda i: (0, i, 0)),
        pl.BlockSpec((gs, tm, LANES), lambda i: (REST_GR // gs, i, 0)),
        pl.BlockSpec((gs, tm, LANES), lambda i: (REST_GA // gs, i, 0)),
        pl.BlockSpec((gs, tm, LANES), lambda i: (REST_GRT // gs, i, 0)),
        row,
    ] + [pl.BlockSpec(memory_space=pl.ANY)] * 6 + [full((D_MODEL, LANES)), vec]
    weight_shapes = [(GROUP_WIDTH, D_MODEL), (RET_V_WIDTH, D_MODEL), (D_MODEL, D_MODEL),
                     (D_MODEL, FFN_HIDDEN), (D_MODEL, FFN_HIDDEN), (FFN_HIDDEN, D_MODEL)]
    return pl.pallas_call(
        _tail_kernel,
        grid=(t // tm,),
        in_specs=in_specs,
        out_specs=row,
        out_shape=jax.ShapeDtypeStruct((t, D_MODEL), F32),
        scratch_shapes=[pltpu.VMEM((tm, RET_V_WIDTH), BF16), pltpu.VMEM((tm, FFN_HIDDEN), BF16)]
        + [pltpu.VMEM(shape, BF16) for shape in weight_shapes]
        + [pltpu.VMEM((W_STAGE_SLOTS, W_STAGE_ROWS, W_STAGE_COLS), F32), pltpu.SemaphoreType.DMA((W_STAGE_SLOTS,))],
        compiler_params=pltpu.CompilerParams(
            dimension_semantics=("arbitrary",), vmem_limit_bytes=V7X_VMEM_LIMIT_BYTES),
        name="merge_ffn",
    )(oa, ret, rest, rest, rest, x2d, wa, wr, wo, wg, wu, wd, gain, gf)


def _row_gain(g):
    return jnp.broadcast_to(g[:, None], (g.shape[0], LANES))


def _position_tables(seq):
    pos = np.arange(seq, dtype=np.float64)
    inv = ROPE_THETA ** (-np.arange(0, HEAD_DIM, 2, dtype=np.float64) / HEAD_DIM)
    ang = pos[:, None] * inv[None, :]
    c, s = np.cos(ang), np.sin(ang)
    cos_a = np.concatenate([c, c, c, c], axis=1)
    sin_a = np.concatenate([-s, s, -s, s], axis=1)
    base = 1.0 / (ROPE_THETA ** np.linspace(0.0, 1.0, RET_KEY_DIM // 2, dtype=np.float64))
    ang_r = pos[:, None] * base[None, :]
    cr, sr = np.cos(ang_r), np.sin(ang_r)
    cos_r = np.repeat(cr, 2, axis=1)
    sin_r = np.repeat(sr, 2, axis=1) * np.tile([-1.0, 1.0], RET_KEY_DIM // 2)[None, :]
    return [jnp.asarray(t.astype(np.float32)) for t in (cos_a, sin_a, cos_r, sin_r)]


def _retention_tables():
    c = RET_CHUNK
    log_g = np.log1p(-(2.0 ** (-5.0 - np.arange(RET_HEADS, dtype=np.float64))))
    idx = np.arange(c, dtype=np.float64)
    diff = idx[:, None] - idx[None, :]
    decay = np.where(diff[None] >= 0, np.exp(-(idx + 1.0)[None, None, :] * log_g[:, None, None]), 0.0)
    zeta = np.exp((c - 1 - idx)[None, :] * log_g[:, None])
    xi = np.exp((idx + 1.0)[None, :] * log_g[:, None])
    chunk_decay = np.exp(c * log_g)
    bcast = lambda v: np.broadcast_to(v[:, :, None], (RET_HEADS, c, LANES))
    cd = np.broadcast_to(chunk_decay[:, None, None], (RET_HEADS, 1, LANES))
    return tuple(jnp.asarray(np.ascontiguousarray(t).astype(np.float32))
                 for t in (decay, bcast(zeta), bcast(xi), cd))


def kernel(x, norm_mix_g, w_in, w_out_attn, w_out_ret, w_out, norm_ffn_g, w_ffn_gate, w_ffn_up,
           w_ffn_down, norm_final_g):
    batch, seq, _ = x.shape
    assert w_in.shape[0] == 1, "single-layer problem"
    assert seq % (TM_IN * 1) == 0 and TM_IN % (DILATIONS[-1] * 16) == 0
    x2d = x.reshape(batch * seq, D_MODEL)
    a0, a1, a2, rest = _in_projection(
        x2d, w_in[0], _row_gain(norm_mix_g[0]), _position_tables(seq), batch, seq)
    oa = _dilated_attention(a0, a1, a2, batch, seq)
    ret = _retention(rest, _retention_tables(), batch, seq)
    out = _tail(oa, ret, rest, x2d, w_out_attn[0], w_out_ret[0], w_out[0], w_ffn_gate[0],
                w_ffn_up[0], w_ffn_down[0], _row_gain(norm_ffn_g[0]), norm_final_g[None, :])
    return out.reshape(batch, seq, D_MODEL)
```

```python
import functools

import jax
import jax.numpy as jnp
import numpy as np
from jax import lax
from jax.experimental import pallas as pl
from jax.experimental.pallas import tpu as pltpu

F32 = jnp.float32
BF16 = jnp.bfloat16

LANES = 128
BF16_SUBLANES = 16
V7X_VMEM_LIMIT_BYTES = 60 * 1024 * 1024

D_MODEL = 1024
HEAD_DIM = 64
HEADS_PER_GROUP = 8
DILATIONS = (1, 4, 16)
BAND = 128
GROUP_WIDTH = HEADS_PER_GROUP * HEAD_DIM
ATT_WIDTH = len(DILATIONS) * GROUP_WIDTH
PAIRS = GROUP_WIDTH // LANES
ROPE_THETA = 10000.0

RET_HEADS = 4
RET_KEY_DIM = 128
RET_VALUE_DIM = 256
RET_QK_WIDTH = RET_HEADS * RET_KEY_DIM
RET_V_WIDTH = RET_HEADS * RET_VALUE_DIM
RET_CHUNK = 128

FFN_HIDDEN = 2816
NORM_EPS = 1e-6
MASK_VALUE = -1e30

IN_WIDTH = 3 * ATT_WIDTH + 2 * RET_QK_WIDTH + 2 * RET_V_WIDTH + 2 * D_MODEL
COL_TILE = 512
ATT_TILES = 3 * ATT_WIDTH // COL_TILE
REST_SLABS = (IN_WIDTH - 3 * ATT_WIDTH) // LANES
REST_QR, REST_KR, REST_VR, REST_GR, REST_GA, REST_GRT = 0, 4, 8, 16, 24, 32

TM_IN = 512
TM_OUT = 512
FFN_CHUNK = 256
ATT_BODY_BLOCKS = 32
MERGE_ROWS = 64
MERGE_UNROLL = 8
ATT_SKEW = 2
LOG2_E = 1.4426950408889634
RET_UNROLL = 32
RET_SCAN_UNROLL = 8


def _rms(x, g):
    ms = jnp.mean(x * x, axis=-1, keepdims=True)
    return x * lax.rsqrt(ms + NORM_EPS) * g


def _rotate(t, cos, sin_signed, span):
    lane = lax.broadcasted_iota(jnp.int32, (1, LANES), 1)
    low = (lane % (2 * span)) < span
    partner = jnp.where(low, pltpu.roll(t, LANES - span, 1), pltpu.roll(t, span, 1))
    return t * cos + partner * sin_signed


W_STAGE_ROWS, W_STAGE_COLS = 1024, 256
W_STAGE_SLOTS = 4


def _weight_tiles(shape):
    rows, cols = shape
    return [(r, min(W_STAGE_ROWS, rows - r), c, min(W_STAGE_COLS, cols - c))
            for r in range(0, rows, W_STAGE_ROWS) for c in range(0, cols, W_STAGE_COLS)]


def _load_weights_bf16(jobs, stage, sems):
    slots = stage.shape[0]

    def copy(k):
        src, _, (r0, nr, c0, nc), _, _ = jobs[k]
        return pltpu.make_async_copy(src.at[pl.ds(r0, nr), pl.ds(c0, nc)],
                                     stage.at[k % slots, pl.ds(0, nr), pl.ds(0, nc)],
                                     sems.at[k % slots])

    for k in range(min(slots - 1, len(jobs))):
        copy(k).start()
    for k, (_, dst, (r0, nr, c0, nc), row_gain, scale) in enumerate(jobs):
        if k + slots - 1 < len(jobs):
            copy(k + slots - 1).start()
        copy(k).wait()
        for c in range(0, nc, LANES):
            tile = stage[k % slots, 0:nr, c:c + LANES]
            if row_gain is not None:
                tile = tile * row_gain[r0:r0 + nr, :]
            if scale != 1.0:
                tile = tile * scale
            dst[r0:r0 + nr, c0 + c:c0 + c + LANES] = tile.astype(BF16)


def _inproj_kernel(x_ref, w_hbm, gain_ref, ca, sa, cr, sr,
                   a0_ref, a1_ref, a2_ref, rest_ref, y_scr, h_scr, tab_scr, rs_scr,
                   w_ref, w_stage, w_sems):
    tm = x_ref.shape[0]
    n_slabs = D_MODEL // LANES

    @pl.when(pl.program_id(0) == 0)
    def _():
        q_scale = HEAD_DIM ** -0.5 * LOG2_E
        _load_weights_bf16(
            [(w_hbm, w_ref, t, gain_ref, q_scale if t[2] < ATT_WIDTH else 1.0)
             for t in _weight_tiles((D_MODEL, IN_WIDTH))], w_stage, w_sems)

    x = x_ref[...]
    h_scr[0] = x.astype(BF16)
    rs_scr[0] = jnp.broadcast_to(lax.rsqrt(jnp.mean(x * x, axis=-1, keepdims=True) + NORM_EPS),
                                 (tm, LANES))
    for k in range(n_slabs):
        y_scr[k] = x[:, k * LANES:(k + 1) * LANES]
    for gi, d in enumerate(DILATIONS):
        if d == 1:
            continue
        n = tm // d
        for r in range(d):
            rs_scr[gi, r * n:(r + 1) * n, :] = rs_scr[0, pl.ds(r, n, stride=d), :]
            for k in range(n_slabs):
                h_scr[gi, r * n:(r + 1) * n, k * LANES:(k + 1) * LANES] = (
                    y_scr[k, pl.ds(r, n, stride=d), :].astype(BF16))

    for gi, d in enumerate(DILATIONS):
        if d == 1:
            continue
        n = tm // d
        for r in range(d):
            tab_scr[gi - 1, 0, r * n:(r + 1) * n, :] = ca[pl.ds(r, n, stride=d), :]
            tab_scr[gi - 1, 1, r * n:(r + 1) * n, :] = sa[pl.ds(r, n, stride=d), :]

    def att_tables(g):
        if g == 0:
            return ca[...], sa[...]
        return tab_scr[g - 1, 0], tab_scr[g - 1, 1]

    def write_att(g, slab, val):
        v = val.astype(BF16)
        d = DILATIONS[g]
        n = tm // d
        if g == 0:
            a0_ref[slab] = v
        else:
            out = a1_ref if g == 1 else a2_ref
            for r in range(d):
                out[slab, r] = v[r * n:(r + 1) * n, :]

    def att_tile(c):
        kind, g = divmod(c, 3)
        res = jnp.dot(h_scr[g], w_ref[:, c * COL_TILE:(c + 1) * COL_TILE],
                      preferred_element_type=F32)
        for k in range(PAIRS):
            s = res[:, k * LANES:(k + 1) * LANES] * rs_scr[g]
            if kind < 2:
                s = _rotate(s, *att_tables(g), span=HEAD_DIM // 2)
            write_att(g, kind * PAIRS + k, s)

    def rest_tile(c):
        col = 3 * ATT_WIDTH + c * COL_TILE
        res = jnp.dot(h_scr[0], w_ref[:, col:col + COL_TILE], preferred_element_type=F32)
        for k in range(COL_TILE // LANES):
            slab = c * (COL_TILE // LANES) + k
            s = res[:, k * LANES:(k + 1) * LANES] * rs_scr[0]
            if slab < REST_VR:
                s = _rotate(s, cr[...], sr[...], span=1)
                if slab >= REST_KR:
                    s = s * (RET_KEY_DIM ** -0.5)
            rest_ref[slab] = s.astype(BF16)

    for c in range(REST_SLABS * LANES // COL_TILE):
        rest_tile(c)
    for g in range(len(DILATIONS)):
        for kind in range(3):
            att_tile(kind * 3 + g)


def _in_projection(x2d, w, gain, tabs, batch, seq):
    t = x2d.shape[0]
    tm = TM_IN
    nt = seq // tm
    n_steps = t // tm
    tab_spec = pl.BlockSpec((tm, LANES), lambda i: (i % nt, 0))
    in_specs = [
        pl.BlockSpec((tm, D_MODEL), lambda i: (i, 0)),
        pl.BlockSpec(memory_space=pl.ANY),
        pl.BlockSpec((D_MODEL, LANES), lambda i: (0, 0), pipeline_mode=pl.Buffered(1)),
    ] + [tab_spec] * 4
    n_att = 3 * PAIRS
    d1, d2 = DILATIONS[1], DILATIONS[2]
    out_shape = [
        jax.ShapeDtypeStruct((n_att, t, LANES), BF16),
        jax.ShapeDtypeStruct((n_att, batch, d1, seq // d1, LANES), BF16),
        jax.ShapeDtypeStruct((n_att, batch, d2, seq // d2, LANES), BF16),
        jax.ShapeDtypeStruct((REST_SLABS, t, LANES), BF16),
    ]
    out_specs = [
        pl.BlockSpec((n_att, tm, LANES), lambda i: (0, i, 0)),
        pl.BlockSpec((n_att, None, d1, tm // d1, LANES), lambda i: (0, i // nt, 0, i % nt, 0)),
        pl.BlockSpec((n_att, None, d2, tm // d2, LANES), lambda i: (0, i // nt, 0, i % nt, 0)),
        pl.BlockSpec((REST_SLABS, tm, LANES), lambda i: (0, i, 0)),
    ]
    return pl.pallas_call(
        _inproj_kernel,
        grid=(n_steps,),
        in_specs=in_specs,
        out_specs=out_specs,
        out_shape=out_shape,
        scratch_shapes=[
            pltpu.VMEM((D_MODEL // LANES, tm, LANES), F32),
            pltpu.VMEM((len(DILATIONS), tm, D_MODEL), BF16),
            pltpu.VMEM((len(DILATIONS) - 1, 2, tm, LANES), F32),
            pltpu.VMEM((len(DILATIONS), tm, LANES), F32),
            pltpu.VMEM((D_MODEL, IN_WIDTH), BF16),
            pltpu.VMEM((W_STAGE_SLOTS, W_STAGE_ROWS, W_STAGE_COLS), F32),
            pltpu.SemaphoreType.DMA((W_STAGE_SLOTS,)),
        ],
        compiler_params=pltpu.CompilerParams(
            dimension_semantics=("arbitrary",), vmem_limit_bytes=V7X_VMEM_LIMIT_BYTES),
        name="in_projection",
    )(x2d, w, gain, *tabs)


def _pitch(d):
    return d + 1 if d % 8 == 0 else d


def _attn_scores(q, kk, bias, first_head):
    zero = jnp.zeros_like(q)
    q2 = jnp.concatenate([jnp.where(first_head, q, zero), jnp.where(first_head, zero, q)], axis=0)
    return lax.dot_general(q2, kk, (((1,), (1,)), ((), ())), preferred_element_type=F32) + bias


def _attn_partials(s, vv, first_head):
    nq = s.shape[0] // 2
    m = jnp.max(s, axis=1, keepdims=True)
    p = jnp.exp2(s - m)
    l = jnp.sum(p, axis=1, keepdims=True)
    pv = jnp.dot(p.astype(BF16), vv, preferred_element_type=F32)
    acc = jnp.where(first_head, pv[:nq], pv[nq:])
    den = jnp.where(first_head, l[:nq], l[nq:])
    mm = jnp.where(first_head, m[:nq], m[nq:])
    return acc, den, mm


def _attn_kernel(q0, k0, v0, q1, k1, v1, q2, k2, v2, o_ref,
                 acc0, den0, max0, acc1, den1, max1, acc2, den2, max2):
    blk = BAND
    seq = q0.shape[0]
    lane = lax.broadcasted_iota(jnp.int32, (1, LANES), 1)
    first_head = lane < HEAD_DIM
    qi = lax.broadcasted_iota(jnp.int32, (2 * blk, 2 * blk), 0) & (blk - 1)
    kj = lax.broadcasted_iota(jnp.int32, (2 * blk, 2 * blk), 1)
    band_bias = jnp.where((kj >= qi) & (kj <= qi + BAND), 0.0, MASK_VALUE).astype(F32)
    qi1 = lax.broadcasted_iota(jnp.int32, (2 * blk, blk), 0) & (blk - 1)
    kj1 = lax.broadcasted_iota(jnp.int32, (2 * blk, blk), 1)
    first_bias = jnp.where(kj1 <= qi1, 0.0, MASK_VALUE).astype(F32)

    def key_rows(n):
        if isinstance(n, int) and n == 0:
            return pl.ds(0, blk), first_bias
        st = n * blk if isinstance(n, int) else pl.multiple_of(n * blk, blk)
        return pl.ds(st - blk, 2 * blk), band_bias

    def query_rows(n):
        return pl.ds(n * blk if isinstance(n, int) else pl.multiple_of(n * blk, blk), blk)

    def run_blocks(blocks, emit):
        scores = []
        for i in range(len(blocks) + ATT_SKEW):
            if i < len(blocks):
                qr, kr, _, n = blocks[i]
                rows, bias = key_rows(n)
                scores.append(_attn_scores(qr[query_rows(n), :], kr[rows, :], bias, first_head))
            if i >= ATT_SKEW:
                j = i - ATT_SKEW
                _, _, vr, n = blocks[j]
                rows, _ = key_rows(n)
                emit(j, _attn_partials(scores[j], vr[rows, :], first_head))
                scores[j] = None

    def dilated_group(qr, kr, vr, outs, d):
        nb = seq // d // blk
        subs = max(1, ATT_BODY_BLOCKS // nb)

        def body(j, carry):
            blocks, where = [], []
            for ri in range(subs):
                r = j * subs + ri
                for n in range(nb):
                    blocks.append((qr.at[r], kr.at[r], vr.at[r], n))
                    where.append(pl.ds(n * (blk * _pitch(d)) + r, blk, stride=_pitch(d)))

            def emit(i, parts):
                for ref, val in zip(outs, parts):
                    ref[where[i], :] = val

            run_blocks(blocks, emit)
            return carry

        lax.fori_loop(0, d // subs, body, 0)

    dilated_group(q1, k1, v1, (acc1, den1, max1), DILATIONS[1])
    dilated_group(q2, k2, v2, (acc2, den2, max2), DILATIONS[2])

    def dense_body(j, carry):
        ns = [j * ATT_BODY_BLOCKS + i for i in range(ATT_BODY_BLOCKS)]

        def emit(i, parts):
            for ref, val in zip((acc0, den0, max0), parts):
                ref[query_rows(ns[i]), :] = val

        run_blocks([(q0, k0, v0, n) for n in ns], emit)
        return carry

    dense_body(0, 0)
    lax.fori_loop(1, seq // blk // ATT_BODY_BLOCKS, dense_body, 0)

    def merge(t, carry):
        rows = pl.ds(pl.multiple_of(t * MERGE_ROWS, MERGE_ROWS), MERGE_ROWS)

        def padded(ref, d):
            if _pitch(d) == d:
                return ref[rows, :]
            per = MERGE_ROWS // d
            return jnp.concatenate(
                [ref[pl.ds((t * per + i) * _pitch(d), d), :] for i in range(per)], axis=0)

        d2 = DILATIONS[2]
        m0, m1, m2 = max0[rows, :], max1[rows, :], padded(max2, d2)
        mx = jnp.maximum(m0, jnp.maximum(m1, m2))
        e0, e1, e2 = jnp.exp2(m0 - mx), jnp.exp2(m1 - mx), jnp.exp2(m2 - mx)
        num = e0 * acc0[rows, :] + e1 * acc1[rows, :] + e2 * padded(acc2, d2)
        den = e0 * den0[rows, :] + e1 * den1[rows, :] + e2 * padded(den2, d2)
        o_ref[rows, :] = (num * (1.0 / den)).astype(o_ref.dtype)
        return carry

    lax.fori_loop(0, seq // MERGE_ROWS, merge, 0, unroll=MERGE_UNROLL)


def _dilated_attention(a0, a1, a2, batch, seq):
    d1, d2 = DILATIONS[1], DILATIONS[2]
    specs0 = [pl.BlockSpec((None, seq, LANES), functools.partial(lambda b, p, k: (k * PAIRS + p, b, 0), k=k))
              for k in range(3)]
    specs1 = [pl.BlockSpec((None, None, d1, seq // d1, LANES),
                           functools.partial(lambda b, p, k: (k * PAIRS + p, b, 0, 0, 0), k=k))
              for k in range(3)]
    specs2 = [pl.BlockSpec((None, None, d2, seq // d2, LANES),
                           functools.partial(lambda b, p, k: (k * PAIRS + p, b, 0, 0, 0), k=k))
              for k in range(3)]
    in_specs = specs0 + specs1 + specs2
    return pl.pallas_call(
        _attn_kernel,
        grid=(batch, PAIRS),
        in_specs=in_specs,
        out_specs=pl.BlockSpec((None, seq, LANES), lambda b, p: (p, b, 0)),
        out_shape=jax.ShapeDtypeStruct((PAIRS, batch * seq, LANES), BF16),
        scratch_shapes=[pltpu.VMEM((seq // d * _pitch(d), LANES), F32) for d in DILATIONS for _ in range(3)],
        compiler_params=pltpu.CompilerParams(
            dimension_semantics=("arbitrary", "arbitrary"), vmem_limit_bytes=V7X_VMEM_LIMIT_BYTES),
        name="dilated_attention",
    )(a0, a0, a0, a1, a1, a1, a2, a2, a2)


def _retention_kernel(q_ref, k_ref, v_ref, dec_ref, zeta_ref, xi_ref, cd_ref, o_ref, kv_scr, state_scr):
    c_len = RET_CHUNK
    n_chunks = q_ref.shape[0] // c_len
    dec = dec_ref[...]
    zeta = zeta_ref[...]
    xi = xi_ref[...]
    xi2 = jnp.concatenate([xi, xi], axis=1)
    cd = cd_ref[...]
    cd2 = jnp.concatenate([cd, cd], axis=1)

    def chunk_rows(c):
        return pl.ds(pl.multiple_of(c * c_len, c_len), c_len)

    def values(rows):
        return jnp.concatenate([v_ref[0, rows, :], v_ref[1, rows, :]], axis=1)

    def chunk_kv(c, carry):
        rows = chunk_rows(c)
        kz = (k_ref[rows, :].astype(F32) * zeta).astype(BF16)
        kv_scr[c] = lax.dot_general(kz, values(rows), (((0,), (0,)), ((), ())),
                                    preferred_element_type=F32)
        return carry

    lax.fori_loop(0, n_chunks, chunk_kv, 0, unroll=RET_UNROLL)

    def chunk_state(c, state):
        state_scr[c] = state.astype(BF16)
        return state * cd2 + kv_scr[c]

    lax.fori_loop(0, n_chunks, chunk_state, jnp.zeros((RET_KEY_DIM, RET_VALUE_DIM), F32),
                  unroll=RET_SCAN_UNROLL)

    def chunk_group_out(j, carry):
        chunks = [j * RET_UNROLL + i for i in range(RET_UNROLL)]
        scores = []
        for c in chunks:
            rows = chunk_rows(c)
            a = lax.dot_general(q_ref[rows, :], k_ref[rows, :], (((1,), (1,)), ((), ())),
                                preferred_element_type=F32) * dec
            scores.append(a.astype(BF16))
        for c, a in zip(chunks, scores):
            rows = chunk_rows(c)
            lhs = jnp.concatenate([a, q_ref[rows, :]], axis=1)
            rhs = jnp.concatenate([values(rows), state_scr[c]], axis=0)
            o = jnp.dot(lhs, rhs, preferred_element_type=F32) * xi2
            o_ref[0, rows, :] = o[:, :LANES].astype(o_ref.dtype)
            o_ref[1, rows, :] = o[:, LANES:].astype(o_ref.dtype)
        return carry

    lax.fori_loop(0, n_chunks // RET_UNROLL, chunk_group_out, 0)


def _retention(rest, tabs, batch, seq):
    dec, zeta, xi, cd = tabs
    vs = RET_VALUE_DIM // LANES
    tab_spec = pl.BlockSpec((None, RET_CHUNK, LANES), lambda b, h: (h, 0, 0))
    in_specs = [
        pl.BlockSpec((None, seq, LANES), lambda b, h: (REST_QR + h, b, 0)),
        pl.BlockSpec((None, seq, LANES), lambda b, h: (REST_KR + h, b, 0)),
        pl.BlockSpec((vs, seq, LANES), lambda b, h: (REST_VR // vs + h, b, 0)),
        tab_spec, tab_spec, tab_spec,
        pl.BlockSpec((None, 1, LANES), lambda b, h: (h, 0, 0)),
    ]
    return pl.pallas_call(
        _retention_kernel,
        grid=(batch, RET_HEADS),
        in_specs=in_specs,
        out_specs=pl.BlockSpec((vs, seq, LANES), lambda b, h: (h, b, 0)),
        out_shape=jax.ShapeDtypeStruct((RET_V_WIDTH // LANES, batch * seq, LANES), BF16),
        scratch_shapes=[pltpu.VMEM((seq // RET_CHUNK, RET_KEY_DIM, RET_VALUE_DIM), F32),
                        pltpu.VMEM((seq // RET_CHUNK, RET_KEY_DIM, RET_VALUE_DIM), BF16)],
        compiler_params=pltpu.CompilerParams(
            dimension_semantics=("arbitrary", "arbitrary"), vmem_limit_bytes=V7X_VMEM_LIMIT_BYTES),
        name="retention",
    )(rest, rest, rest, dec, zeta, xi, cd)


def _slabs(ref, lo=0, hi=None):
    hi = ref.shape[0] if hi is None else hi
    return jnp.concatenate([ref[k] for k in range(lo, hi)], axis=1)


def _tail_kernel(oa_ref, ret_ref, sg_ref, ga_ref, gr_ref, x_ref, wa_hbm, wr_hbm, wo_hbm,
                 wg_hbm, wu_hbm, wd_hbm, gain_ref, gf_ref, out_ref, ret_scr, act_scr,
                 wa_ref, wr_ref, wo_ref, wg_ref, wu_ref, wd_ref, w_stage, w_sems):
    @pl.when(pl.program_id(0) == 0)
    def _():
        jobs = []
        for src, dst, gain in ((wa_hbm, wa_ref, None), (wr_hbm, wr_ref, None), (wo_hbm, wo_ref, None),
                               (wg_hbm, wg_ref, gain_ref), (wu_hbm, wu_ref, gain_ref),
                               (wd_hbm, wd_ref, None)):
            jobs += [(src, dst, t, gain, 1.0) for t in _weight_tiles(dst.shape)]
        _load_weights_bf16(jobs, w_stage, w_sems)

    vs = RET_VALUE_DIM // LANES
    for h in range(RET_HEADS):
        o = _slabs(ret_ref, h * vs, (h + 1) * vs).astype(F32)
        oc = o - jnp.mean(o, axis=-1, keepdims=True)
        var = jnp.mean(oc * oc, axis=-1, keepdims=True)
        gate = _slabs(sg_ref, h * vs, (h + 1) * vs).astype(F32)
        gated = oc * lax.rsqrt(var + NORM_EPS) * (gate * jax.nn.sigmoid(gate))
        ret_scr[:, h * RET_VALUE_DIM:(h + 1) * RET_VALUE_DIM] = gated.astype(BF16)
    ya = jnp.dot(_slabs(oa_ref), wa_ref[...], preferred_element_type=F32)
    yr = jnp.dot(ret_scr[...], wr_ref[...], preferred_element_type=F32)
    merged = (jax.nn.sigmoid(_slabs(ga_ref).astype(F32)) * ya
              + jax.nn.sigmoid(_slabs(gr_ref).astype(F32)) * yr)
    x1 = x_ref[...] + jnp.dot(merged.astype(BF16), wo_ref[...], preferred_element_type=F32)
    scale = jnp.broadcast_to(lax.rsqrt(jnp.mean(x1 * x1, axis=-1, keepdims=True) + NORM_EPS),
                             (x1.shape[0], FFN_CHUNK))
    h2 = x1.astype(BF16)
    for c in range(0, FFN_HIDDEN, FFN_CHUNK):
        gate = jnp.dot(h2, wg_ref[:, c:c + FFN_CHUNK], preferred_element_type=F32) * scale
        up = jnp.dot(h2, wu_ref[:, c:c + FFN_CHUNK], preferred_element_type=F32) * scale
        act_scr[:, c:c + FFN_CHUNK] = (gate * jax.nn.sigmoid(gate) * up).astype(BF16)
    down = jnp.dot(act_scr[...], wd_ref[...], preferred_element_type=F32)
    out_ref[...] = _rms(x1 + down, gf_ref[...])


def _tail(oa, ret, rest, x2d, wa, wr, wo, wg, wu, wd, gain, gf):
    t = x2d.shape[0]
    tm = TM_OUT
    gs = D_MODEL // LANES
    full = lambda shape: pl.BlockSpec(shape, lambda i: (0,) * len(shape), pipeline_mode=pl.Buffered(1))
    vec = pl.BlockSpec((1, D_MODEL), lambda i: (0, 0))
    row = pl.BlockSpec((tm, D_MODEL), lambda i: (i, 0))
    in_specs = [
        pl.BlockSpec((PAIRS, tm, LANES), lambda i: (0, i, 0)),
        pl.BlockSpec((gs, tm, LANES), lambda i: (0, i, 0)),
        pl.BlockSpec((gs, tm, LANES), lambda i: (REST_GR // gs, i, 0)),
        pl.BlockSpec((gs, tm, LANES), lambda i: (REST_GA // gs, i, 0)),
        pl.BlockSpec((gs, tm, LANES), lambda i: (REST_GRT // gs, i, 0)),
        row,
    ] + [pl.BlockSpec(memory_space=pl.ANY)] * 6 + [full((D_MODEL, LANES)), vec]
    weight_shapes = [(GROUP_WIDTH, D_MODEL), (RET_V_WIDTH, D_MODEL), (D_MODEL, D_MODEL),
                     (D_MODEL, FFN_HIDDEN), (D_MODEL, FFN_HIDDEN), (FFN_HIDDEN, D_MODEL)]
    return pl.pallas_call(
        _tail_kernel,
        grid=(t // tm,),
        in_specs=in_specs,
        out_specs=row,
        out_shape=jax.ShapeDtypeStruct((t, D_MODEL), F32),
        scratch_shapes=[pltpu.VMEM((tm, RET_V_WIDTH), BF16), pltpu.VMEM((tm, FFN_HIDDEN), BF16)]
        + [pltpu.VMEM(shape, BF16) for shape in weight_shapes]
        + [pltpu.VMEM((W_STAGE_SLOTS, W_STAGE_ROWS, W_STAGE_COLS), F32), pltpu.SemaphoreType.DMA((W_STAGE_SLOTS,))],
        compiler_params=pltpu.CompilerParams(
            dimension_semantics=("arbitrary",), vmem_limit_bytes=V7X_VMEM_LIMIT_BYTES),
        name="merge_ffn",
    )(oa, ret, rest, rest, rest, x2d, wa, wr, wo, wg, wu, wd, gain, gf)


def _row_gain(g):
    return jnp.broadcast_to(g[:, None], (g.shape[0], LANES))


def _position_tables(seq):
    pos = np.arange(seq, dtype=np.float64)
    inv = ROPE_THETA ** (-np.arange(0, HEAD_DIM, 2, dtype=np.float64) / HEAD_DIM)
    ang = pos[:, None] * inv[None, :]
    c, s = np.cos(ang), np.sin(ang)
    cos_a = np.concatenate([c, c, c, c], axis=1)
    sin_a = np.concatenate([-s, s, -s, s], axis=1)
    base = 1.0 / (ROPE_THETA ** np.linspace(0.0, 1.0, RET_KEY_DIM // 2, dtype=np.float64))
    ang_r = pos[:, None] * base[None, :]
    cr, sr = np.cos(ang_r), np.sin(ang_r)
    cos_r = np.repeat(cr, 2, axis=1)
    sin_r = np.repeat(sr, 2, axis=1) * np.tile([-1.0, 1.0], RET_KEY_DIM // 2)[None, :]
    return [jnp.asarray(t.astype(np.float32)) for t in (cos_a, sin_a, cos_r, sin_r)]


def _retention_tables():
    c = RET_CHUNK
    log_g = np.log1p(-(2.0 ** (-5.0 - np.arange(RET_HEADS, dtype=np.float64))))
    idx = np.arange(c, dtype=np.float64)
    diff = idx[:, None] - idx[None, :]
    decay = np.where(diff[None] >= 0, np.exp(-(idx + 1.0)[None, None, :] * log_g[:, None, None]), 0.0)
    zeta = np.exp((c - 1 - idx)[None, :] * log_g[:, None])
    xi = np.exp((idx + 1.0)[None, :] * log_g[:, None])
    chunk_decay = np.exp(c * log_g)
    bcast = lambda v: np.broadcast_to(v[:, :, None], (RET_HEADS, c, LANES))
    cd = np.broadcast_to(chunk_decay[:, None, None], (RET_HEADS, 1, LANES))
    return tuple(jnp.asarray(np.ascontiguousarray(t).astype(np.float32))
                 for t in (decay, bcast(zeta), bcast(xi), cd))


def kernel(x, norm_mix_g, w_in, w_out_attn, w_out_ret, w_out, norm_ffn_g, w_ffn_gate, w_ffn_up,
           w_ffn_down, norm_final_g):
    batch, seq, _ = x.shape
    assert w_in.shape[0] == 1, "single-layer problem"
    assert seq % TM_IN == 0 and TM_IN % (DILATIONS[-1] * BF16_SUBLANES) == 0
    x2d = x.reshape(batch * seq, D_MODEL)
    a0, a1, a2, rest = _in_projection(
        x2d, w_in[0], _row_gain(norm_mix_g[0]), _position_tables(seq), batch, seq)
    oa = _dilated_attention(a0, a1, a2, batch, seq)
    ret = _retention(rest, _retention_tables(), batch, seq)
    out = _tail(oa, ret, rest, x2d, w_out_attn[0], w_out_ret[0], w_out[0], w_ffn_gate[0],
                w_ffn_up[0], w_ffn_down[0], _row_gain(norm_ffn_g[0]), norm_final_g[None, :])
    return out.reshape(batch, seq, D_MODEL)
```

```python
import functools

import jax
import jax.numpy as jnp
import numpy as np
from jax import lax
from jax.experimental import pallas as pl
from jax.experimental.pallas import tpu as pltpu

F32 = jnp.float32
BF16 = jnp.bfloat16

LANES = 128
BF16_SUBLANES = 16
V7X_VMEM_LIMIT_BYTES = 60 * 1024 * 1024

D_MODEL = 1024
HEAD_DIM = 64
HEADS_PER_GROUP = 8
DILATIONS = (1, 4, 16)
BAND = 128
GROUP_WIDTH = HEADS_PER_GROUP * HEAD_DIM
ATT_WIDTH = len(DILATIONS) * GROUP_WIDTH
PAIRS = GROUP_WIDTH // LANES
ROPE_THETA = 10000.0

RET_HEADS = 4
RET_KEY_DIM = 128
RET_VALUE_DIM = 256
RET_QK_WIDTH = RET_HEADS * RET_KEY_DIM
RET_V_WIDTH = RET_HEADS * RET_VALUE_DIM
RET_CHUNK = 128

FFN_HIDDEN = 2816
NORM_EPS = 1e-6
MASK_VALUE = -1e30

IN_WIDTH = 3 * ATT_WIDTH + 2 * RET_QK_WIDTH + 2 * RET_V_WIDTH + 2 * D_MODEL
COL_TILE = 512
ATT_TILES = 3 * ATT_WIDTH // COL_TILE
REST_SLABS = (IN_WIDTH - 3 * ATT_WIDTH) // LANES
REST_QR, REST_KR, REST_VR, REST_GR, REST_GA, REST_GRT = 0, 4, 8, 16, 24, 32

TM_IN = 512
TM_OUT = 512
FFN_CHUNK = 256
ATT_BODY_BLOCKS = 32
MERGE_ROWS = 64
MERGE_UNROLL = 8
ATT_SKEW = 2
LOG2_E = 1.4426950408889634
RET_UNROLL = 32
RET_SCAN_UNROLL = 8


def _rms(x, g):
    ms = jnp.mean(x * x, axis=-1, keepdims=True)
    return x * lax.rsqrt(ms + NORM_EPS) * g


def _rotate(t, cos, sin_signed, span):
    lane = lax.broadcasted_iota(jnp.int32, (1, LANES), 1)
    low = (lane % (2 * span)) < span
    partner = jnp.where(low, pltpu.roll(t, LANES - span, 1), pltpu.roll(t, span, 1))
    return t * cos + partner * sin_signed


W_STAGE_BYTES = 5 << 18
W_STAGE_SLOTS = 4


def _stage_rows(cols):
    return W_STAGE_BYTES // (4 * cols) // BF16_SUBLANES * BF16_SUBLANES


def _load_weights_bf16(jobs, stage, sems):
    slots, stage_rows, _ = stage.shape
    tiles = [(j, r0, min(stage_rows, jobs[j][1].shape[0] - r0))
             for j in range(len(jobs)) for r0 in range(0, jobs[j][1].shape[0], stage_rows)]

    def copy(k):
        j, r0, nr = tiles[k]
        cols = jobs[j][1].shape[1]
        return pltpu.make_async_copy(jobs[j][0].at[pl.ds(r0, nr), :],
                                     stage.at[k % slots, pl.ds(0, nr), pl.ds(0, cols)],
                                     sems.at[k % slots])

    for k in range(min(slots - 1, len(tiles))):
        copy(k).start()
    for k, (j, r0, nr) in enumerate(tiles):
        _, dst, row_gain, scale, scaled_cols = jobs[j]
        if k + slots - 1 < len(tiles):
            copy(k + slots - 1).start()
        copy(k).wait()
        for c in range(0, dst.shape[1], LANES):
            tile = stage[k % slots, 0:nr, c:c + LANES]
            if row_gain is not None:
                tile = tile * row_gain[r0:r0 + nr, :]
            if c < scaled_cols:
                tile = tile * scale
            dst[r0:r0 + nr, c:c + LANES] = tile.astype(BF16)


def _inproj_kernel(x_ref, w_hbm, gain_ref, ca, sa, cr, sr,
                   a0_ref, a1_ref, a2_ref, rest_ref, y_scr, h_scr, tab_scr, rs_scr,
                   w_ref, w_stage, w_sems):
    tm = x_ref.shape[0]
    n_slabs = D_MODEL // LANES

    @pl.when(pl.program_id(0) == 0)
    def _():
        _load_weights_bf16([(w_hbm, w_ref, gain_ref, HEAD_DIM ** -0.5 * LOG2_E, ATT_WIDTH)],
                           w_stage, w_sems)

    x = x_ref[...]
    h_scr[0] = x.astype(BF16)
    rs_scr[0] = jnp.broadcast_to(lax.rsqrt(jnp.mean(x * x, axis=-1, keepdims=True) + NORM_EPS),
                                 (tm, LANES))
    for k in range(n_slabs):
        y_scr[k] = x[:, k * LANES:(k + 1) * LANES]
    for gi, d in enumerate(DILATIONS):
        if d == 1:
            continue
        n = tm // d
        for r in range(d):
            rs_scr[gi, r * n:(r + 1) * n, :] = rs_scr[0, pl.ds(r, n, stride=d), :]
            for k in range(n_slabs):
                h_scr[gi, r * n:(r + 1) * n, k * LANES:(k + 1) * LANES] = (
                    y_scr[k, pl.ds(r, n, stride=d), :].astype(BF16))

    for gi, d in enumerate(DILATIONS):
        if d == 1:
            continue
        n = tm // d
        for r in range(d):
            tab_scr[gi - 1, 0, r * n:(r + 1) * n, :] = ca[pl.ds(r, n, stride=d), :]
            tab_scr[gi - 1, 1, r * n:(r + 1) * n, :] = sa[pl.ds(r, n, stride=d), :]

    def att_tables(g):
        if g == 0:
            return ca[...], sa[...]
        return tab_scr[g - 1, 0], tab_scr[g - 1, 1]

    def write_att(g, slab, val):
        v = val.astype(BF16)
        d = DILATIONS[g]
        n = tm // d
        if g == 0:
            a0_ref[slab] = v
        else:
            out = a1_ref if g == 1 else a2_ref
            for r in range(d):
                out[slab, r] = v[r * n:(r + 1) * n, :]

    def att_tile(c):
        kind, g = divmod(c, 3)
        res = jnp.dot(h_scr[g], w_ref[:, c * COL_TILE:(c + 1) * COL_TILE],
                      preferred_element_type=F32)
        for k in range(PAIRS):
            s = res[:, k * LANES:(k + 1) * LANES] * rs_scr[g]
            if kind < 2:
                s = _rotate(s, *att_tables(g), span=HEAD_DIM // 2)
            write_att(g, kind * PAIRS + k, s)

    def rest_tile(c):
        col = 3 * ATT_WIDTH + c * COL_TILE
        res = jnp.dot(h_scr[0], w_ref[:, col:col + COL_TILE], preferred_element_type=F32)
        for k in range(COL_TILE // LANES):
            slab = c * (COL_TILE // LANES) + k
            s = res[:, k * LANES:(k + 1) * LANES] * rs_scr[0]
            if slab < REST_VR:
                s = _rotate(s, cr[...], sr[...], span=1)
                if slab >= REST_KR:
                    s = s * (RET_KEY_DIM ** -0.5)
            rest_ref[slab] = s.astype(BF16)

    for c in range(REST_SLABS * LANES // COL_TILE):
        rest_tile(c)
    for g in range(len(DILATIONS)):
        for kind in range(3):
            att_tile(kind * 3 + g)


def _in_projection(x2d, w, gain, tabs, batch, seq):
    t = x2d.shape[0]
    tm = TM_IN
    nt = seq // tm
    n_steps = t // tm
    tab_spec = pl.BlockSpec((tm, LANES), lambda i: (i % nt, 0))
    in_specs = [
        pl.BlockSpec((tm, D_MODEL), lambda i: (i, 0)),
        pl.BlockSpec(memory_space=pl.ANY),
        pl.BlockSpec((D_MODEL, LANES), lambda i: (0, 0), pipeline_mode=pl.Buffered(1)),
    ] + [tab_spec] * 4
    n_att = 3 * PAIRS
    d1, d2 = DILATIONS[1], DILATIONS[2]
    out_shape = [
        jax.ShapeDtypeStruct((n_att, t, LANES), BF16),
        jax.ShapeDtypeStruct((n_att, batch, d1, seq // d1, LANES), BF16),
        jax.ShapeDtypeStruct((n_att, batch, d2, seq // d2, LANES), BF16),
        jax.ShapeDtypeStruct((REST_SLABS, t, LANES), BF16),
    ]
    out_specs = [
        pl.BlockSpec((n_att, tm, LANES), lambda i: (0, i, 0)),
        pl.BlockSpec((n_att, None, d1, tm // d1, LANES), lambda i: (0, i // nt, 0, i % nt, 0)),
        pl.BlockSpec((n_att, None, d2, tm // d2, LANES), lambda i: (0, i // nt, 0, i % nt, 0)),
        pl.BlockSpec((REST_SLABS, tm, LANES), lambda i: (0, i, 0)),
    ]
    return pl.pallas_call(
        _inproj_kernel,
        grid=(n_steps,),
        in_specs=in_specs,
        out_specs=out_specs,
        out_shape=out_shape,
        scratch_shapes=[
            pltpu.VMEM((D_MODEL // LANES, tm, LANES), F32),
            pltpu.VMEM((len(DILATIONS), tm, D_MODEL), BF16),
            pltpu.VMEM((len(DILATIONS) - 1, 2, tm, LANES), F32),
            pltpu.VMEM((len(DILATIONS), tm, LANES), F32),
            pltpu.VMEM((D_MODEL, IN_WIDTH), BF16),
            pltpu.VMEM((W_STAGE_SLOTS, _stage_rows(IN_WIDTH), IN_WIDTH), F32),
            pltpu.SemaphoreType.DMA((W_STAGE_SLOTS,)),
        ],
        compiler_params=pltpu.CompilerParams(
            dimension_semantics=("arbitrary",), vmem_limit_bytes=V7X_VMEM_LIMIT_BYTES),
        name="in_projection",
    )(x2d, w, gain, *tabs)


def _pitch(d):
    return d + 1 if d % 8 == 0 else d


def _attn_scores(q, kk, bias, first_head):
    zero = jnp.zeros_like(q)
    q2 = jnp.concatenate([jnp.where(first_head, q, zero), jnp.where(first_head, zero, q)], axis=0)
    return lax.dot_general(q2, kk, (((1,), (1,)), ((), ())), preferred_element_type=F32) + bias


def _attn_partials(s, vv, first_head):
    nq = s.shape[0] // 2
    m = jnp.max(s, axis=1, keepdims=True)
    p = jnp.exp2(s - m)
    l = jnp.sum(p, axis=1, keepdims=True)
    pv = jnp.dot(p.astype(BF16), vv, preferred_element_type=F32)
    acc = jnp.where(first_head, pv[:nq], pv[nq:])
    den = jnp.where(first_head, l[:nq], l[nq:])
    mm = jnp.where(first_head, m[:nq], m[nq:])
    return acc, den, mm


def _attn_kernel(q0, k0, v0, q1, k1, v1, q2, k2, v2, o_ref,
                 acc0, den0, max0, acc1, den1, max1, acc2, den2, max2):
    blk = BAND
    seq = q0.shape[0]
    lane = lax.broadcasted_iota(jnp.int32, (1, LANES), 1)
    first_head = lane < HEAD_DIM
    qi = lax.broadcasted_iota(jnp.int32, (2 * blk, 2 * blk), 0) & (blk - 1)
    kj = lax.broadcasted_iota(jnp.int32, (2 * blk, 2 * blk), 1)
    band_bias = jnp.where((kj >= qi) & (kj <= qi + BAND), 0.0, MASK_VALUE).astype(F32)
    qi1 = lax.broadcasted_iota(jnp.int32, (2 * blk, blk), 0) & (blk - 1)
    kj1 = lax.broadcasted_iota(jnp.int32, (2 * blk, blk), 1)
    first_bias = jnp.where(kj1 <= qi1, 0.0, MASK_VALUE).astype(F32)

    def key_rows(n):
        if isinstance(n, int) and n == 0:
            return pl.ds(0, blk), first_bias
        st = n * blk if isinstance(n, int) else pl.multiple_of(n * blk, blk)
        return pl.ds(st - blk, 2 * blk), band_bias

    def query_rows(n):
        return pl.ds(n * blk if isinstance(n, int) else pl.multiple_of(n * blk, blk), blk)

    def run_blocks(blocks, emit):
        scores = []
        for i in range(len(blocks) + ATT_SKEW):
            if i < len(blocks):
                qr, kr, _, n = blocks[i]
                rows, bias = key_rows(n)
                scores.append(_attn_scores(qr[query_rows(n), :], kr[rows, :], bias, first_head))
            if i >= ATT_SKEW:
                j = i - ATT_SKEW
                _, _, vr, n = blocks[j]
                rows, _ = key_rows(n)
                emit(j, _attn_partials(scores[j], vr[rows, :], first_head))
                scores[j] = None

    def dilated_group(qr, kr, vr, outs, d):
        nb = seq // d // blk
        subs = max(1, ATT_BODY_BLOCKS // nb)

        def body(j, carry):
            blocks, where = [], []
            for ri in range(subs):
                r = j * subs + ri
                for n in range(nb):
                    blocks.append((qr.at[r], kr.at[r], vr.at[r], n))
                    where.append(pl.ds(n * (blk * _pitch(d)) + r, blk, stride=_pitch(d)))

            def emit(i, parts):
                for ref, val in zip(outs, parts):
                    ref[where[i], :] = val

            run_blocks(blocks, emit)
            return carry

        lax.fori_loop(0, d // subs, body, 0)

    dilated_group(q1, k1, v1, (acc1, den1, max1), DILATIONS[1])
    dilated_group(q2, k2, v2, (acc2, den2, max2), DILATIONS[2])

    def dense_body(j, carry):
        ns = [j * ATT_BODY_BLOCKS + i for i in range(ATT_BODY_BLOCKS)]

        def emit(i, parts):
            for ref, val in zip((acc0, den0, max0), parts):
                ref[query_rows(ns[i]), :] = val

        run_blocks([(q0, k0, v0, n) for n in ns], emit)
        return carry

    dense_body(0, 0)
    lax.fori_loop(1, seq // blk // ATT_BODY_BLOCKS, dense_body, 0)

    def merge(t, carry):
        rows = pl.ds(pl.multiple_of(t * MERGE_ROWS, MERGE_ROWS), MERGE_ROWS)

        def padded(ref, d):
            if _pitch(d) == d:
                return ref[rows, :]
            per = MERGE_ROWS // d
            return jnp.concatenate(
                [ref[pl.ds((t * per + i) * _pitch(d), d), :] for i in range(per)], axis=0)

        d2 = DILATIONS[2]
        m0, m1, m2 = max0[rows, :], max1[rows, :], padded(max2, d2)
        mx = jnp.maximum(m0, jnp.maximum(m1, m2))
        e0, e1, e2 = jnp.exp2(m0 - mx), jnp.exp2(m1 - mx), jnp.exp2(m2 - mx)
        num = e0 * acc0[rows, :] + e1 * acc1[rows, :] + e2 * padded(acc2, d2)
        den = e0 * den0[rows, :] + e1 * den1[rows, :] + e2 * padded(den2, d2)
        o_ref[rows, :] = (num * (1.0 / den)).astype(o_ref.dtype)
        return carry

    lax.fori_loop(0, seq // MERGE_ROWS, merge, 0, unroll=MERGE_UNROLL)


def _dilated_attention(a0, a1, a2, batch, seq):
    d1, d2 = DILATIONS[1], DILATIONS[2]
    specs0 = [pl.BlockSpec((None, seq, LANES), functools.partial(lambda b, p, k: (k * PAIRS + p, b, 0), k=k))
              for k in range(3)]
    specs1 = [pl.BlockSpec((None, None, d1, seq // d1, LANES),
                           functools.partial(lambda b, p, k: (k * PAIRS + p, b, 0, 0, 0), k=k))
              for k in range(3)]
    specs2 = [pl.BlockSpec((None, None, d2, seq // d2, LANES),
                           functools.partial(lambda b, p, k: (k * PAIRS + p, b, 0, 0, 0), k=k))
              for k in range(3)]
    in_specs = specs0 + specs1 + specs2
    return pl.pallas_call(
        _attn_kernel,
        grid=(batch, PAIRS),
        in_specs=in_specs,
        out_specs=pl.BlockSpec((None, seq, LANES), lambda b, p: (p, b, 0)),
        out_shape=jax.ShapeDtypeStruct((PAIRS, batch * seq, LANES), BF16),
        scratch_shapes=[pltpu.VMEM((seq // d * _pitch(d), LANES), F32) for d in DILATIONS for _ in range(3)],
        compiler_params=pltpu.CompilerParams(
            dimension_semantics=("arbitrary", "arbitrary"), vmem_limit_bytes=V7X_VMEM_LIMIT_BYTES),
        name="dilated_attention",
    )(a0, a0, a0, a1, a1, a1, a2, a2, a2)


def _retention_kernel(q_ref, k_ref, v_ref, dec_ref, zeta_ref, xi_ref, cd_ref, o_ref, kv_scr, state_scr):
    c_len = RET_CHUNK
    n_chunks = q_ref.shape[0] // c_len
    dec = dec_ref[...]
    zeta = zeta_ref[...]
    xi = xi_ref[...]
    xi2 = jnp.concatenate([xi, xi], axis=1)
    cd = cd_ref[...]
    cd2 = jnp.concatenate([cd, cd], axis=1)

    def chunk_rows(c):
        return pl.ds(pl.multiple_of(c * c_len, c_len), c_len)

    def values(rows):
        return jnp.concatenate([v_ref[0, rows, :], v_ref[1, rows, :]], axis=1)

    def chunk_kv(c, carry):
        rows = chunk_rows(c)
        kz = (k_ref[rows, :].astype(F32) * zeta).astype(BF16)
        kv_scr[c] = lax.dot_general(kz, values(rows), (((0,), (0,)), ((), ())),
                                    preferred_element_type=F32)
        return carry

    lax.fori_loop(0, n_chunks, chunk_kv, 0, unroll=RET_UNROLL)

    def chunk_state(c, state):
        state_scr[c] = state.astype(BF16)
        return state * cd2 + kv_scr[c]

    lax.fori_loop(0, n_chunks, chunk_state, jnp.zeros((RET_KEY_DIM, RET_VALUE_DIM), F32),
                  unroll=RET_SCAN_UNROLL)

    def chunk_group_out(j, carry):
        chunks = [j * RET_UNROLL + i for i in range(RET_UNROLL)]
        scores = []
        for c in chunks:
            rows = chunk_rows(c)
            a = lax.dot_general(q_ref[rows, :], k_ref[rows, :], (((1,), (1,)), ((), ())),
                                preferred_element_type=F32) * dec
            scores.append(a.astype(BF16))
        for c, a in zip(chunks, scores):
            rows = chunk_rows(c)
            lhs = jnp.concatenate([a, q_ref[rows, :]], axis=1)
            rhs = jnp.concatenate([values(rows), state_scr[c]], axis=0)
            o = jnp.dot(lhs, rhs, preferred_element_type=F32) * xi2
            o_ref[0, rows, :] = o[:, :LANES].astype(o_ref.dtype)
            o_ref[1, rows, :] = o[:, LANES:].astype(o_ref.dtype)
        return carry

    lax.fori_loop(0, n_chunks // RET_UNROLL, chunk_group_out, 0)


def _retention(rest, tabs, batch, seq):
    dec, zeta, xi, cd = tabs
    vs = RET_VALUE_DIM // LANES
    tab_spec = pl.BlockSpec((None, RET_CHUNK, LANES), lambda b, h: (h, 0, 0))
    in_specs = [
        pl.BlockSpec((None, seq, LANES), lambda b, h: (REST_QR + h, b, 0)),
        pl.BlockSpec((None, seq, LANES), lambda b, h: (REST_KR + h, b, 0)),
        pl.BlockSpec((vs, seq, LANES), lambda b, h: (REST_VR // vs + h, b, 0)),
        tab_spec, tab_spec, tab_spec,
        pl.BlockSpec((None, 1, LANES), lambda b, h: (h, 0, 0)),
    ]
    return pl.pallas_call(
        _retention_kernel,
        grid=(batch, RET_HEADS),
        in_specs=in_specs,
        out_specs=pl.BlockSpec((vs, seq, LANES), lambda b, h: (h, b, 0)),
        out_shape=jax.ShapeDtypeStruct((RET_V_WIDTH // LANES, batch * seq, LANES), BF16),
        scratch_shapes=[pltpu.VMEM((seq // RET_CHUNK, RET_KEY_DIM, RET_VALUE_DIM), F32),
                        pltpu.VMEM((seq // RET_CHUNK, RET_KEY_DIM, RET_VALUE_DIM), BF16)],
        compiler_params=pltpu.CompilerParams(
            dimension_semantics=("arbitrary", "arbitrary"), vmem_limit_bytes=V7X_VMEM_LIMIT_BYTES),
        name="retention",
    )(rest, rest, rest, dec, zeta, xi, cd)


def _slabs(ref, lo=0, hi=None):
    hi = ref.shape[0] if hi is None else hi
    return jnp.concatenate([ref[k] for k in range(lo, hi)], axis=1)


def _tail_kernel(oa_ref, ret_ref, sg_ref, ga_ref, gr_ref, x_ref, wa_hbm, wr_hbm, wo_hbm,
                 wg_hbm, wu_hbm, wd_hbm, gain_ref, gf_ref, out_ref, ret_scr, act_scr,
                 wa_ref, wr_ref, wo_ref, wg_ref, wu_ref, wd_ref, w_stage, w_sems):
    @pl.when(pl.program_id(0) == 0)
    def _():
        _load_weights_bf16(
            [(src, dst, gain, 1.0, 0)
             for src, dst, gain in ((wa_hbm, wa_ref, None), (wr_hbm, wr_ref, None),
                                    (wo_hbm, wo_ref, None), (wg_hbm, wg_ref, gain_ref),
                                    (wu_hbm, wu_ref, gain_ref), (wd_hbm, wd_ref, None))],
            w_stage, w_sems)

    vs = RET_VALUE_DIM // LANES
    for h in range(RET_HEADS):
        o = _slabs(ret_ref, h * vs, (h + 1) * vs).astype(F32)
        oc = o - jnp.mean(o, axis=-1, keepdims=True)
        var = jnp.mean(oc * oc, axis=-1, keepdims=True)
        gate = _slabs(sg_ref, h * vs, (h + 1) * vs).astype(F32)
        gated = oc * lax.rsqrt(var + NORM_EPS) * (gate * jax.nn.sigmoid(gate))
        ret_scr[:, h * RET_VALUE_DIM:(h + 1) * RET_VALUE_DIM] = gated.astype(BF16)
    ya = jnp.dot(_slabs(oa_ref), wa_ref[...], preferred_element_type=F32)
    yr = jnp.dot(ret_scr[...], wr_ref[...], preferred_element_type=F32)
    merged = (jax.nn.sigmoid(_slabs(ga_ref).astype(F32)) * ya
              + jax.nn.sigmoid(_slabs(gr_ref).astype(F32)) * yr)
    x1 = x_ref[...] + jnp.dot(merged.astype(BF16), wo_ref[...], preferred_element_type=F32)
    scale = jnp.broadcast_to(lax.rsqrt(jnp.mean(x1 * x1, axis=-1, keepdims=True) + NORM_EPS),
                             (x1.shape[0], FFN_CHUNK))
    h2 = x1.astype(BF16)
    for c in range(0, FFN_HIDDEN, FFN_CHUNK):
        gate = jnp.dot(h2, wg_ref[:, c:c + FFN_CHUNK], preferred_element_type=F32) * scale
        up = jnp.dot(h2, wu_ref[:, c:c + FFN_CHUNK], preferred_element_type=F32) * scale
        act_scr[:, c:c + FFN_CHUNK] = (gate * jax.nn.sigmoid(gate) * up).astype(BF16)
    down = jnp.dot(act_scr[...], wd_ref[...], preferred_element_type=F32)
    out_ref[...] = _rms(x1 + down, gf_ref[...])


def _tail(oa, ret, rest, x2d, wa, wr, wo, wg, wu, wd, gain, gf):
    t = x2d.shape[0]
    tm = TM_OUT
    gs = D_MODEL // LANES
    full = lambda shape: pl.BlockSpec(shape, lambda i: (0,) * len(shape), pipeline_mode=pl.Buffered(1))
    vec = pl.BlockSpec((1, D_MODEL), lambda i: (0, 0))
    row = pl.BlockSpec((tm, D_MODEL), lambda i: (i, 0))
    in_specs = [
        pl.BlockSpec((PAIRS, tm, LANES), lambda i: (0, i, 0)),
        pl.BlockSpec((gs, tm, LANES), lambda i: (0, i, 0)),
        pl.BlockSpec((gs, tm, LANES), lambda i: (REST_GR // gs, i, 0)),
        pl.BlockSpec((gs, tm, LANES), lambda i: (REST_GA // gs, i, 0)),
        pl.BlockSpec((gs, tm, LANES), lambda i: (REST_GRT // gs, i, 0)),
        row,
    ] + [pl.BlockSpec(memory_space=pl.ANY)] * 6 + [full((D_MODEL, LANES)), vec]
    weight_shapes = [(GROUP_WIDTH, D_MODEL), (RET_V_WIDTH, D_MODEL), (D_MODEL, D_MODEL),
                     (D_MODEL, FFN_HIDDEN), (D_MODEL, FFN_HIDDEN), (FFN_HIDDEN, D_MODEL)]
    return pl.pallas_call(
        _tail_kernel,
        grid=(t // tm,),
        in_specs=in_specs,
        out_specs=row,
        out_shape=jax.ShapeDtypeStruct((t, D_MODEL), F32),
        scratch_shapes=[pltpu.VMEM((tm, RET_V_WIDTH), BF16), pltpu.VMEM((tm, FFN_HIDDEN), BF16)]
        + [pltpu.VMEM(shape, BF16) for shape in weight_shapes]
        + [pltpu.VMEM((W_STAGE_SLOTS, _stage_rows(FFN_HIDDEN), FFN_HIDDEN), F32),
           pltpu.SemaphoreType.DMA((W_STAGE_SLOTS,))],
        compiler_params=pltpu.CompilerParams(
            dimension_semantics=("arbitrary",), vmem_limit_bytes=V7X_VMEM_LIMIT_BYTES),
        name="merge_ffn",
    )(oa, ret, rest, rest, rest, x2d, wa, wr, wo, wg, wu, wd, gain, gf)


def _row_gain(g):
    return jnp.broadcast_to(g[:, None], (g.shape[0], LANES))


def _position_tables(seq):
    pos = np.arange(seq, dtype=np.float64)
    inv = ROPE_THETA ** (-np.arange(0, HEAD_DIM, 2, dtype=np.float64) / HEAD_DIM)
    ang = pos[:, None] * inv[None, :]
    c, s = np.cos(ang), np.sin(ang)
    cos_a = np.concatenate([c, c, c, c], axis=1)
    sin_a = np.concatenate([-s, s, -s, s], axis=1)
    base = 1.0 / (ROPE_THETA ** np.linspace(0.0, 1.0, RET_KEY_DIM // 2, dtype=np.float64))
    ang_r = pos[:, None] * base[None, :]
    cr, sr = np.cos(ang_r), np.sin(ang_r)
    cos_r = np.repeat(cr, 2, axis=1)
    sin_r = np.repeat(sr, 2, axis=1) * np.tile([-1.0, 1.0], RET_KEY_DIM // 2)[None, :]
    return [jnp.asarray(t.astype(np.float32)) for t in (cos_a, sin_a, cos_r, sin_r)]


def _retention_tables():
    c = RET_CHUNK
    log_g = np.log1p(-(2.0 ** (-5.0 - np.arange(RET_HEADS, dtype=np.float64))))
    idx = np.arange(c, dtype=np.float64)
    diff = idx[:, None] - idx[None, :]
    decay = np.where(diff[None] >= 0, np.exp(-(idx + 1.0)[None, None, :] * log_g[:, None, None]), 0.0)
    zeta = np.exp((c - 1 - idx)[None, :] * log_g[:, None])
    xi = np.exp((idx + 1.0)[None, :] * log_g[:, None])
    chunk_decay = np.exp(c * log_g)
    bcast = lambda v: np.broadcast_to(v[:, :, None], (RET_HEADS, c, LANES))
    cd = np.broadcast_to(chunk_decay[:, None, None], (RET_HEADS, 1, LANES))
    return tuple(jnp.asarray(np.ascontiguousarray(t).astype(np.float32))
                 for t in (decay, bcast(zeta), bcast(xi), cd))


def kernel(x, norm_mix_g, w_in, w_out_attn, w_out_ret, w_out, norm_ffn_g, w_ffn_gate, w_ffn_up,
           w_ffn_down, norm_final_g):
    batch, seq, _ = x.shape
    assert w_in.shape[0] == 1, "single-layer problem"
    assert seq % TM_IN == 0 and TM_IN % (DILATIONS[-1] * BF16_SUBLANES) == 0
    x2d = x.reshape(batch * seq, D_MODEL)
    a0, a1, a2, rest = _in_projection(
        x2d, w_in[0], _row_gain(norm_mix_g[0]), _position_tables(seq), batch, seq)
    oa = _dilated_attention(a0, a1, a2, batch, seq)
    ret = _retention(rest, _retention_tables(), batch, seq)
    out = _tail(oa, ret, rest, x2d, w_out_attn[0], w_out_ret[0], w_out[0], w_ffn_gate[0],
                w_ffn_up[0], w_ffn_down[0], _row_gain(norm_ffn_g[0]), norm_final_g[None, :])
    return out.reshape(batch, seq, D_MODEL)
```

```python
import functools

import jax
import jax.numpy as jnp
import numpy as np
from jax import lax
from jax.experimental import pallas as pl
from jax.experimental.pallas import tpu as pltpu

F32 = jnp.float32
BF16 = jnp.bfloat16

LANES = 128
BF16_SUBLANES = 16
V7X_VMEM_LIMIT_BYTES = 60 * 1024 * 1024

D_MODEL = 1024
HEAD_DIM = 64
HEADS_PER_GROUP = 8
DILATIONS = (1, 4, 16)
BAND = 128
GROUP_WIDTH = HEADS_PER_GROUP * HEAD_DIM
ATT_WIDTH = len(DILATIONS) * GROUP_WIDTH
PAIRS = GROUP_WIDTH // LANES
ROPE_THETA = 10000.0

RET_HEADS = 4
RET_KEY_DIM = 128
RET_VALUE_DIM = 256
RET_QK_WIDTH = RET_HEADS * RET_KEY_DIM
RET_V_WIDTH = RET_HEADS * RET_VALUE_DIM
RET_CHUNK = 128

FFN_HIDDEN = 2816
NORM_EPS = 1e-6
MASK_VALUE = -1e30

IN_WIDTH = 3 * ATT_WIDTH + 2 * RET_QK_WIDTH + 2 * RET_V_WIDTH + 2 * D_MODEL
COL_TILE = 512
ATT_TILES = 3 * ATT_WIDTH // COL_TILE
REST_SLABS = (IN_WIDTH - 3 * ATT_WIDTH) // LANES
REST_QR, REST_KR, REST_VR, REST_GR, REST_GA, REST_GRT = 0, 4, 8, 16, 24, 32

TM_IN = 512
TM_OUT = 512
FFN_CHUNK = 256
ATT_BODY_BLOCKS = 32
MERGE_ROWS = 64
MERGE_UNROLL = 8
ATT_SKEW = 2
LOG2_E = 1.4426950408889634
RET_UNROLL = 32
RET_SCAN_UNROLL = 8
TAIL_WEIGHT_GAINED = (False, False, False, True, True, False)


def _rms(x, g):
    ms = jnp.mean(x * x, axis=-1, keepdims=True)
    return x * lax.rsqrt(ms + NORM_EPS) * g


def _rotate(t, cos, sin_signed, span):
    lane = lax.broadcasted_iota(jnp.int32, (1, LANES), 1)
    low = (lane % (2 * span)) < span
    partner = jnp.where(low, pltpu.roll(t, LANES - span, 1), pltpu.roll(t, span, 1))
    return t * cos + partner * sin_signed


W_STAGE_BYTES = 5 << 18
W_STAGE_SLOTS = 4


def _stage_rows(cols):
    return W_STAGE_BYTES // (4 * cols) // BF16_SUBLANES * BF16_SUBLANES


def _load_weights_bf16(jobs, stage, sems):
    slots, stage_rows, _ = stage.shape
    tiles = [(j, r0, min(stage_rows, jobs[j][1].shape[0] - r0))
             for j in range(len(jobs)) for r0 in range(0, jobs[j][1].shape[0], stage_rows)]

    def copy(k):
        j, r0, nr = tiles[k]
        cols = jobs[j][1].shape[1]
        return pltpu.make_async_copy(jobs[j][0].at[pl.ds(r0, nr), :],
                                     stage.at[k % slots, pl.ds(0, nr), pl.ds(0, cols)],
                                     sems.at[k % slots])

    for k in range(min(slots - 1, len(tiles))):
        copy(k).start()
    for k, (j, r0, nr) in enumerate(tiles):
        _, dst, row_gain, scale, scaled_cols = jobs[j]
        if k + slots - 1 < len(tiles):
            copy(k + slots - 1).start()
        copy(k).wait()
        for c in range(0, dst.shape[1], LANES):
            tile = stage[k % slots, 0:nr, c:c + LANES]
            if row_gain is not None:
                tile = tile * row_gain[r0:r0 + nr, :]
            if c < scaled_cols:
                tile = tile * scale
            dst[r0:r0 + nr, c:c + LANES] = tile.astype(BF16)


def _inproj_kernel(x_ref, w_hbm, gain_ref, ca, sa, cr, sr,
                   a0_ref, a1_ref, a2_ref, rest_ref, y_scr, h_scr, tab_scr, rs_scr,
                   w_ref, w_stage, w_sems):
    tm = x_ref.shape[0]
    n_slabs = D_MODEL // LANES

    @pl.when(pl.program_id(0) == 0)
    def _():
        _load_weights_bf16([(w_hbm, w_ref, gain_ref, HEAD_DIM ** -0.5 * LOG2_E, ATT_WIDTH)],
                           w_stage, w_sems)

    x = x_ref[...]
    h_scr[0] = x.astype(BF16)
    rs_scr[0] = jnp.broadcast_to(lax.rsqrt(jnp.mean(x * x, axis=-1, keepdims=True) + NORM_EPS),
                                 (tm, LANES))
    for k in range(n_slabs):
        y_scr[k] = x[:, k * LANES:(k + 1) * LANES]
    for gi, d in enumerate(DILATIONS):
        if d == 1:
            continue
        n = tm // d
        for r in range(d):
            rs_scr[gi, r * n:(r + 1) * n, :] = rs_scr[0, pl.ds(r, n, stride=d), :]
            for k in range(n_slabs):
                h_scr[gi, r * n:(r + 1) * n, k * LANES:(k + 1) * LANES] = (
                    y_scr[k, pl.ds(r, n, stride=d), :].astype(BF16))

    for gi, d in enumerate(DILATIONS):
        if d == 1:
            continue
        n = tm // d
        for r in range(d):
            tab_scr[gi - 1, 0, r * n:(r + 1) * n, :] = ca[pl.ds(r, n, stride=d), :]
            tab_scr[gi - 1, 1, r * n:(r + 1) * n, :] = sa[pl.ds(r, n, stride=d), :]

    def att_tables(g):
        if g == 0:
            return ca[...], sa[...]
        return tab_scr[g - 1, 0], tab_scr[g - 1, 1]

    def write_att(g, slab, val):
        v = val.astype(BF16)
        d = DILATIONS[g]
        n = tm // d
        if g == 0:
            a0_ref[slab] = v
        else:
            out = a1_ref if g == 1 else a2_ref
            for r in range(d):
                out[slab, r] = v[r * n:(r + 1) * n, :]

    def att_tile(c):
        kind, g = divmod(c, 3)
        res = jnp.dot(h_scr[g], w_ref[:, c * COL_TILE:(c + 1) * COL_TILE],
                      preferred_element_type=F32)
        for k in range(PAIRS):
            s = res[:, k * LANES:(k + 1) * LANES] * rs_scr[g]
            if kind < 2:
                s = _rotate(s, *att_tables(g), span=HEAD_DIM // 2)
            write_att(g, kind * PAIRS + k, s)

    def rest_tile(c):
        col = 3 * ATT_WIDTH + c * COL_TILE
        res = jnp.dot(h_scr[0], w_ref[:, col:col + COL_TILE], preferred_element_type=F32)
        for k in range(COL_TILE // LANES):
            slab = c * (COL_TILE // LANES) + k
            s = res[:, k * LANES:(k + 1) * LANES] * rs_scr[0]
            if slab < REST_VR:
                s = _rotate(s, cr[...], sr[...], span=1)
                if slab >= REST_KR:
                    s = s * (RET_KEY_DIM ** -0.5)
            rest_ref[slab] = s.astype(BF16)

    for c in range(REST_SLABS * LANES // COL_TILE):
        rest_tile(c)
    for g in range(len(DILATIONS)):
        for kind in range(3):
            att_tile(kind * 3 + g)


def _in_projection(x2d, w, gain, tabs, batch, seq):
    t = x2d.shape[0]
    tm = TM_IN
    nt = seq // tm
    n_steps = t // tm
    tab_spec = pl.BlockSpec((tm, LANES), lambda i: (i % nt, 0))
    in_specs = [
        pl.BlockSpec((tm, D_MODEL), lambda i: (i, 0)),
        pl.BlockSpec(memory_space=pl.ANY),
        pl.BlockSpec((D_MODEL, LANES), lambda i: (0, 0), pipeline_mode=pl.Buffered(1)),
    ] + [tab_spec] * 4
    n_att = 3 * PAIRS
    d1, d2 = DILATIONS[1], DILATIONS[2]
    out_shape = [
        jax.ShapeDtypeStruct((n_att, t, LANES), BF16),
        jax.ShapeDtypeStruct((n_att, batch, d1, seq // d1, LANES), BF16),
        jax.ShapeDtypeStruct((n_att, batch, d2, seq // d2, LANES), BF16),
        jax.ShapeDtypeStruct((REST_SLABS, t, LANES), BF16),
    ]
    out_specs = [
        pl.BlockSpec((n_att, tm, LANES), lambda i: (0, i, 0)),
        pl.BlockSpec((n_att, None, d1, tm // d1, LANES), lambda i: (0, i // nt, 0, i % nt, 0)),
        pl.BlockSpec((n_att, None, d2, tm // d2, LANES), lambda i: (0, i // nt, 0, i % nt, 0)),
        pl.BlockSpec((REST_SLABS, tm, LANES), lambda i: (0, i, 0)),
    ]
    return pl.pallas_call(
        _inproj_kernel,
        grid=(n_steps,),
        in_specs=in_specs,
        out_specs=out_specs,
        out_shape=out_shape,
        scratch_shapes=[
            pltpu.VMEM((D_MODEL // LANES, tm, LANES), F32),
            pltpu.VMEM((len(DILATIONS), tm, D_MODEL), BF16),
            pltpu.VMEM((len(DILATIONS) - 1, 2, tm, LANES), F32),
            pltpu.VMEM((len(DILATIONS), tm, LANES), F32),
            pltpu.VMEM((D_MODEL, IN_WIDTH), BF16),
            pltpu.VMEM((W_STAGE_SLOTS, _stage_rows(IN_WIDTH), IN_WIDTH), F32),
            pltpu.SemaphoreType.DMA((W_STAGE_SLOTS,)),
        ],
        compiler_params=pltpu.CompilerParams(
            dimension_semantics=("arbitrary",), vmem_limit_bytes=V7X_VMEM_LIMIT_BYTES),
        name="in_projection",
    )(x2d, w, gain, *tabs)


def _pitch(d):
    return d + 1 if d % 8 == 0 else d


def _attn_scores(q, kk, bias, first_head):
    zero = jnp.zeros_like(q)
    q2 = jnp.concatenate([jnp.where(first_head, q, zero), jnp.where(first_head, zero, q)], axis=0)
    return lax.dot_general(q2, kk, (((1,), (1,)), ((), ())), preferred_element_type=F32) + bias


def _attn_partials(s, vv, first_head):
    nq = s.shape[0] // 2
    m = jnp.max(s, axis=1, keepdims=True)
    p = jnp.exp2(s - m)
    l = jnp.sum(p, axis=1, keepdims=True)
    pv = jnp.dot(p.astype(BF16), vv, preferred_element_type=F32)
    acc = jnp.where(first_head, pv[:nq], pv[nq:])
    den = jnp.where(first_head, l[:nq], l[nq:])
    mm = jnp.where(first_head, m[:nq], m[nq:])
    return acc, den, mm


def _attn_kernel(q0, k0, v0, q1, k1, v1, q2, k2, v2, *refs):
    n_w = len(TAIL_WEIGHT_GAINED)
    w_src, gain_ref, o_ref = refs[:n_w], refs[n_w], refs[n_w + 1]
    w_dst = refs[n_w + 2:2 * n_w + 2]
    acc0, den0, max0, acc1, den1, max1, acc2, den2, max2 = refs[2 * n_w + 2:]

    for src, dst, gained in zip(w_src, w_dst, TAIL_WEIGHT_GAINED):
        for c in range(0, src.shape[1], LANES):
            tile = src[:, c:c + LANES]
            if gained:
                tile = tile * gain_ref[...]
            dst[:, c:c + LANES] = tile.astype(BF16)

    blk = BAND
    seq = q0.shape[0]
    lane = lax.broadcasted_iota(jnp.int32, (1, LANES), 1)
    first_head = lane < HEAD_DIM
    qi = lax.broadcasted_iota(jnp.int32, (2 * blk, 2 * blk), 0) & (blk - 1)
    kj = lax.broadcasted_iota(jnp.int32, (2 * blk, 2 * blk), 1)
    band_bias = jnp.where((kj >= qi) & (kj <= qi + BAND), 0.0, MASK_VALUE).astype(F32)
    qi1 = lax.broadcasted_iota(jnp.int32, (2 * blk, blk), 0) & (blk - 1)
    kj1 = lax.broadcasted_iota(jnp.int32, (2 * blk, blk), 1)
    first_bias = jnp.where(kj1 <= qi1, 0.0, MASK_VALUE).astype(F32)

    def key_rows(n):
        if isinstance(n, int) and n == 0:
            return pl.ds(0, blk), first_bias
        st = n * blk if isinstance(n, int) else pl.multiple_of(n * blk, blk)
        return pl.ds(st - blk, 2 * blk), band_bias

    def query_rows(n):
        return pl.ds(n * blk if isinstance(n, int) else pl.multiple_of(n * blk, blk), blk)

    def run_blocks(blocks, emit):
        scores = []
        for i in range(len(blocks) + ATT_SKEW):
            if i < len(blocks):
                qr, kr, _, n = blocks[i]
                rows, bias = key_rows(n)
                scores.append(_attn_scores(qr[query_rows(n), :], kr[rows, :], bias, first_head))
            if i >= ATT_SKEW:
                j = i - ATT_SKEW
                _, _, vr, n = blocks[j]
                rows, _ = key_rows(n)
                emit(j, _attn_partials(scores[j], vr[rows, :], first_head))
                scores[j] = None

    def dilated_group(qr, kr, vr, outs, d):
        nb = seq // d // blk
        subs = max(1, ATT_BODY_BLOCKS // nb)

        def body(j, carry):
            blocks, where = [], []
            for ri in range(subs):
                r = j * subs + ri
                for n in range(nb):
                    blocks.append((qr.at[r], kr.at[r], vr.at[r], n))
                    where.append(pl.ds(n * (blk * _pitch(d)) + r, blk, stride=_pitch(d)))

            def emit(i, parts):
                for ref, val in zip(outs, parts):
                    ref[where[i], :] = val

            run_blocks(blocks, emit)
            return carry

        lax.fori_loop(0, d // subs, body, 0)

    dilated_group(q1, k1, v1, (acc1, den1, max1), DILATIONS[1])
    dilated_group(q2, k2, v2, (acc2, den2, max2), DILATIONS[2])

    def dense_body(j, carry):
        ns = [j * ATT_BODY_BLOCKS + i for i in range(ATT_BODY_BLOCKS)]

        def emit(i, parts):
            for ref, val in zip((acc0, den0, max0), parts):
                ref[query_rows(ns[i]), :] = val

        run_blocks([(q0, k0, v0, n) for n in ns], emit)
        return carry

    dense_body(0, 0)
    lax.fori_loop(1, seq // blk // ATT_BODY_BLOCKS, dense_body, 0)

    def merge(t, carry):
        rows = pl.ds(pl.multiple_of(t * MERGE_ROWS, MERGE_ROWS), MERGE_ROWS)

        def padded(ref, d):
            if _pitch(d) == d:
                return ref[rows, :]
            per = MERGE_ROWS // d
            return jnp.concatenate(
                [ref[pl.ds((t * per + i) * _pitch(d), d), :] for i in range(per)], axis=0)

        d2 = DILATIONS[2]
        m0, m1, m2 = max0[rows, :], max1[rows, :], padded(max2, d2)
        mx = jnp.maximum(m0, jnp.maximum(m1, m2))
        e0, e1, e2 = jnp.exp2(m0 - mx), jnp.exp2(m1 - mx), jnp.exp2(m2 - mx)
        num = e0 * acc0[rows, :] + e1 * acc1[rows, :] + e2 * padded(acc2, d2)
        den = e0 * den0[rows, :] + e1 * den1[rows, :] + e2 * padded(den2, d2)
        o_ref[rows, :] = (num * (1.0 / den)).astype(o_ref.dtype)
        return carry

    lax.fori_loop(0, seq // MERGE_ROWS, merge, 0, unroll=MERGE_UNROLL)


def _dilated_attention(a0, a1, a2, weights, gain, batch, seq):
    d1, d2 = DILATIONS[1], DILATIONS[2]
    n_steps = batch * PAIRS
    step_rows = lambda w: pl.BlockSpec((w.shape[0] // n_steps, w.shape[1]), lambda b, p: (b * PAIRS + p, 0))
    for w in weights:
        assert w.shape[0] % (n_steps * BF16_SUBLANES) == 0 and w.shape[1] % LANES == 0, w.shape
    assert gain.shape[0] == weights[3].shape[0] == weights[4].shape[0]
    specs0 = [pl.BlockSpec((None, seq, LANES), functools.partial(lambda b, p, k: (k * PAIRS + p, b, 0), k=k))
              for k in range(3)]
    specs1 = [pl.BlockSpec((None, None, d1, seq // d1, LANES),
                           functools.partial(lambda b, p, k: (k * PAIRS + p, b, 0, 0, 0), k=k))
              for k in range(3)]
    specs2 = [pl.BlockSpec((None, None, d2, seq // d2, LANES),
                           functools.partial(lambda b, p, k: (k * PAIRS + p, b, 0, 0, 0), k=k))
              for k in range(3)]
    in_specs = specs0 + specs1 + specs2 + [step_rows(w) for w in weights] + [step_rows(gain)]
    outs = pl.pallas_call(
        _attn_kernel,
        grid=(batch, PAIRS),
        in_specs=in_specs,
        out_specs=[pl.BlockSpec((None, seq, LANES), lambda b, p: (p, b, 0))]
        + [step_rows(w) for w in weights],
        out_shape=[jax.ShapeDtypeStruct((PAIRS, batch * seq, LANES), BF16)]
        + [jax.ShapeDtypeStruct(w.shape, BF16) for w in weights],
        scratch_shapes=[pltpu.VMEM((seq // d * _pitch(d), LANES), F32) for d in DILATIONS for _ in range(3)],
        compiler_params=pltpu.CompilerParams(
            dimension_semantics=("arbitrary", "arbitrary"), vmem_limit_bytes=V7X_VMEM_LIMIT_BYTES),
        name="dilated_attention",
    )(a0, a0, a0, a1, a1, a1, a2, a2, a2, *weights, gain)
    return outs[0], outs[1:]


def _retention_kernel(q_ref, k_ref, v_ref, dec_ref, zeta_ref, xi_ref, cd_ref, o_ref, kv_scr, state_scr):
    c_len = RET_CHUNK
    n_chunks = q_ref.shape[0] // c_len
    dec = dec_ref[...]
    zeta = zeta_ref[...]
    xi = xi_ref[...]
    xi2 = jnp.concatenate([xi, xi], axis=1)
    cd = cd_ref[...]
    cd2 = jnp.concatenate([cd, cd], axis=1)

    def chunk_rows(c):
        return pl.ds(pl.multiple_of(c * c_len, c_len), c_len)

    def values(rows):
        return jnp.concatenate([v_ref[0, rows, :], v_ref[1, rows, :]], axis=1)

    def chunk_kv(c, carry):
        rows = chunk_rows(c)
        kz = (k_ref[rows, :].astype(F32) * zeta).astype(BF16)
        kv_scr[c] = lax.dot_general(kz, values(rows), (((0,), (0,)), ((), ())),
                                    preferred_element_type=F32)
        return carry

    lax.fori_loop(0, n_chunks, chunk_kv, 0, unroll=RET_UNROLL)

    def chunk_state(c, state):
        state_scr[c] = state.astype(BF16)
        return state * cd2 + kv_scr[c]

    lax.fori_loop(0, n_chunks, chunk_state, jnp.zeros((RET_KEY_DIM, RET_VALUE_DIM), F32),
                  unroll=RET_SCAN_UNROLL)

    def chunk_group_out(j, carry):
        chunks = [j * RET_UNROLL + i for i in range(RET_UNROLL)]
        scores = []
        for c in chunks:
            rows = chunk_rows(c)
            a = lax.dot_general(q_ref[rows, :], k_ref[rows, :], (((1,), (1,)), ((), ())),
                                preferred_element_type=F32) * dec
            scores.append(a.astype(BF16))
        for c, a in zip(chunks, scores):
            rows = chunk_rows(c)
            lhs = jnp.concatenate([a, q_ref[rows, :]], axis=1)
            rhs = jnp.concatenate([values(rows), state_scr[c]], axis=0)
            o = jnp.dot(lhs, rhs, preferred_element_type=F32) * xi2
            o_ref[0, rows, :] = o[:, :LANES].astype(o_ref.dtype)
            o_ref[1, rows, :] = o[:, LANES:].astype(o_ref.dtype)
        return carry

    lax.fori_loop(0, n_chunks // RET_UNROLL, chunk_group_out, 0)


def _retention(rest, tabs, batch, seq):
    dec, zeta, xi, cd = tabs
    vs = RET_VALUE_DIM // LANES
    tab_spec = pl.BlockSpec((None, RET_CHUNK, LANES), lambda b, h: (h, 0, 0))
    in_specs = [
        pl.BlockSpec((None, seq, LANES), lambda b, h: (REST_QR + h, b, 0)),
        pl.BlockSpec((None, seq, LANES), lambda b, h: (REST_KR + h, b, 0)),
        pl.BlockSpec((vs, seq, LANES), lambda b, h: (REST_VR // vs + h, b, 0)),
        tab_spec, tab_spec, tab_spec,
        pl.BlockSpec((None, 1, LANES), lambda b, h: (h, 0, 0)),
    ]
    return pl.pallas_call(
        _retention_kernel,
        grid=(batch, RET_HEADS),
        in_specs=in_specs,
        out_specs=pl.BlockSpec((vs, seq, LANES), lambda b, h: (h, b, 0)),
        out_shape=jax.ShapeDtypeStruct((RET_V_WIDTH // LANES, batch * seq, LANES), BF16),
        scratch_shapes=[pltpu.VMEM((seq // RET_CHUNK, RET_KEY_DIM, RET_VALUE_DIM), F32),
                        pltpu.VMEM((seq // RET_CHUNK, RET_KEY_DIM, RET_VALUE_DIM), BF16)],
        compiler_params=pltpu.CompilerParams(
            dimension_semantics=("arbitrary", "arbitrary"), vmem_limit_bytes=V7X_VMEM_LIMIT_BYTES),
        name="retention",
    )(rest, rest, rest, dec, zeta, xi, cd)


def _slabs(ref, lo=0, hi=None):
    hi = ref.shape[0] if hi is None else hi
    return jnp.concatenate([ref[k] for k in range(lo, hi)], axis=1)


def _tail_kernel(oa_ref, ret_ref, sg_ref, ga_ref, gr_ref, x_ref, wa_ref, wr_ref, wo_ref,
                 wg_ref, wu_ref, wd_ref, gf_ref, out_ref, ret_scr, act_scr):
    vs = RET_VALUE_DIM // LANES
    for h in range(RET_HEADS):
        o = _slabs(ret_ref, h * vs, (h + 1) * vs).astype(F32)
        oc = o - jnp.mean(o, axis=-1, keepdims=True)
        var = jnp.mean(oc * oc, axis=-1, keepdims=True)
        gate = _slabs(sg_ref, h * vs, (h + 1) * vs).astype(F32)
        gated = oc * lax.rsqrt(var + NORM_EPS) * (gate * jax.nn.sigmoid(gate))
        ret_scr[:, h * RET_VALUE_DIM:(h + 1) * RET_VALUE_DIM] = gated.astype(BF16)
    ya = jnp.dot(_slabs(oa_ref), wa_ref[...], preferred_element_type=F32)
    yr = jnp.dot(ret_scr[...], wr_ref[...], preferred_element_type=F32)
    merged = (jax.nn.sigmoid(_slabs(ga_ref).astype(F32)) * ya
              + jax.nn.sigmoid(_slabs(gr_ref).astype(F32)) * yr)
    x1 = x_ref[...] + jnp.dot(merged.astype(BF16), wo_ref[...], preferred_element_type=F32)
    scale = jnp.broadcast_to(lax.rsqrt(jnp.mean(x1 * x1, axis=-1, keepdims=True) + NORM_EPS),
                             (x1.shape[0], FFN_CHUNK))
    h2 = x1.astype(BF16)
    for c in range(0, FFN_HIDDEN, FFN_CHUNK):
        gate = jnp.dot(h2, wg_ref[:, c:c + FFN_CHUNK], preferred_element_type=F32) * scale
        up = jnp.dot(h2, wu_ref[:, c:c + FFN_CHUNK], preferred_element_type=F32) * scale
        act_scr[:, c:c + FFN_CHUNK] = (gate * jax.nn.sigmoid(gate) * up).astype(BF16)
    down = jnp.dot(act_scr[...], wd_ref[...], preferred_element_type=F32)
    out_ref[...] = _rms(x1 + down, gf_ref[...])


def _tail(oa, ret, rest, x2d, wa, wr, wo, wg, wu, wd, gf):
    t = x2d.shape[0]
    tm = TM_OUT
    gs = D_MODEL // LANES
    full = lambda shape: pl.BlockSpec(shape, lambda i: (0,) * len(shape), pipeline_mode=pl.Buffered(1))
    vec = pl.BlockSpec((1, D_MODEL), lambda i: (0, 0))
    row = pl.BlockSpec((tm, D_MODEL), lambda i: (i, 0))
    in_specs = [
        pl.BlockSpec((PAIRS, tm, LANES), lambda i: (0, i, 0)),
        pl.BlockSpec((gs, tm, LANES), lambda i: (0, i, 0)),
        pl.BlockSpec((gs, tm, LANES), lambda i: (REST_GR // gs, i, 0)),
        pl.BlockSpec((gs, tm, LANES), lambda i: (REST_GA // gs, i, 0)),
        pl.BlockSpec((gs, tm, LANES), lambda i: (REST_GRT // gs, i, 0)),
        row,
        full((GROUP_WIDTH, D_MODEL)), full((RET_V_WIDTH, D_MODEL)), full((D_MODEL, D_MODEL)),
        full((D_MODEL, FFN_HIDDEN)), full((D_MODEL, FFN_HIDDEN)), full((FFN_HIDDEN, D_MODEL)), vec,
    ]
    return pl.pallas_call(
        _tail_kernel,
        grid=(t // tm,),
        in_specs=in_specs,
        out_specs=row,
        out_shape=jax.ShapeDtypeStruct((t, D_MODEL), F32),
        scratch_shapes=[pltpu.VMEM((tm, RET_V_WIDTH), BF16), pltpu.VMEM((tm, FFN_HIDDEN), BF16)],
        compiler_params=pltpu.CompilerParams(
            dimension_semantics=("arbitrary",), vmem_limit_bytes=V7X_VMEM_LIMIT_BYTES),
        name="merge_ffn",
    )(oa, ret, rest, rest, rest, x2d, wa, wr, wo, wg, wu, wd, gf)


def _row_gain(g):
    return jnp.broadcast_to(g[:, None], (g.shape[0], LANES))


def _position_tables(seq):
    pos = np.arange(seq, dtype=np.float64)
    inv = ROPE_THETA ** (-np.arange(0, HEAD_DIM, 2, dtype=np.float64) / HEAD_DIM)
    ang = pos[:, None] * inv[None, :]
    c, s = np.cos(ang), np.sin(ang)
    cos_a = np.concatenate([c, c, c, c], axis=1)
    sin_a = np.concatenate([-s, s, -s, s], axis=1)
    base = 1.0 / (ROPE_THETA ** np.linspace(0.0, 1.0, RET_KEY_DIM // 2, dtype=np.float64))
    ang_r = pos[:, None] * base[None, :]
    cr, sr = np.cos(ang_r), np.sin(ang_r)
    cos_r = np.repeat(cr, 2, axis=1)
    sin_r = np.repeat(sr, 2, axis=1) * np.tile([-1.0, 1.0], RET_KEY_DIM // 2)[None, :]
    return [jnp.asarray(t.astype(np.float32)) for t in (cos_a, sin_a, cos_r, sin_r)]


def _retention_tables():
    c = RET_CHUNK
    log_g = np.log1p(-(2.0 ** (-5.0 - np.arange(RET_HEADS, dtype=np.float64))))
    idx = np.arange(c, dtype=np.float64)
    diff = idx[:, None] - idx[None, :]
    decay = np.where(diff[None] >= 0, np.exp(-(idx + 1.0)[None, None, :] * log_g[:, None, None]), 0.0)
    zeta = np.exp((c - 1 - idx)[None, :] * log_g[:, None])
    xi = np.exp((idx + 1.0)[None, :] * log_g[:, None])
    chunk_decay = np.exp(c * log_g)
    bcast = lambda v: np.broadcast_to(v[:, :, None], (RET_HEADS, c, LANES))
    cd = np.broadcast_to(chunk_decay[:, None, None], (RET_HEADS, 1, LANES))
    return tuple(jnp.asarray(np.ascontiguousarray(t).astype(np.float32))
                 for t in (decay, bcast(zeta), bcast(xi), cd))


def kernel(x, norm_mix_g, w_in, w_out_attn, w_out_ret, w_out, norm_ffn_g, w_ffn_gate, w_ffn_up,
           w_ffn_down, norm_final_g):
    batch, seq, _ = x.shape
    assert w_in.shape[0] == 1, "single-layer problem"
    assert seq % TM_IN == 0 and TM_IN % (DILATIONS[-1] * BF16_SUBLANES) == 0
    x2d = x.reshape(batch * seq, D_MODEL)
    a0, a1, a2, rest = _in_projection(
        x2d, w_in[0], _row_gain(norm_mix_g[0]), _position_tables(seq), batch, seq)
    tail_weights = [w_out_attn[0], w_out_ret[0], w_out[0], w_ffn_gate[0], w_ffn_up[0],
                    w_ffn_down[0].reshape(D_MODEL, FFN_HIDDEN)]
    oa, (wa, wr, wo, wg, wu, wd) = _dilated_attention(
        a0, a1, a2, tail_weights, _row_gain(norm_ffn_g[0]), batch, seq)
    ret = _retention(rest, _retention_tables(), batch, seq)
    out = _tail(oa, ret, rest, x2d, wa, wr, wo, wg, wu, wd.reshape(FFN_HIDDEN, D_MODEL),
                norm_final_g[None, :])
    return out.reshape(batch, seq, D_MODEL)
```

```python
import functools

import jax
import jax.numpy as jnp
import numpy as np
from jax import lax
from jax.experimental import pallas as pl
from jax.experimental.pallas import tpu as pltpu

F32 = jnp.float32
BF16 = jnp.bfloat16

LANES = 128
BF16_SUBLANES = 16
V7X_VMEM_LIMIT_BYTES = 60 * 1024 * 1024

D_MODEL = 1024
HEAD_DIM = 64
HEADS_PER_GROUP = 8
DILATIONS = (1, 4, 16)
BAND = 128
GROUP_WIDTH = HEADS_PER_GROUP * HEAD_DIM
ATT_WIDTH = len(DILATIONS) * GROUP_WIDTH
PAIRS = GROUP_WIDTH // LANES
ROPE_THETA = 10000.0

RET_HEADS = 4
RET_KEY_DIM = 128
RET_VALUE_DIM = 256
RET_QK_WIDTH = RET_HEADS * RET_KEY_DIM
RET_V_WIDTH = RET_HEADS * RET_VALUE_DIM
RET_CHUNK = 128

FFN_HIDDEN = 2816
NORM_EPS = 1e-6
MASK_VALUE = -1e30

IN_WIDTH = 3 * ATT_WIDTH + 2 * RET_QK_WIDTH + 2 * RET_V_WIDTH + 2 * D_MODEL
COL_TILE = 512
ATT_TILES = 3 * ATT_WIDTH // COL_TILE
REST_SLABS = (IN_WIDTH - 3 * ATT_WIDTH) // LANES
REST_QR, REST_KR, REST_VR, REST_GR, REST_GA, REST_GRT = 0, 4, 8, 16, 24, 32

TM_IN = 512
TM_OUT = 512
FFN_CHUNK = 256
ATT_BODY_BLOCKS = 32
MERGE_ROWS = 64
MERGE_UNROLL = 8
ATT_SKEW = 2
LOG2_E = 1.4426950408889634
RET_UNROLL = 32
RET_SCAN_UNROLL = 8
TAIL_WEIGHT_GAINED = (False, False, False, True, True, False)


def _rms(x, g):
    ms = jnp.mean(x * x, axis=-1, keepdims=True)
    return x * lax.rsqrt(ms + NORM_EPS) * g


def _rotate(t, cos, sin_signed, span):
    lane = lax.broadcasted_iota(jnp.int32, (1, LANES), 1)
    low = (lane % (2 * span)) < span
    partner = jnp.where(low, pltpu.roll(t, LANES - span, 1), pltpu.roll(t, span, 1))
    return t * cos + partner * sin_signed


W_STAGE_BYTES = 5 << 18
W_STAGE_SLOTS = 4


def _stage_rows(cols):
    return W_STAGE_BYTES // (4 * cols) // BF16_SUBLANES * BF16_SUBLANES


def _load_weights_bf16(jobs, stage, sems):
    slots, stage_rows, _ = stage.shape
    tiles = [(j, r0, min(stage_rows, jobs[j][1].shape[0] - r0))
             for j in range(len(jobs)) for r0 in range(0, jobs[j][1].shape[0], stage_rows)]

    def copy(k):
        j, r0, nr = tiles[k]
        cols = jobs[j][1].shape[1]
        return pltpu.make_async_copy(jobs[j][0].at[pl.ds(r0, nr), :],
                                     stage.at[k % slots, pl.ds(0, nr), pl.ds(0, cols)],
                                     sems.at[k % slots])

    for k in range(min(slots - 1, len(tiles))):
        copy(k).start()
    for k, (j, r0, nr) in enumerate(tiles):
        _, dst, row_gain, scale, scaled_cols = jobs[j]
        if k + slots - 1 < len(tiles):
            copy(k + slots - 1).start()
        copy(k).wait()
        for c in range(0, dst.shape[1], LANES):
            tile = stage[k % slots, 0:nr, c:c + LANES]
            if row_gain is not None:
                tile = tile * row_gain[r0:r0 + nr, :]
            if c < scaled_cols:
                tile = tile * scale
            dst[r0:r0 + nr, c:c + LANES] = tile.astype(BF16)


def _inproj_kernel(x_ref, w_hbm, gain_ref, ca, sa, cr, sr,
                   a0_ref, a1_ref, a2_ref, rest_ref, y_scr, h_scr, tab_scr, rs_scr,
                   w_ref, w_stage, w_sems):
    tm = x_ref.shape[0]
    n_slabs = D_MODEL // LANES

    @pl.when(pl.program_id(0) == 0)
    def _():
        _load_weights_bf16([(w_hbm, w_ref, gain_ref, HEAD_DIM ** -0.5 * LOG2_E, ATT_WIDTH)],
                           w_stage, w_sems)

    x = x_ref[...]
    h_scr[0] = x.astype(BF16)
    rs_scr[0] = jnp.broadcast_to(lax.rsqrt(jnp.mean(x * x, axis=-1, keepdims=True) + NORM_EPS),
                                 (tm, LANES))
    for k in range(n_slabs):
        y_scr[k] = x[:, k * LANES:(k + 1) * LANES]
    for gi, d in enumerate(DILATIONS):
        if d == 1:
            continue
        n = tm // d
        for r in range(d):
            rs_scr[gi, r * n:(r + 1) * n, :] = rs_scr[0, pl.ds(r, n, stride=d), :]
            for k in range(n_slabs):
                h_scr[gi, r * n:(r + 1) * n, k * LANES:(k + 1) * LANES] = (
                    y_scr[k, pl.ds(r, n, stride=d), :].astype(BF16))

    for gi, d in enumerate(DILATIONS):
        if d == 1:
            continue
        n = tm // d
        for r in range(d):
            tab_scr[gi - 1, 0, r * n:(r + 1) * n, :] = ca[pl.ds(r, n, stride=d), :]
            tab_scr[gi - 1, 1, r * n:(r + 1) * n, :] = sa[pl.ds(r, n, stride=d), :]

    def att_tables(g):
        if g == 0:
            return ca[...], sa[...]
        return tab_scr[g - 1, 0], tab_scr[g - 1, 1]

    def write_att(g, slab, val):
        v = val.astype(BF16)
        d = DILATIONS[g]
        n = tm // d
        if g == 0:
            a0_ref[slab] = v
        else:
            out = a1_ref if g == 1 else a2_ref
            for r in range(d):
                out[slab, r] = v[r * n:(r + 1) * n, :]

    def att_tile(c):
        kind, g = divmod(c, 3)
        res = jnp.dot(h_scr[g], w_ref[:, c * COL_TILE:(c + 1) * COL_TILE],
                      preferred_element_type=F32)
        for k in range(PAIRS):
            s = res[:, k * LANES:(k + 1) * LANES] * rs_scr[g]
            if kind < 2:
                s = _rotate(s, *att_tables(g), span=HEAD_DIM // 2)
            write_att(g, kind * PAIRS + k, s)

    def rest_tile(c):
        col = 3 * ATT_WIDTH + c * COL_TILE
        res = jnp.dot(h_scr[0], w_ref[:, col:col + COL_TILE], preferred_element_type=F32)
        for k in range(COL_TILE // LANES):
            slab = c * (COL_TILE // LANES) + k
            s = res[:, k * LANES:(k + 1) * LANES] * rs_scr[0]
            if slab < REST_VR:
                s = _rotate(s, cr[...], sr[...], span=1)
                if slab >= REST_KR:
                    s = s * (RET_KEY_DIM ** -0.5)
            rest_ref[slab] = s.astype(BF16)

    for c in range(REST_SLABS * LANES // COL_TILE):
        rest_tile(c)
    for g in range(len(DILATIONS)):
        for kind in range(3):
            att_tile(kind * 3 + g)


def _in_projection(x2d, w, gain, tabs, batch, seq):
    t = x2d.shape[0]
    tm = TM_IN
    nt = seq // tm
    n_steps = t // tm
    tab_spec = pl.BlockSpec((tm, LANES), lambda i: (i % nt, 0))
    in_specs = [
        pl.BlockSpec((tm, D_MODEL), lambda i: (i, 0)),
        pl.BlockSpec(memory_space=pl.ANY),
        pl.BlockSpec((D_MODEL, LANES), lambda i: (0, 0), pipeline_mode=pl.Buffered(1)),
    ] + [tab_spec] * 4
    n_att = 3 * PAIRS
    d1, d2 = DILATIONS[1], DILATIONS[2]
    out_shape = [
        jax.ShapeDtypeStruct((n_att, t, LANES), BF16),
        jax.ShapeDtypeStruct((n_att, batch, d1, seq // d1, LANES), BF16),
        jax.ShapeDtypeStruct((n_att, batch, d2, seq // d2, LANES), BF16),
        jax.ShapeDtypeStruct((REST_SLABS, t, LANES), BF16),
    ]
    out_specs = [
        pl.BlockSpec((n_att, tm, LANES), lambda i: (0, i, 0)),
        pl.BlockSpec((n_att, None, d1, tm // d1, LANES), lambda i: (0, i // nt, 0, i % nt, 0)),
        pl.BlockSpec((n_att, None, d2, tm // d2, LANES), lambda i: (0, i // nt, 0, i % nt, 0)),
        pl.BlockSpec((REST_SLABS, tm, LANES), lambda i: (0, i, 0)),
    ]
    return pl.pallas_call(
        _inproj_kernel,
        grid=(n_steps,),
        in_specs=in_specs,
        out_specs=out_specs,
        out_shape=out_shape,
        scratch_shapes=[
            pltpu.VMEM((D_MODEL // LANES, tm, LANES), F32),
            pltpu.VMEM((len(DILATIONS), tm, D_MODEL), BF16),
            pltpu.VMEM((len(DILATIONS) - 1, 2, tm, LANES), F32),
            pltpu.VMEM((len(DILATIONS), tm, LANES), F32),
            pltpu.VMEM((D_MODEL, IN_WIDTH), BF16),
            pltpu.VMEM((W_STAGE_SLOTS, _stage_rows(IN_WIDTH), IN_WIDTH), F32),
            pltpu.SemaphoreType.DMA((W_STAGE_SLOTS,)),
        ],
        compiler_params=pltpu.CompilerParams(
            dimension_semantics=("arbitrary",), vmem_limit_bytes=V7X_VMEM_LIMIT_BYTES),
        name="in_projection",
    )(x2d, w, gain, *tabs)


def _pitch(d):
    return d + 1 if d % 8 == 0 else d


def _attn_scores(q, kk, bias, first_head):
    zero = jnp.zeros_like(q)
    q2 = jnp.concatenate([jnp.where(first_head, q, zero), jnp.where(first_head, zero, q)], axis=0)
    return lax.dot_general(q2, kk, (((1,), (1,)), ((), ())), preferred_element_type=F32) + bias


def _attn_partials(s, vv, first_head):
    nq = s.shape[0] // 2
    m = jnp.max(s, axis=1, keepdims=True)
    p = jnp.exp2(s - m)
    l = jnp.sum(p, axis=1, keepdims=True)
    pv = jnp.dot(p.astype(BF16), vv, preferred_element_type=F32)
    acc = jnp.where(first_head, pv[:nq], pv[nq:])
    den = jnp.where(first_head, l[:nq], l[nq:])
    mm = jnp.where(first_head, m[:nq], m[nq:])
    return acc, den, mm


def _attn_kernel(q0, k0, v0, q1, k1, v1, q2, k2, v2, *refs):
    n_w = len(TAIL_WEIGHT_GAINED)
    w_src, gain_ref, o_ref = refs[:n_w], refs[n_w], refs[n_w + 1]
    w_dst = refs[n_w + 2:2 * n_w + 2]
    acc0, den0, max0, acc1, den1, max1, acc2, den2, max2 = refs[2 * n_w + 2:]

    for src, dst, gained in zip(w_src, w_dst, TAIL_WEIGHT_GAINED):
        for c in range(0, src.shape[1], LANES):
            tile = src[:, c:c + LANES]
            if gained:
                tile = tile * gain_ref[...]
            dst[:, c:c + LANES] = tile.astype(BF16)

    blk = BAND
    seq = q0.shape[0]
    lane = lax.broadcasted_iota(jnp.int32, (1, LANES), 1)
    first_head = lane < HEAD_DIM
    qi = lax.broadcasted_iota(jnp.int32, (2 * blk, 2 * blk), 0) & (blk - 1)
    kj = lax.broadcasted_iota(jnp.int32, (2 * blk, 2 * blk), 1)
    band_bias = jnp.where((kj >= qi) & (kj <= qi + BAND), 0.0, MASK_VALUE).astype(F32)
    qi1 = lax.broadcasted_iota(jnp.int32, (2 * blk, blk), 0) & (blk - 1)
    kj1 = lax.broadcasted_iota(jnp.int32, (2 * blk, blk), 1)
    first_bias = jnp.where(kj1 <= qi1, 0.0, MASK_VALUE).astype(F32)

    def key_rows(n):
        if isinstance(n, int) and n == 0:
            return pl.ds(0, blk), first_bias
        st = n * blk if isinstance(n, int) else pl.multiple_of(n * blk, blk)
        return pl.ds(st - blk, 2 * blk), band_bias

    def query_rows(n):
        return pl.ds(n * blk if isinstance(n, int) else pl.multiple_of(n * blk, blk), blk)

    def run_blocks(blocks, emit):
        scores = []
        for i in range(len(blocks) + ATT_SKEW):
            if i < len(blocks):
                qr, kr, _, n = blocks[i]
                rows, bias = key_rows(n)
                scores.append(_attn_scores(qr[query_rows(n), :], kr[rows, :], bias, first_head))
            if i >= ATT_SKEW:
                j = i - ATT_SKEW
                _, _, vr, n = blocks[j]
                rows, _ = key_rows(n)
                emit(j, _attn_partials(scores[j], vr[rows, :], first_head))
                scores[j] = None

    def dilated_group(qr, kr, vr, outs, d):
        nb = seq // d // blk
        subs = max(1, ATT_BODY_BLOCKS // nb)

        def body(j, carry):
            blocks, where = [], []
            for ri in range(subs):
                r = j * subs + ri
                for n in range(nb):
                    blocks.append((qr.at[r], kr.at[r], vr.at[r], n))
                    where.append(pl.ds(n * (blk * _pitch(d)) + r, blk, stride=_pitch(d)))

            def emit(i, parts):
                for ref, val in zip(outs, parts):
                    ref[where[i], :] = val

            run_blocks(blocks, emit)
            return carry

        lax.fori_loop(0, d // subs, body, 0)

    dilated_group(q1, k1, v1, (acc1, den1, max1), DILATIONS[1])
    dilated_group(q2, k2, v2, (acc2, den2, max2), DILATIONS[2])

    def dense_body(j, carry):
        ns = [j * ATT_BODY_BLOCKS + i for i in range(ATT_BODY_BLOCKS)]

        def emit(i, parts):
            for ref, val in zip((acc0, den0, max0), parts):
                ref[query_rows(ns[i]), :] = val

        run_blocks([(q0, k0, v0, n) for n in ns], emit)
        return carry

    dense_body(0, 0)
    lax.fori_loop(1, seq // blk // ATT_BODY_BLOCKS, dense_body, 0)

    def merge(t, carry):
        rows = pl.ds(pl.multiple_of(t * MERGE_ROWS, MERGE_ROWS), MERGE_ROWS)

        def padded(ref, d):
            if _pitch(d) == d:
                return ref[rows, :]
            per = MERGE_ROWS // d
            return jnp.concatenate(
                [ref[pl.ds((t * per + i) * _pitch(d), d), :] for i in range(per)], axis=0)

        d2 = DILATIONS[2]
        m0, m1, m2 = max0[rows, :], max1[rows, :], padded(max2, d2)
        mx = jnp.maximum(m0, jnp.maximum(m1, m2))
        e0, e1, e2 = jnp.exp2(m0 - mx), jnp.exp2(m1 - mx), jnp.exp2(m2 - mx)
        num = e0 * acc0[rows, :] + e1 * acc1[rows, :] + e2 * padded(acc2, d2)
        den = e0 * den0[rows, :] + e1 * den1[rows, :] + e2 * padded(den2, d2)
        o_ref[rows, :] = (num * (1.0 / den)).astype(o_ref.dtype)
        return carry

    lax.fori_loop(0, seq // MERGE_ROWS, merge, 0, unroll=MERGE_UNROLL)


def _dilated_attention(a0, a1, a2, weights, gain, batch, seq):
    d1, d2 = DILATIONS[1], DILATIONS[2]
    n_steps = batch * PAIRS

    def step_rows(w):
        rows, cols = w.shape
        n_blocks = max(n for n in range(1, n_steps + 1) if rows % (n * BF16_SUBLANES) == 0)
        return pl.BlockSpec((rows // n_blocks, cols),
                            lambda b, p: (jnp.minimum(b * PAIRS + p, n_blocks - 1), 0))

    for w in weights:
        assert w.shape[0] % BF16_SUBLANES == 0 and w.shape[1] % LANES == 0, w.shape
    assert gain.shape[0] == weights[3].shape[0] == weights[4].shape[0]
    specs0 = [pl.BlockSpec((None, seq, LANES), functools.partial(lambda b, p, k: (k * PAIRS + p, b, 0), k=k))
              for k in range(3)]
    specs1 = [pl.BlockSpec((None, None, d1, seq // d1, LANES),
                           functools.partial(lambda b, p, k: (k * PAIRS + p, b, 0, 0, 0), k=k))
              for k in range(3)]
    specs2 = [pl.BlockSpec((None, None, d2, seq // d2, LANES),
                           functools.partial(lambda b, p, k: (k * PAIRS + p, b, 0, 0, 0), k=k))
              for k in range(3)]
    in_specs = specs0 + specs1 + specs2 + [step_rows(w) for w in weights] + [step_rows(gain)]
    outs = pl.pallas_call(
        _attn_kernel,
        grid=(batch, PAIRS),
        in_specs=in_specs,
        out_specs=[pl.BlockSpec((None, seq, LANES), lambda b, p: (p, b, 0))]
        + [step_rows(w) for w in weights],
        out_shape=[jax.ShapeDtypeStruct((PAIRS, batch * seq, LANES), BF16)]
        + [jax.ShapeDtypeStruct(w.shape, BF16) for w in weights],
        scratch_shapes=[pltpu.VMEM((seq // d * _pitch(d), LANES), F32) for d in DILATIONS for _ in range(3)],
        compiler_params=pltpu.CompilerParams(
            dimension_semantics=("arbitrary", "arbitrary"), vmem_limit_bytes=V7X_VMEM_LIMIT_BYTES),
        name="dilated_attention",
    )(a0, a0, a0, a1, a1, a1, a2, a2, a2, *weights, gain)
    return outs[0], outs[1:]


def _retention_kernel(q_ref, k_ref, v_ref, dec_ref, zeta_ref, xi_ref, cd_ref, o_ref, kv_scr, state_scr):
    c_len = RET_CHUNK
    n_chunks = q_ref.shape[0] // c_len
    dec = dec_ref[...]
    zeta = zeta_ref[...]
    xi = xi_ref[...]
    xi2 = jnp.concatenate([xi, xi], axis=1)
    cd = cd_ref[...]
    cd2 = jnp.concatenate([cd, cd], axis=1)

    def chunk_rows(c):
        return pl.ds(pl.multiple_of(c * c_len, c_len), c_len)

    def values(rows):
        return jnp.concatenate([v_ref[0, rows, :], v_ref[1, rows, :]], axis=1)

    def chunk_kv(c, carry):
        rows = chunk_rows(c)
        kz = (k_ref[rows, :].astype(F32) * zeta).astype(BF16)
        kv_scr[c] = lax.dot_general(kz, values(rows), (((0,), (0,)), ((), ())),
                                    preferred_element_type=F32)
        return carry

    lax.fori_loop(0, n_chunks, chunk_kv, 0, unroll=RET_UNROLL)

    def chunk_state(c, state):
        state_scr[c] = state.astype(BF16)
        return state * cd2 + kv_scr[c]

    lax.fori_loop(0, n_chunks, chunk_state, jnp.zeros((RET_KEY_DIM, RET_VALUE_DIM), F32),
                  unroll=RET_SCAN_UNROLL)

    def chunk_group_out(j, carry):
        chunks = [j * RET_UNROLL + i for i in range(RET_UNROLL)]
        scores = []
        for c in chunks:
            rows = chunk_rows(c)
            a = lax.dot_general(q_ref[rows, :], k_ref[rows, :], (((1,), (1,)), ((), ())),
                                preferred_element_type=F32) * dec
            scores.append(a.astype(BF16))
        for c, a in zip(chunks, scores):
            rows = chunk_rows(c)
            lhs = jnp.concatenate([a, q_ref[rows, :]], axis=1)
            rhs = jnp.concatenate([values(rows), state_scr[c]], axis=0)
            o = jnp.dot(lhs, rhs, preferred_element_type=F32) * xi2
            o_ref[0, rows, :] = o[:, :LANES].astype(o_ref.dtype)
            o_ref[1, rows, :] = o[:, LANES:].astype(o_ref.dtype)
        return carry

    lax.fori_loop(0, n_chunks // RET_UNROLL, chunk_group_out, 0)


def _retention(rest, tabs, batch, seq):
    dec, zeta, xi, cd = tabs
    vs = RET_VALUE_DIM // LANES
    tab_spec = pl.BlockSpec((None, RET_CHUNK, LANES), lambda b, h: (h, 0, 0))
    in_specs = [
        pl.BlockSpec((None, seq, LANES), lambda b, h: (REST_QR + h, b, 0)),
        pl.BlockSpec((None, seq, LANES), lambda b, h: (REST_KR + h, b, 0)),
        pl.BlockSpec((vs, seq, LANES), lambda b, h: (REST_VR // vs + h, b, 0)),
        tab_spec, tab_spec, tab_spec,
        pl.BlockSpec((None, 1, LANES), lambda b, h: (h, 0, 0)),
    ]
    return pl.pallas_call(
        _retention_kernel,
        grid=(batch, RET_HEADS),
        in_specs=in_specs,
        out_specs=pl.BlockSpec((vs, seq, LANES), lambda b, h: (h, b, 0)),
        out_shape=jax.ShapeDtypeStruct((RET_V_WIDTH // LANES, batch * seq, LANES), BF16),
        scratch_shapes=[pltpu.VMEM((seq // RET_CHUNK, RET_KEY_DIM, RET_VALUE_DIM), F32),
                        pltpu.VMEM((seq // RET_CHUNK, RET_KEY_DIM, RET_VALUE_DIM), BF16)],
        compiler_params=pltpu.CompilerParams(
            dimension_semantics=("arbitrary", "arbitrary"), vmem_limit_bytes=V7X_VMEM_LIMIT_BYTES),
        name="retention",
    )(rest, rest, rest, dec, zeta, xi, cd)


def _slabs(ref, lo=0, hi=None):
    hi = ref.shape[0] if hi is None else hi
    return jnp.concatenate([ref[k] for k in range(lo, hi)], axis=1)


def _tail_kernel(oa_ref, ret_ref, sg_ref, ga_ref, gr_ref, x_ref, wa_ref, wr_ref, wo_ref,
                 wg_ref, wu_ref, wd_ref, gf_ref, out_ref, ret_scr, act_scr):
    vs = RET_VALUE_DIM // LANES
    for h in range(RET_HEADS):
        o = _slabs(ret_ref, h * vs, (h + 1) * vs).astype(F32)
        oc = o - jnp.mean(o, axis=-1, keepdims=True)
        var = jnp.mean(oc * oc, axis=-1, keepdims=True)
        gate = _slabs(sg_ref, h * vs, (h + 1) * vs).astype(F32)
        gated = oc * lax.rsqrt(var + NORM_EPS) * (gate * jax.nn.sigmoid(gate))
        ret_scr[:, h * RET_VALUE_DIM:(h + 1) * RET_VALUE_DIM] = gated.astype(BF16)
    ya = jnp.dot(_slabs(oa_ref), wa_ref[...], preferred_element_type=F32)
    yr = jnp.dot(ret_scr[...], wr_ref[...], preferred_element_type=F32)
    merged = (jax.nn.sigmoid(_slabs(ga_ref).astype(F32)) * ya
              + jax.nn.sigmoid(_slabs(gr_ref).astype(F32)) * yr)
    x1 = x_ref[...] + jnp.dot(merged.astype(BF16), wo_ref[...], preferred_element_type=F32)
    scale = jnp.broadcast_to(lax.rsqrt(jnp.mean(x1 * x1, axis=-1, keepdims=True) + NORM_EPS),
                             (x1.shape[0], FFN_CHUNK))
    h2 = x1.astype(BF16)
    for c in range(0, FFN_HIDDEN, FFN_CHUNK):
        gate = jnp.dot(h2, wg_ref[:, c:c + FFN_CHUNK], preferred_element_type=F32) * scale
        up = jnp.dot(h2, wu_ref[:, c:c + FFN_CHUNK], preferred_element_type=F32) * scale
        act_scr[:, c:c + FFN_CHUNK] = (gate * jax.nn.sigmoid(gate) * up).astype(BF16)
    down = jnp.dot(act_scr[...], wd_ref[...], preferred_element_type=F32)
    out_ref[...] = _rms(x1 + down, gf_ref[...])


def _tail(oa, ret, rest, x2d, wa, wr, wo, wg, wu, wd, gf):
    t = x2d.shape[0]
    tm = TM_OUT
    gs = D_MODEL // LANES
    full = lambda shape: pl.BlockSpec(shape, lambda i: (0,) * len(shape), pipeline_mode=pl.Buffered(1))
    vec = pl.BlockSpec((1, D_MODEL), lambda i: (0, 0))
    row = pl.BlockSpec((tm, D_MODEL), lambda i: (i, 0))
    in_specs = [
        pl.BlockSpec((PAIRS, tm, LANES), lambda i: (0, i, 0)),
        pl.BlockSpec((gs, tm, LANES), lambda i: (0, i, 0)),
        pl.BlockSpec((gs, tm, LANES), lambda i: (REST_GR // gs, i, 0)),
        pl.BlockSpec((gs, tm, LANES), lambda i: (REST_GA // gs, i, 0)),
        pl.BlockSpec((gs, tm, LANES), lambda i: (REST_GRT // gs, i, 0)),
        row,
        full((GROUP_WIDTH, D_MODEL)), full((RET_V_WIDTH, D_MODEL)), full((D_MODEL, D_MODEL)),
        full((D_MODEL, FFN_HIDDEN)), full((D_MODEL, FFN_HIDDEN)), full((FFN_HIDDEN, D_MODEL)), vec,
    ]
    return pl.pallas_call(
        _tail_kernel,
        grid=(t // tm,),
        in_specs=in_specs,
        out_specs=row,
        out_shape=jax.ShapeDtypeStruct((t, D_MODEL), F32),
        scratch_shapes=[pltpu.VMEM((tm, RET_V_WIDTH), BF16), pltpu.VMEM((tm, FFN_HIDDEN), BF16)],
        compiler_params=pltpu.CompilerParams(
            dimension_semantics=("arbitrary",), vmem_limit_bytes=V7X_VMEM_LIMIT_BYTES),
        name="merge_ffn",
    )(oa, ret, rest, rest, rest, x2d, wa, wr, wo, wg, wu, wd, gf)


def _row_gain(g):
    return jnp.broadcast_to(g[:, None], (g.shape[0], LANES))


def _position_tables(seq):
    pos = np.arange(seq, dtype=np.float64)
    inv = ROPE_THETA ** (-np.arange(0, HEAD_DIM, 2, dtype=np.float64) / HEAD_DIM)
    ang = pos[:, None] * inv[None, :]
    c, s = np.cos(ang), np.sin(ang)
    cos_a = np.concatenate([c, c, c, c], axis=1)
    sin_a = np.concatenate([-s, s, -s, s], axis=1)
    base = 1.0 / (ROPE_THETA ** np.linspace(0.0, 1.0, RET_KEY_DIM // 2, dtype=np.float64))
    ang_r = pos[:, None] * base[None, :]
    cr, sr = np.cos(ang_r), np.sin(ang_r)
    cos_r = np.repeat(cr, 2, axis=1)
    sin_r = np.repeat(sr, 2, axis=1) * np.tile([-1.0, 1.0], RET_KEY_DIM // 2)[None, :]
    return [jnp.asarray(t.astype(np.float32)) for t in (cos_a, sin_a, cos_r, sin_r)]


def _retention_tables():
    c = RET_CHUNK
    log_g = np.log1p(-(2.0 ** (-5.0 - np.arange(RET_HEADS, dtype=np.float64))))
    idx = np.arange(c, dtype=np.float64)
    diff = idx[:, None] - idx[None, :]
    decay = np.where(diff[None] >= 0, np.exp(-(idx + 1.0)[None, None, :] * log_g[:, None, None]), 0.0)
    zeta = np.exp((c - 1 - idx)[None, :] * log_g[:, None])
    xi = np.exp((idx + 1.0)[None, :] * log_g[:, None])
    chunk_decay = np.exp(c * log_g)
    bcast = lambda v: np.broadcast_to(v[:, :, None], (RET_HEADS, c, LANES))
    cd = np.broadcast_to(chunk_decay[:, None, None], (RET_HEADS, 1, LANES))
    return tuple(jnp.asarray(np.ascontiguousarray(t).astype(np.float32))
                 for t in (decay, bcast(zeta), bcast(xi), cd))


def kernel(x, norm_mix_g, w_in, w_out_attn, w_out_ret, w_out, norm_ffn_g, w_ffn_gate, w_ffn_up,
           w_ffn_down, norm_final_g):
    batch, seq, _ = x.shape
    assert w_in.shape[0] == 1, "single-layer problem"
    assert seq % TM_IN == 0 and TM_IN % (DILATIONS[-1] * BF16_SUBLANES) == 0
    x2d = x.reshape(batch * seq, D_MODEL)
    a0, a1, a2, rest = _in_projection(
        x2d, w_in[0], _row_gain(norm_mix_g[0]), _position_tables(seq), batch, seq)
    tail_weights = [w_out_attn[0], w_out_ret[0], w_out[0], w_ffn_gate[0], w_ffn_up[0], w_ffn_down[0]]
    oa, (wa, wr, wo, wg, wu, wd) = _dilated_attention(
        a0, a1, a2, tail_weights, _row_gain(norm_ffn_g[0]), batch, seq)
    ret = _retention(rest, _retention_tables(), batch, seq)
    out = _tail(oa, ret, rest, x2d, wa, wr, wo, wg, wu, wd, norm_final_g[None, :])
    return out.reshape(batch, seq, D_MODEL)
```

```python
import functools

import jax
import jax.numpy as jnp
import numpy as np
from jax import lax
from jax.experimental import pallas as pl
from jax.experimental.pallas import tpu as pltpu

F32 = jnp.float32
BF16 = jnp.bfloat16

LANES = 128
BF16_SUBLANES = 16
V7X_VMEM_LIMIT_BYTES = 60 * 1024 * 1024

D_MODEL = 1024
HEAD_DIM = 64
HEADS_PER_GROUP = 8
DILATIONS = (1, 4, 16)
BAND = 128
GROUP_WIDTH = HEADS_PER_GROUP * HEAD_DIM
ATT_WIDTH = len(DILATIONS) * GROUP_WIDTH
PAIRS = GROUP_WIDTH // LANES
ROPE_THETA = 10000.0

RET_HEADS = 4
RET_KEY_DIM = 128
RET_VALUE_DIM = 256
RET_QK_WIDTH = RET_HEADS * RET_KEY_DIM
RET_V_WIDTH = RET_HEADS * RET_VALUE_DIM
RET_CHUNK = 128

FFN_HIDDEN = 2816
NORM_EPS = 1e-6
MASK_VALUE = -1e30

IN_WIDTH = 3 * ATT_WIDTH + 2 * RET_QK_WIDTH + 2 * RET_V_WIDTH + 2 * D_MODEL
COL_TILE = 512
ATT_TILES = 3 * ATT_WIDTH // COL_TILE
REST_SLABS = (IN_WIDTH - 3 * ATT_WIDTH) // LANES
REST_QR, REST_KR, REST_VR, REST_GR, REST_GA, REST_GRT = 0, 4, 8, 16, 24, 32

TM_IN = 512
TM_OUT = 512
FFN_CHUNK = 256
ATT_BODY_BLOCKS = 32
MERGE_ROWS = 64
MERGE_UNROLL = 8
ATT_SKEW = 2
LOG2_E = 1.4426950408889634
RET_UNROLL = 32
RET_SCAN_UNROLL = 8
TAIL_WEIGHT_GAINED = (False, False, False, True, True, False)


def _rms(x, g):
    ms = jnp.mean(x * x, axis=-1, keepdims=True)
    return x * lax.rsqrt(ms + NORM_EPS) * g


def _rotate(t, cos, sin_signed, span):
    lane = lax.broadcasted_iota(jnp.int32, (1, LANES), 1)
    low = (lane % (2 * span)) < span
    partner = jnp.where(low, pltpu.roll(t, LANES - span, 1), pltpu.roll(t, span, 1))
    return t * cos + partner * sin_signed


W_STAGE_BYTES = 5 << 17
W_STAGE_SLOTS = 8


def _stage_rows(cols):
    return W_STAGE_BYTES // (4 * cols) // BF16_SUBLANES * BF16_SUBLANES


def _load_weights_bf16(jobs, stage, sems):
    slots, stage_rows, _ = stage.shape
    tiles = [(j, r0, min(stage_rows, jobs[j][1].shape[0] - r0))
             for j in range(len(jobs)) for r0 in range(0, jobs[j][1].shape[0], stage_rows)]

    def copy(k):
        j, r0, nr = tiles[k]
        cols = jobs[j][1].shape[1]
        return pltpu.make_async_copy(jobs[j][0].at[pl.ds(r0, nr), :],
                                     stage.at[k % slots, pl.ds(0, nr), pl.ds(0, cols)],
                                     sems.at[k % slots])

    for k in range(min(slots - 1, len(tiles))):
        copy(k).start()
    for k, (j, r0, nr) in enumerate(tiles):
        _, dst, row_gain, scale, scaled_cols = jobs[j]
        if k + slots - 1 < len(tiles):
            copy(k + slots - 1).start()
        copy(k).wait()
        for c in range(0, dst.shape[1], LANES):
            tile = stage[k % slots, 0:nr, c:c + LANES]
            if row_gain is not None:
                tile = tile * row_gain[r0:r0 + nr, :]
            if c < scaled_cols:
                tile = tile * scale
            dst[r0:r0 + nr, c:c + LANES] = tile.astype(BF16)


def _inproj_kernel(x_ref, w_hbm, gain_ref, ca, sa, cr, sr,
                   a0_ref, a1_ref, a2_ref, rest_ref, y_scr, h_scr, tab_scr, rs_scr,
                   w_ref, w_stage, w_sems):
    tm = x_ref.shape[0]
    n_slabs = D_MODEL // LANES

    @pl.when(pl.program_id(0) == 0)
    def _():
        _load_weights_bf16([(w_hbm, w_ref, gain_ref, HEAD_DIM ** -0.5 * LOG2_E, ATT_WIDTH)],
                           w_stage, w_sems)

    x = x_ref[...]
    h_scr[0] = x.astype(BF16)
    rs_scr[0] = jnp.broadcast_to(lax.rsqrt(jnp.mean(x * x, axis=-1, keepdims=True) + NORM_EPS),
                                 (tm, LANES))
    for k in range(n_slabs):
        y_scr[k] = x[:, k * LANES:(k + 1) * LANES]
    for gi, d in enumerate(DILATIONS):
        if d == 1:
            continue
        n = tm // d
        for r in range(d):
            rs_scr[gi, r * n:(r + 1) * n, :] = rs_scr[0, pl.ds(r, n, stride=d), :]
            for k in range(n_slabs):
                h_scr[gi, r * n:(r + 1) * n, k * LANES:(k + 1) * LANES] = (
                    y_scr[k, pl.ds(r, n, stride=d), :].astype(BF16))

    for gi, d in enumerate(DILATIONS):
        if d == 1:
            continue
        n = tm // d
        for r in range(d):
            tab_scr[gi - 1, 0, r * n:(r + 1) * n, :] = ca[pl.ds(r, n, stride=d), :]
            tab_scr[gi - 1, 1, r * n:(r + 1) * n, :] = sa[pl.ds(r, n, stride=d), :]

    def att_tables(g):
        if g == 0:
            return ca[...], sa[...]
        return tab_scr[g - 1, 0], tab_scr[g - 1, 1]

    def write_att(g, slab, val):
        v = val.astype(BF16)
        d = DILATIONS[g]
        n = tm // d
        if g == 0:
            a0_ref[slab] = v
        else:
            out = a1_ref if g == 1 else a2_ref
            for r in range(d):
                out[slab, r] = v[r * n:(r + 1) * n, :]

    def att_tile(c):
        kind, g = divmod(c, 3)
        res = jnp.dot(h_scr[g], w_ref[:, c * COL_TILE:(c + 1) * COL_TILE],
                      preferred_element_type=F32)
        for k in range(PAIRS):
            s = res[:, k * LANES:(k + 1) * LANES] * rs_scr[g]
            if kind < 2:
                s = _rotate(s, *att_tables(g), span=HEAD_DIM // 2)
            write_att(g, kind * PAIRS + k, s)

    def rest_tile(c):
        col = 3 * ATT_WIDTH + c * COL_TILE
        res = jnp.dot(h_scr[0], w_ref[:, col:col + COL_TILE], preferred_element_type=F32)
        for k in range(COL_TILE // LANES):
            slab = c * (COL_TILE // LANES) + k
            s = res[:, k * LANES:(k + 1) * LANES] * rs_scr[0]
            if slab < REST_VR:
                s = _rotate(s, cr[...], sr[...], span=1)
                if slab >= REST_KR:
                    s = s * (RET_KEY_DIM ** -0.5)
            rest_ref[slab] = s.astype(BF16)

    for c in range(REST_SLABS * LANES // COL_TILE):
        rest_tile(c)
    for g in range(len(DILATIONS)):
        for kind in range(3):
            att_tile(kind * 3 + g)


def _in_projection(x2d, w, gain, tabs, batch, seq):
    t = x2d.shape[0]
    tm = TM_IN
    nt = seq // tm
    n_steps = t // tm
    tab_spec = pl.BlockSpec((tm, LANES), lambda i: (i % nt, 0))
    in_specs = [
        pl.BlockSpec((tm, D_MODEL), lambda i: (i, 0)),
        pl.BlockSpec(memory_space=pl.ANY),
        pl.BlockSpec((D_MODEL, LANES), lambda i: (0, 0), pipeline_mode=pl.Buffered(1)),
    ] + [tab_spec] * 4
    n_att = 3 * PAIRS
    d1, d2 = DILATIONS[1], DILATIONS[2]
    out_shape = [
        jax.ShapeDtypeStruct((n_att, t, LANES), BF16),
        jax.ShapeDtypeStruct((n_att, batch, d1, seq // d1, LANES), BF16),
        jax.ShapeDtypeStruct((n_att, batch, d2, seq // d2, LANES), BF16),
        jax.ShapeDtypeStruct((REST_SLABS, t, LANES), BF16),
    ]
    out_specs = [
        pl.BlockSpec((n_att, tm, LANES), lambda i: (0, i, 0)),
        pl.BlockSpec((n_att, None, d1, tm // d1, LANES), lambda i: (0, i // nt, 0, i % nt, 0)),
        pl.BlockSpec((n_att, None, d2, tm // d2, LANES), lambda i: (0, i // nt, 0, i % nt, 0)),
        pl.BlockSpec((REST_SLABS, tm, LANES), lambda i: (0, i, 0)),
    ]
    return pl.pallas_call(
        _inproj_kernel,
        grid=(n_steps,),
        in_specs=in_specs,
        out_specs=out_specs,
        out_shape=out_shape,
        scratch_shapes=[
            pltpu.VMEM((D_MODEL // LANES, tm, LANES), F32),
            pltpu.VMEM((len(DILATIONS), tm, D_MODEL), BF16),
            pltpu.VMEM((len(DILATIONS) - 1, 2, tm, LANES), F32),
            pltpu.VMEM((len(DILATIONS), tm, LANES), F32),
            pltpu.VMEM((D_MODEL, IN_WIDTH), BF16),
            pltpu.VMEM((W_STAGE_SLOTS, _stage_rows(IN_WIDTH), IN_WIDTH), F32),
            pltpu.SemaphoreType.DMA((W_STAGE_SLOTS,)),
        ],
        compiler_params=pltpu.CompilerParams(
            dimension_semantics=("arbitrary",), vmem_limit_bytes=V7X_VMEM_LIMIT_BYTES),
        name="in_projection",
    )(x2d, w, gain, *tabs)


def _pitch(d):
    return d + 1 if d % 8 == 0 else d


def _attn_scores(q, kk, bias, first_head):
    zero = jnp.zeros_like(q)
    q2 = jnp.concatenate([jnp.where(first_head, q, zero), jnp.where(first_head, zero, q)], axis=0)
    return lax.dot_general(q2, kk, (((1,), (1,)), ((), ())), preferred_element_type=F32) + bias


def _attn_partials(s, vv, first_head):
    nq = s.shape[0] // 2
    m = jnp.max(s, axis=1, keepdims=True)
    p = jnp.exp2(s - m)
    l = jnp.sum(p, axis=1, keepdims=True)
    pv = jnp.dot(p.astype(BF16), vv, preferred_element_type=F32)
    acc = jnp.where(first_head, pv[:nq], pv[nq:])
    den = jnp.where(first_head, l[:nq], l[nq:])
    mm = jnp.where(first_head, m[:nq], m[nq:])
    return acc, den, mm


def _attn_kernel(q0, k0, v0, q1, k1, v1, q2, k2, v2, *refs):
    n_w = len(TAIL_WEIGHT_GAINED)
    w_src, gain_ref, o_ref = refs[:n_w], refs[n_w], refs[n_w + 1]
    w_dst = refs[n_w + 2:2 * n_w + 2]
    acc0, den0, max0, acc1, den1, max1, acc2, den2, max2 = refs[2 * n_w + 2:]

    for src, dst, gained in zip(w_src, w_dst, TAIL_WEIGHT_GAINED):
        for c in range(0, src.shape[1], LANES):
            tile = src[:, c:c + LANES]
            if gained:
                tile = tile * gain_ref[...]
            dst[:, c:c + LANES] = tile.astype(BF16)

    blk = BAND
    seq = q0.shape[0]
    lane = lax.broadcasted_iota(jnp.int32, (1, LANES), 1)
    first_head = lane < HEAD_DIM
    qi = lax.broadcasted_iota(jnp.int32, (2 * blk, 2 * blk), 0) & (blk - 1)
    kj = lax.broadcasted_iota(jnp.int32, (2 * blk, 2 * blk), 1)
    band_bias = jnp.where((kj >= qi) & (kj <= qi + BAND), 0.0, MASK_VALUE).astype(F32)
    qi1 = lax.broadcasted_iota(jnp.int32, (2 * blk, blk), 0) & (blk - 1)
    kj1 = lax.broadcasted_iota(jnp.int32, (2 * blk, blk), 1)
    first_bias = jnp.where(kj1 <= qi1, 0.0, MASK_VALUE).astype(F32)

    def key_rows(n):
        if isinstance(n, int) and n == 0:
            return pl.ds(0, blk), first_bias
        st = n * blk if isinstance(n, int) else pl.multiple_of(n * blk, blk)
        return pl.ds(st - blk, 2 * blk), band_bias

    def query_rows(n):
        return pl.ds(n * blk if isinstance(n, int) else pl.multiple_of(n * blk, blk), blk)

    def run_blocks(blocks, emit):
        scores = []
        for i in range(len(blocks) + ATT_SKEW):
            if i < len(blocks):
                qr, kr, _, n = blocks[i]
                rows, bias = key_rows(n)
                scores.append(_attn_scores(qr[query_rows(n), :], kr[rows, :], bias, first_head))
            if i >= ATT_SKEW:
                j = i - ATT_SKEW
                _, _, vr, n = blocks[j]
                rows, _ = key_rows(n)
                emit(j, _attn_partials(scores[j], vr[rows, :], first_head))
                scores[j] = None

    def dilated_group(qr, kr, vr, outs, d):
        nb = seq // d // blk
        subs = max(1, ATT_BODY_BLOCKS // nb)

        def body(j, carry):
            blocks, where = [], []
            for ri in range(subs):
                r = j * subs + ri
                for n in range(nb):
                    blocks.append((qr.at[r], kr.at[r], vr.at[r], n))
                    where.append(pl.ds(n * (blk * _pitch(d)) + r, blk, stride=_pitch(d)))

            def emit(i, parts):
                for ref, val in zip(outs, parts):
                    ref[where[i], :] = val

            run_blocks(blocks, emit)
            return carry

        lax.fori_loop(0, d // subs, body, 0)

    dilated_group(q1, k1, v1, (acc1, den1, max1), DILATIONS[1])
    dilated_group(q2, k2, v2, (acc2, den2, max2), DILATIONS[2])

    def dense_body(j, carry):
        ns = [j * ATT_BODY_BLOCKS + i for i in range(ATT_BODY_BLOCKS)]

        def emit(i, parts):
            for ref, val in zip((acc0, den0, max0), parts):
                ref[query_rows(ns[i]), :] = val

        run_blocks([(q0, k0, v0, n) for n in ns], emit)
        return carry

    dense_body(0, 0)
    lax.fori_loop(1, seq // blk // ATT_BODY_BLOCKS, dense_body, 0)

    def merge(t, carry):
        rows = pl.ds(pl.multiple_of(t * MERGE_ROWS, MERGE_ROWS), MERGE_ROWS)

        def padded(ref, d):
            if _pitch(d) == d:
                return ref[rows, :]
            per = MERGE_ROWS // d
            return jnp.concatenate(
                [ref[pl.ds((t * per + i) * _pitch(d), d), :] for i in range(per)], axis=0)

        d2 = DILATIONS[2]
        m0, m1, m2 = max0[rows, :], max1[rows, :], padded(max2, d2)
        mx = jnp.maximum(m0, jnp.maximum(m1, m2))
        e0, e1, e2 = jnp.exp2(m0 - mx), jnp.exp2(m1 - mx), jnp.exp2(m2 - mx)
        num = e0 * acc0[rows, :] + e1 * acc1[rows, :] + e2 * padded(acc2, d2)
        den = e0 * den0[rows, :] + e1 * den1[rows, :] + e2 * padded(den2, d2)
        o_ref[rows, :] = (num * (1.0 / den)).astype(o_ref.dtype)
        return carry

    lax.fori_loop(0, seq // MERGE_ROWS, merge, 0, unroll=MERGE_UNROLL)


def _dilated_attention(a0, a1, a2, weights, gain, batch, seq):
    d1, d2 = DILATIONS[1], DILATIONS[2]
    n_steps = batch * PAIRS

    def step_rows(w):
        rows, cols = w.shape
        n_blocks = max(n for n in range(1, n_steps + 1) if rows % (n * BF16_SUBLANES) == 0)
        return pl.BlockSpec((rows // n_blocks, cols),
                            lambda b, p: (jnp.minimum(b * PAIRS + p, n_blocks - 1), 0))

    for w in weights:
        assert w.shape[0] % BF16_SUBLANES == 0 and w.shape[1] % LANES == 0, w.shape
    assert gain.shape[0] == weights[3].shape[0] == weights[4].shape[0]
    specs0 = [pl.BlockSpec((None, seq, LANES), functools.partial(lambda b, p, k: (k * PAIRS + p, b, 0), k=k))
              for k in range(3)]
    specs1 = [pl.BlockSpec((None, None, d1, seq // d1, LANES),
                           functools.partial(lambda b, p, k: (k * PAIRS + p, b, 0, 0, 0), k=k))
              for k in range(3)]
    specs2 = [pl.BlockSpec((None, None, d2, seq // d2, LANES),
                           functools.partial(lambda b, p, k: (k * PAIRS + p, b, 0, 0, 0), k=k))
              for k in range(3)]
    in_specs = specs0 + specs1 + specs2 + [step_rows(w) for w in weights] + [step_rows(gain)]
    outs = pl.pallas_call(
        _attn_kernel,
        grid=(batch, PAIRS),
        in_specs=in_specs,
        out_specs=[pl.BlockSpec((None, seq, LANES), lambda b, p: (p, b, 0))]
        + [step_rows(w) for w in weights],
        out_shape=[jax.ShapeDtypeStruct((PAIRS, batch * seq, LANES), BF16)]
        + [jax.ShapeDtypeStruct(w.shape, BF16) for w in weights],
        scratch_shapes=[pltpu.VMEM((seq // d * _pitch(d), LANES), F32) for d in DILATIONS for _ in range(3)],
        compiler_params=pltpu.CompilerParams(
            dimension_semantics=("arbitrary", "arbitrary"), vmem_limit_bytes=V7X_VMEM_LIMIT_BYTES),
        name="dilated_attention",
    )(a0, a0, a0, a1, a1, a1, a2, a2, a2, *weights, gain)
    return outs[0], outs[1:]


def _retention_kernel(q_ref, k_ref, v_ref, dec_ref, zeta_ref, xi_ref, cd_ref, o_ref, kv_scr, state_scr):
    c_len = RET_CHUNK
    n_chunks = q_ref.shape[0] // c_len
    dec = dec_ref[...]
    zeta = zeta_ref[...]
    xi = xi_ref[...]
    xi2 = jnp.concatenate([xi, xi], axis=1)
    cd = cd_ref[...]
    cd2 = jnp.concatenate([cd, cd], axis=1)

    def chunk_rows(c):
        return pl.ds(pl.multiple_of(c * c_len, c_len), c_len)

    def values(rows):
        return jnp.concatenate([v_ref[0, rows, :], v_ref[1, rows, :]], axis=1)

    def chunk_kv(c, carry):
        rows = chunk_rows(c)
        kz = (k_ref[rows, :].astype(F32) * zeta).astype(BF16)
        kv_scr[c] = lax.dot_general(kz, values(rows), (((0,), (0,)), ((), ())),
                                    preferred_element_type=F32)
        return carry

    lax.fori_loop(0, n_chunks, chunk_kv, 0, unroll=RET_UNROLL)

    def chunk_state(c, state):
        state_scr[c] = state.astype(BF16)
        return state * cd2 + kv_scr[c]

    lax.fori_loop(0, n_chunks, chunk_state, jnp.zeros((RET_KEY_DIM, RET_VALUE_DIM), F32),
                  unroll=RET_SCAN_UNROLL)

    def chunk_group_out(j, carry):
        chunks = [j * RET_UNROLL + i for i in range(RET_UNROLL)]
        scores = []
        for c in chunks:
            rows = chunk_rows(c)
            a = lax.dot_general(q_ref[rows, :], k_ref[rows, :], (((1,), (1,)), ((), ())),
                                preferred_element_type=F32) * dec
            scores.append(a.astype(BF16))
        for c, a in zip(chunks, scores):
            rows = chunk_rows(c)
            lhs = jnp.concatenate([a, q_ref[rows, :]], axis=1)
            rhs = jnp.concatenate([values(rows), state_scr[c]], axis=0)
            o = jnp.dot(lhs, rhs, preferred_element_type=F32) * xi2
            o_ref[0, rows, :] = o[:, :LANES].astype(o_ref.dtype)
            o_ref[1, rows, :] = o[:, LANES:].astype(o_ref.dtype)
        return carry

    lax.fori_loop(0, n_chunks // RET_UNROLL, chunk_group_out, 0)


def _retention(rest, tabs, batch, seq):
    dec, zeta, xi, cd = tabs
    vs = RET_VALUE_DIM // LANES
    tab_spec = pl.BlockSpec((None, RET_CHUNK, LANES), lambda b, h: (h, 0, 0))
    in_specs = [
        pl.BlockSpec((None, seq, LANES), lambda b, h: (REST_QR + h, b, 0)),
        pl.BlockSpec((None, seq, LANES), lambda b, h: (REST_KR + h, b, 0)),
        pl.BlockSpec((vs, seq, LANES), lambda b, h: (REST_VR // vs + h, b, 0)),
        tab_spec, tab_spec, tab_spec,
        pl.BlockSpec((None, 1, LANES), lambda b, h: (h, 0, 0)),
    ]
    return pl.pallas_call(
        _retention_kernel,
        grid=(batch, RET_HEADS),
        in_specs=in_specs,
        out_specs=pl.BlockSpec((vs, seq, LANES), lambda b, h: (h, b, 0)),
        out_shape=jax.ShapeDtypeStruct((RET_V_WIDTH // LANES, batch * seq, LANES), BF16),
        scratch_shapes=[pltpu.VMEM((seq // RET_CHUNK, RET_KEY_DIM, RET_VALUE_DIM), F32),
                        pltpu.VMEM((seq // RET_CHUNK, RET_KEY_DIM, RET_VALUE_DIM), BF16)],
        compiler_params=pltpu.CompilerParams(
            dimension_semantics=("arbitrary", "arbitrary"), vmem_limit_bytes=V7X_VMEM_LIMIT_BYTES),
        name="retention",
    )(rest, rest, rest, dec, zeta, xi, cd)


def _slabs(ref, lo=0, hi=None):
    hi = ref.shape[0] if hi is None else hi
    return jnp.concatenate([ref[k] for k in range(lo, hi)], axis=1)


def _tail_kernel(oa_ref, ret_ref, sg_ref, ga_ref, gr_ref, x_ref, wa_ref, wr_ref, wo_ref,
                 wg_ref, wu_ref, wd_ref, gf_ref, out_ref, ret_scr, act_scr):
    vs = RET_VALUE_DIM // LANES
    for h in range(RET_HEADS):
        o = _slabs(ret_ref, h * vs, (h + 1) * vs).astype(F32)
        oc = o - jnp.mean(o, axis=-1, keepdims=True)
        var = jnp.mean(oc * oc, axis=-1, keepdims=True)
        gate = _slabs(sg_ref, h * vs, (h + 1) * vs).astype(F32)
        gated = oc * lax.rsqrt(var + NORM_EPS) * (gate * jax.nn.sigmoid(gate))
        ret_scr[:, h * RET_VALUE_DIM:(h + 1) * RET_VALUE_DIM] = gated.astype(BF16)
    ya = jnp.dot(_slabs(oa_ref), wa_ref[...], preferred_element_type=F32)
    yr = jnp.dot(ret_scr[...], wr_ref[...], preferred_element_type=F32)
    merged = (jax.nn.sigmoid(_slabs(ga_ref).astype(F32)) * ya
              + jax.nn.sigmoid(_slabs(gr_ref).astype(F32)) * yr)
    x1 = x_ref[...] + jnp.dot(merged.astype(BF16), wo_ref[...], preferred_element_type=F32)
    scale = jnp.broadcast_to(lax.rsqrt(jnp.mean(x1 * x1, axis=-1, keepdims=True) + NORM_EPS),
                             (x1.shape[0], FFN_CHUNK))
    h2 = x1.astype(BF16)
    for c in range(0, FFN_HIDDEN, FFN_CHUNK):
        gate = jnp.dot(h2, wg_ref[:, c:c + FFN_CHUNK], preferred_element_type=F32) * scale
        up = jnp.dot(h2, wu_ref[:, c:c + FFN_CHUNK], preferred_element_type=F32) * scale
        act_scr[:, c:c + FFN_CHUNK] = (gate * jax.nn.sigmoid(gate) * up).astype(BF16)
    down = jnp.dot(act_scr[...], wd_ref[...], preferred_element_type=F32)
    out_ref[...] = _rms(x1 + down, gf_ref[...])


def _tail(oa, ret, rest, x2d, wa, wr, wo, wg, wu, wd, gf):
    t = x2d.shape[0]
    tm = TM_OUT
    gs = D_MODEL // LANES
    full = lambda shape: pl.BlockSpec(shape, lambda i: (0,) * len(shape), pipeline_mode=pl.Buffered(1))
    vec = pl.BlockSpec((1, D_MODEL), lambda i: (0, 0))
    row = pl.BlockSpec((tm, D_MODEL), lambda i: (i, 0))
    in_specs = [
        pl.BlockSpec((PAIRS, tm, LANES), lambda i: (0, i, 0)),
        pl.BlockSpec((gs, tm, LANES), lambda i: (0, i, 0)),
        pl.BlockSpec((gs, tm, LANES), lambda i: (REST_GR // gs, i, 0)),
        pl.BlockSpec((gs, tm, LANES), lambda i: (REST_GA // gs, i, 0)),
        pl.BlockSpec((gs, tm, LANES), lambda i: (REST_GRT // gs, i, 0)),
        row,
        full((GROUP_WIDTH, D_MODEL)), full((RET_V_WIDTH, D_MODEL)), full((D_MODEL, D_MODEL)),
        full((D_MODEL, FFN_HIDDEN)), full((D_MODEL, FFN_HIDDEN)), full((FFN_HIDDEN, D_MODEL)), vec,
    ]
    return pl.pallas_call(
        _tail_kernel,
        grid=(t // tm,),
        in_specs=in_specs,
        out_specs=row,
        out_shape=jax.ShapeDtypeStruct((t, D_MODEL), F32),
        scratch_shapes=[pltpu.VMEM((tm, RET_V_WIDTH), BF16), pltpu.VMEM((tm, FFN_HIDDEN), BF16)],
        compiler_params=pltpu.CompilerParams(
            dimension_semantics=("arbitrary",), vmem_limit_bytes=V7X_VMEM_LIMIT_BYTES),
        name="merge_ffn",
    )(oa, ret, rest, rest, rest, x2d, wa, wr, wo, wg, wu, wd, gf)


def _row_gain(g):
    return jnp.broadcast_to(g[:, None], (g.shape[0], LANES))


def _position_tables(seq):
    pos = np.arange(seq, dtype=np.float64)
    inv = ROPE_THETA ** (-np.arange(0, HEAD_DIM, 2, dtype=np.float64) / HEAD_DIM)
    ang = pos[:, None] * inv[None, :]
    c, s = np.cos(ang), np.sin(ang)
    cos_a = np.concatenate([c, c, c, c], axis=1)
    sin_a = np.concatenate([-s, s, -s, s], axis=1)
    base = 1.0 / (ROPE_THETA ** np.linspace(0.0, 1.0, RET_KEY_DIM // 2, dtype=np.float64))
    ang_r = pos[:, None] * base[None, :]
    cr, sr = np.cos(ang_r), np.sin(ang_r)
    cos_r = np.repeat(cr, 2, axis=1)
    sin_r = np.repeat(sr, 2, axis=1) * np.tile([-1.0, 1.0], RET_KEY_DIM // 2)[None, :]
    return [jnp.asarray(t.astype(np.float32)) for t in (cos_a, sin_a, cos_r, sin_r)]


def _retention_tables():
    c = RET_CHUNK
    log_g = np.log1p(-(2.0 ** (-5.0 - np.arange(RET_HEADS, dtype=np.float64))))
    idx = np.arange(c, dtype=np.float64)
    diff = idx[:, None] - idx[None, :]
    decay = np.where(diff[None] >= 0, np.exp(-(idx + 1.0)[None, None, :] * log_g[:, None, None]), 0.0)
    zeta = np.exp((c - 1 - idx)[None, :] * log_g[:, None])
    xi = np.exp((idx + 1.0)[None, :] * log_g[:, None])
    chunk_decay = np.exp(c * log_g)
    bcast = lambda v: np.broadcast_to(v[:, :, None], (RET_HEADS, c, LANES))
    cd = np.broadcast_to(chunk_decay[:, None, None], (RET_HEADS, 1, LANES))
    return tuple(jnp.asarray(np.ascontiguousarray(t).astype(np.float32))
                 for t in (decay, bcast(zeta), bcast(xi), cd))


def kernel(x, norm_mix_g, w_in, w_out_attn, w_out_ret, w_out, norm_ffn_g, w_ffn_gate, w_ffn_up,
           w_ffn_down, norm_final_g):
    batch, seq, _ = x.shape
    assert w_in.shape[0] == 1, "single-layer problem"
    assert seq % TM_IN == 0 and TM_IN % (DILATIONS[-1] * BF16_SUBLANES) == 0
    x2d = x.reshape(batch * seq, D_MODEL)
    a0, a1, a2, rest = _in_projection(
        x2d, w_in[0], _row_gain(norm_mix_g[0]), _position_tables(seq), batch, seq)
    tail_weights = [w_out_attn[0], w_out_ret[0], w_out[0], w_ffn_gate[0], w_ffn_up[0], w_ffn_down[0]]
    oa, (wa, wr, wo, wg, wu, wd) = _dilated_attention(
        a0, a1, a2, tail_weights, _row_gain(norm_ffn_g[0]), batch, seq)
    ret = _retention(rest, _retention_tables(), batch, seq)
    out = _tail(oa, ret, rest, x2d, wa, wr, wo, wg, wu, wd, norm_final_g[None, :])
    return out.reshape(batch, seq, D_MODEL)
```

```python
import functools

import jax
import jax.numpy as jnp
import numpy as np
from jax import lax
from jax.experimental import pallas as pl
from jax.experimental.pallas import tpu as pltpu

F32 = jnp.float32
BF16 = jnp.bfloat16

LANES = 128
BF16_SUBLANES = 16
V7X_VMEM_LIMIT_BYTES = 60 * 1024 * 1024

D_MODEL = 1024
HEAD_DIM = 64
HEADS_PER_GROUP = 8
DILATIONS = (1, 4, 16)
BAND = 128
GROUP_WIDTH = HEADS_PER_GROUP * HEAD_DIM
ATT_WIDTH = len(DILATIONS) * GROUP_WIDTH
PAIRS = GROUP_WIDTH // LANES
ROPE_THETA = 10000.0

RET_HEADS = 4
RET_KEY_DIM = 128
RET_VALUE_DIM = 256
RET_QK_WIDTH = RET_HEADS * RET_KEY_DIM
RET_V_WIDTH = RET_HEADS * RET_VALUE_DIM
RET_CHUNK = 128

FFN_HIDDEN = 2816
NORM_EPS = 1e-6
MASK_VALUE = -1e30

IN_WIDTH = 3 * ATT_WIDTH + 2 * RET_QK_WIDTH + 2 * RET_V_WIDTH + 2 * D_MODEL
COL_TILE = 512
ATT_TILES = 3 * ATT_WIDTH // COL_TILE
REST_SLABS = (IN_WIDTH - 3 * ATT_WIDTH) // LANES
REST_QR, REST_KR, REST_VR, REST_GR, REST_GA, REST_GRT = 0, 4, 8, 16, 24, 32

TM_IN = 512
TM_OUT = 512
FFN_CHUNK = 256
ATT_BODY_BLOCKS = 32
MERGE_ROWS = 64
MERGE_UNROLL = 8
ATT_SKEW = 2
LOG2_E = 1.4426950408889634
TAIL_WEIGHT_GAINED = (False, False, False, True, True, False)


def _rms(x, g):
    ms = jnp.mean(x * x, axis=-1, keepdims=True)
    return x * lax.rsqrt(ms + NORM_EPS) * g


def _rotate(t, cos, sin_signed, span):
    lane = lax.broadcasted_iota(jnp.int32, (1, LANES), 1)
    low = (lane % (2 * span)) < span
    partner = jnp.where(low, pltpu.roll(t, LANES - span, 1), pltpu.roll(t, span, 1))
    return t * cos + partner * sin_signed


W_STAGE_BYTES = 5 << 17
W_STAGE_SLOTS = 8


def _stage_rows(cols):
    return W_STAGE_BYTES // (4 * cols) // BF16_SUBLANES * BF16_SUBLANES


def _load_weights_bf16(jobs, stage, sems):
    slots, stage_rows, _ = stage.shape
    tiles = [(j, r0, min(stage_rows, jobs[j][1].shape[0] - r0))
             for j in range(len(jobs)) for r0 in range(0, jobs[j][1].shape[0], stage_rows)]

    def copy(k):
        j, r0, nr = tiles[k]
        cols = jobs[j][1].shape[1]
        return pltpu.make_async_copy(jobs[j][0].at[pl.ds(r0, nr), :],
                                     stage.at[k % slots, pl.ds(0, nr), pl.ds(0, cols)],
                                     sems.at[k % slots])

    for k in range(min(slots - 1, len(tiles))):
        copy(k).start()
    for k, (j, r0, nr) in enumerate(tiles):
        _, dst, row_gain, scale, scaled_cols = jobs[j]
        if k + slots - 1 < len(tiles):
            copy(k + slots - 1).start()
        copy(k).wait()
        for c in range(0, dst.shape[1], LANES):
            tile = stage[k % slots, 0:nr, c:c + LANES]
            if row_gain is not None:
                tile = tile * row_gain[r0:r0 + nr, :]
            if c < scaled_cols:
                tile = tile * scale
            dst[r0:r0 + nr, c:c + LANES] = tile.astype(BF16)


def _inproj_kernel(x_ref, w_hbm, gain_ref, ca, sa, cr, sr,
                   a0_ref, a1_ref, a2_ref, rest_ref, y_scr, h_scr, tab_scr, rs_scr,
                   w_ref, w_stage, w_sems):
    tm = x_ref.shape[0]
    n_slabs = D_MODEL // LANES

    @pl.when(pl.program_id(0) == 0)
    def _():
        _load_weights_bf16([(w_hbm, w_ref, gain_ref, HEAD_DIM ** -0.5 * LOG2_E, ATT_WIDTH)],
                           w_stage, w_sems)

    x = x_ref[...]
    h_scr[0] = x.astype(BF16)
    rs_scr[0] = jnp.broadcast_to(lax.rsqrt(jnp.mean(x * x, axis=-1, keepdims=True) + NORM_EPS),
                                 (tm, LANES))
    for k in range(n_slabs):
        y_scr[k] = x[:, k * LANES:(k + 1) * LANES]
    for gi, d in enumerate(DILATIONS):
        if d == 1:
            continue
        n = tm // d
        for r in range(d):
            rs_scr[gi, r * n:(r + 1) * n, :] = rs_scr[0, pl.ds(r, n, stride=d), :]
            for k in range(n_slabs):
                h_scr[gi, r * n:(r + 1) * n, k * LANES:(k + 1) * LANES] = (
                    y_scr[k, pl.ds(r, n, stride=d), :].astype(BF16))

    for gi, d in enumerate(DILATIONS):
        if d == 1:
            continue
        n = tm // d
        for r in range(d):
            tab_scr[gi - 1, 0, r * n:(r + 1) * n, :] = ca[pl.ds(r, n, stride=d), :]
            tab_scr[gi - 1, 1, r * n:(r + 1) * n, :] = sa[pl.ds(r, n, stride=d), :]

    def att_tables(g):
        if g == 0:
            return ca[...], sa[...]
        return tab_scr[g - 1, 0], tab_scr[g - 1, 1]

    def write_att(g, slab, val):
        v = val.astype(BF16)
        d = DILATIONS[g]
        n = tm // d
        if g == 0:
            a0_ref[slab] = v
        else:
            out = a1_ref if g == 1 else a2_ref
            for r in range(d):
                out[slab, r] = v[r * n:(r + 1) * n, :]

    def att_tile(c):
        kind, g = divmod(c, 3)
        res = jnp.dot(h_scr[g], w_ref[:, c * COL_TILE:(c + 1) * COL_TILE],
                      preferred_element_type=F32)
        for k in range(PAIRS):
            s = res[:, k * LANES:(k + 1) * LANES] * rs_scr[g]
            if kind < 2:
                s = _rotate(s, *att_tables(g), span=HEAD_DIM // 2)
            write_att(g, kind * PAIRS + k, s)

    def rest_tile(c):
        col = 3 * ATT_WIDTH + c * COL_TILE
        res = jnp.dot(h_scr[0], w_ref[:, col:col + COL_TILE], preferred_element_type=F32)
        for k in range(COL_TILE // LANES):
            slab = c * (COL_TILE // LANES) + k
            s = res[:, k * LANES:(k + 1) * LANES] * rs_scr[0]
            if slab < REST_VR:
                s = _rotate(s, cr[...], sr[...], span=1)
                if slab >= REST_KR:
                    s = s * (RET_KEY_DIM ** -0.5)
            rest_ref[slab] = s.astype(BF16)

    for c in range(REST_SLABS * LANES // COL_TILE):
        rest_tile(c)
    for g in range(len(DILATIONS)):
        for kind in range(3):
            att_tile(kind * 3 + g)


def _in_projection(x2d, w, gain, tabs, batch, seq):
    t = x2d.shape[0]
    tm = TM_IN
    nt = seq // tm
    n_steps = t // tm
    tab_spec = pl.BlockSpec((tm, LANES), lambda i: (i % nt, 0))
    in_specs = [
        pl.BlockSpec((tm, D_MODEL), lambda i: (i, 0)),
        pl.BlockSpec(memory_space=pl.ANY),
        pl.BlockSpec((D_MODEL, LANES), lambda i: (0, 0), pipeline_mode=pl.Buffered(1)),
    ] + [tab_spec] * 4
    n_att = 3 * PAIRS
    d1, d2 = DILATIONS[1], DILATIONS[2]
    out_shape = [
        jax.ShapeDtypeStruct((n_att, t, LANES), BF16),
        jax.ShapeDtypeStruct((n_att, batch, d1, seq // d1, LANES), BF16),
        jax.ShapeDtypeStruct((n_att, batch, d2, seq // d2, LANES), BF16),
        jax.ShapeDtypeStruct((REST_SLABS, t, LANES), BF16),
    ]
    out_specs = [
        pl.BlockSpec((n_att, tm, LANES), lambda i: (0, i, 0)),
        pl.BlockSpec((n_att, None, d1, tm // d1, LANES), lambda i: (0, i // nt, 0, i % nt, 0)),
        pl.BlockSpec((n_att, None, d2, tm // d2, LANES), lambda i: (0, i // nt, 0, i % nt, 0)),
        pl.BlockSpec((REST_SLABS, tm, LANES), lambda i: (0, i, 0)),
    ]
    return pl.pallas_call(
        _inproj_kernel,
        grid=(n_steps,),
        in_specs=in_specs,
        out_specs=out_specs,
        out_shape=out_shape,
        scratch_shapes=[
            pltpu.VMEM((D_MODEL // LANES, tm, LANES), F32),
            pltpu.VMEM((len(DILATIONS), tm, D_MODEL), BF16),
            pltpu.VMEM((len(DILATIONS) - 1, 2, tm, LANES), F32),
            pltpu.VMEM((len(DILATIONS), tm, LANES), F32),
            pltpu.VMEM((D_MODEL, IN_WIDTH), BF16),
            pltpu.VMEM((W_STAGE_SLOTS, _stage_rows(IN_WIDTH), IN_WIDTH), F32),
            pltpu.SemaphoreType.DMA((W_STAGE_SLOTS,)),
        ],
        compiler_params=pltpu.CompilerParams(
            dimension_semantics=("arbitrary",), vmem_limit_bytes=V7X_VMEM_LIMIT_BYTES),
        name="in_projection",
    )(x2d, w, gain, *tabs)


def _pitch(d):
    return d + 1 if d % 8 == 0 else d


def _attn_scores(q, kk, bias, first_head):
    zero = jnp.zeros_like(q)
    q2 = jnp.concatenate([jnp.where(first_head, q, zero), jnp.where(first_head, zero, q)], axis=0)
    return lax.dot_general(q2, kk, (((1,), (1,)), ((), ())), preferred_element_type=F32) + bias


def _attn_partials(s, vv, first_head):
    nq = s.shape[0] // 2
    m = jnp.max(s, axis=1, keepdims=True)
    p = jnp.exp2(s - m)
    l = jnp.sum(p, axis=1, keepdims=True)
    pv = jnp.dot(p.astype(BF16), vv, preferred_element_type=F32)
    acc = jnp.where(first_head, pv[:nq], pv[nq:])
    den = jnp.where(first_head, l[:nq], l[nq:])
    mm = jnp.where(first_head, m[:nq], m[nq:])
    return acc, den, mm


def _attn_kernel(q0, k0, v0, q1, k1, v1, q2, k2, v2, *refs):
    n_w = len(TAIL_WEIGHT_GAINED)
    w_src, gain_ref, o_ref = refs[:n_w], refs[n_w], refs[n_w + 1]
    w_dst = refs[n_w + 2:2 * n_w + 2]
    acc0, den0, max0, acc1, den1, max1, acc2, den2, max2 = refs[2 * n_w + 2:]

    for src, dst, gained in zip(w_src, w_dst, TAIL_WEIGHT_GAINED):
        for c in range(0, src.shape[1], LANES):
            tile = src[:, c:c + LANES]
            if gained:
                tile = tile * gain_ref[...]
            dst[:, c:c + LANES] = tile.astype(BF16)

    blk = BAND
    seq = q0.shape[0]
    lane = lax.broadcasted_iota(jnp.int32, (1, LANES), 1)
    first_head = lane < HEAD_DIM
    qi = lax.broadcasted_iota(jnp.int32, (2 * blk, 2 * blk), 0) & (blk - 1)
    kj = lax.broadcasted_iota(jnp.int32, (2 * blk, 2 * blk), 1)
    band_bias = jnp.where((kj >= qi) & (kj <= qi + BAND), 0.0, MASK_VALUE).astype(F32)
    qi1 = lax.broadcasted_iota(jnp.int32, (2 * blk, blk), 0) & (blk - 1)
    kj1 = lax.broadcasted_iota(jnp.int32, (2 * blk, blk), 1)
    first_bias = jnp.where(kj1 <= qi1, 0.0, MASK_VALUE).astype(F32)

    def key_rows(n):
        if isinstance(n, int) and n == 0:
            return pl.ds(0, blk), first_bias
        st = n * blk if isinstance(n, int) else pl.multiple_of(n * blk, blk)
        return pl.ds(st - blk, 2 * blk), band_bias

    def query_rows(n):
        return pl.ds(n * blk if isinstance(n, int) else pl.multiple_of(n * blk, blk), blk)

    def run_blocks(blocks, emit):
        scores = []
        for i in range(len(blocks) + ATT_SKEW):
            if i < len(blocks):
                qr, kr, _, n = blocks[i]
                rows, bias = key_rows(n)
                scores.append(_attn_scores(qr[query_rows(n), :], kr[rows, :], bias, first_head))
            if i >= ATT_SKEW:
                j = i - ATT_SKEW
                _, _, vr, n = blocks[j]
                rows, _ = key_rows(n)
                emit(j, _attn_partials(scores[j], vr[rows, :], first_head))
                scores[j] = None

    def dilated_group(qr, kr, vr, outs, d):
        nb = seq // d // blk
        subs = max(1, ATT_BODY_BLOCKS // nb)

        def body(j, carry):
            blocks, where = [], []
            for ri in range(subs):
                r = j * subs + ri
                for n in range(nb):
                    blocks.append((qr.at[r], kr.at[r], vr.at[r], n))
                    where.append(pl.ds(n * (blk * _pitch(d)) + r, blk, stride=_pitch(d)))

            def emit(i, parts):
                for ref, val in zip(outs, parts):
                    ref[where[i], :] = val

            run_blocks(blocks, emit)
            return carry

        lax.fori_loop(0, d // subs, body, 0)

    dilated_group(q1, k1, v1, (acc1, den1, max1), DILATIONS[1])
    dilated_group(q2, k2, v2, (acc2, den2, max2), DILATIONS[2])

    def dense_body(j, carry):
        ns = [j * ATT_BODY_BLOCKS + i for i in range(ATT_BODY_BLOCKS)]

        def emit(i, parts):
            for ref, val in zip((acc0, den0, max0), parts):
                ref[query_rows(ns[i]), :] = val

        run_blocks([(q0, k0, v0, n) for n in ns], emit)
        return carry

    dense_body(0, 0)
    lax.fori_loop(1, seq // blk // ATT_BODY_BLOCKS, dense_body, 0)

    def merge(t, carry):
        rows = pl.ds(pl.multiple_of(t * MERGE_ROWS, MERGE_ROWS), MERGE_ROWS)

        def padded(ref, d):
            if _pitch(d) == d:
                return ref[rows, :]
            per = MERGE_ROWS // d
            return jnp.concatenate(
                [ref[pl.ds((t * per + i) * _pitch(d), d), :] for i in range(per)], axis=0)

        d2 = DILATIONS[2]
        m0, m1, m2 = max0[rows, :], max1[rows, :], padded(max2, d2)
        mx = jnp.maximum(m0, jnp.maximum(m1, m2))
        e0, e1, e2 = jnp.exp2(m0 - mx), jnp.exp2(m1 - mx), jnp.exp2(m2 - mx)
        num = e0 * acc0[rows, :] + e1 * acc1[rows, :] + e2 * padded(acc2, d2)
        den = e0 * den0[rows, :] + e1 * den1[rows, :] + e2 * padded(den2, d2)
        o_ref[rows, :] = (num * (1.0 / den)).astype(o_ref.dtype)
        return carry

    lax.fori_loop(0, seq // MERGE_ROWS, merge, 0, unroll=MERGE_UNROLL)


def _dilated_attention(a0, a1, a2, weights, gain, batch, seq):
    d1, d2 = DILATIONS[1], DILATIONS[2]
    n_steps = batch * PAIRS

    def step_rows(w):
        rows, cols = w.shape
        n_blocks = max(n for n in range(1, n_steps + 1) if rows % (n * BF16_SUBLANES) == 0)
        return pl.BlockSpec((rows // n_blocks, cols),
                            lambda b, p: (jnp.minimum(b * PAIRS + p, n_blocks - 1), 0))

    for w in weights:
        assert w.shape[0] % BF16_SUBLANES == 0 and w.shape[1] % LANES == 0, w.shape
    assert gain.shape[0] == weights[3].shape[0] == weights[4].shape[0]
    specs0 = [pl.BlockSpec((None, seq, LANES), functools.partial(lambda b, p, k: (k * PAIRS + p, b, 0), k=k))
              for k in range(3)]
    specs1 = [pl.BlockSpec((None, None, d1, seq // d1, LANES),
                           functools.partial(lambda b, p, k: (k * PAIRS + p, b, 0, 0, 0), k=k))
              for k in range(3)]
    specs2 = [pl.BlockSpec((None, None, d2, seq // d2, LANES),
                           functools.partial(lambda b, p, k: (k * PAIRS + p, b, 0, 0, 0), k=k))
              for k in range(3)]
    in_specs = specs0 + specs1 + specs2 + [step_rows(w) for w in weights] + [step_rows(gain)]
    outs = pl.pallas_call(
        _attn_kernel,
        grid=(batch, PAIRS),
        in_specs=in_specs,
        out_specs=[pl.BlockSpec((None, seq, LANES), lambda b, p: (p, b, 0))]
        + [step_rows(w) for w in weights],
        out_shape=[jax.ShapeDtypeStruct((PAIRS, batch * seq, LANES), BF16)]
        + [jax.ShapeDtypeStruct(w.shape, BF16) for w in weights],
        scratch_shapes=[pltpu.VMEM((seq // d * _pitch(d), LANES), F32) for d in DILATIONS for _ in range(3)],
        compiler_params=pltpu.CompilerParams(
            dimension_semantics=("arbitrary", "arbitrary"), vmem_limit_bytes=V7X_VMEM_LIMIT_BYTES),
        name="dilated_attention",
    )(a0, a0, a0, a1, a1, a1, a2, a2, a2, *weights, gain)
    return outs[0], outs[1:]


def _retention_kernel(q_ref, k_ref, v_ref, dec_ref, zeta_ref, xi_ref, cd_ref, o_ref, state_scr):
    c_len = RET_CHUNK
    n_chunks = q_ref.shape[0] // c_len
    dec = dec_ref[...]
    zeta = zeta_ref[...]
    xi = xi_ref[...]
    xi2 = jnp.concatenate([xi, xi], axis=1)
    cd = cd_ref[...]
    cd2 = jnp.concatenate([cd, cd], axis=1)
    chunk_rows = [pl.ds(c * c_len, c_len) for c in range(n_chunks)]

    def values(rows):
        return jnp.concatenate([v_ref[0, rows, :], v_ref[1, rows, :]], axis=1)

    scores = [(lax.dot_general(q_ref[rows, :], k_ref[rows, :], (((1,), (1,)), ((), ())),
                               preferred_element_type=F32) * dec).astype(BF16)
              for rows in chunk_rows]

    state = jnp.zeros((RET_KEY_DIM, RET_VALUE_DIM), F32)
    for c, rows in enumerate(chunk_rows):
        state_scr[c] = state.astype(BF16)
        if c + 1 < n_chunks:
            kz = (k_ref[rows, :].astype(F32) * zeta).astype(BF16)
            kv = lax.dot_general(kz, values(rows), (((0,), (0,)), ((), ())),
                                 preferred_element_type=F32)
            state = state * cd2 + kv

    for c, rows in enumerate(chunk_rows):
        lhs = jnp.concatenate([scores[c], q_ref[rows, :]], axis=1)
        rhs = jnp.concatenate([values(rows), state_scr[c]], axis=0)
        o = jnp.dot(lhs, rhs, preferred_element_type=F32) * xi2
        o_ref[0, rows, :] = o[:, :LANES].astype(o_ref.dtype)
        o_ref[1, rows, :] = o[:, LANES:].astype(o_ref.dtype)


def _retention(rest, tabs, batch, seq):
    dec, zeta, xi, cd = tabs
    vs = RET_VALUE_DIM // LANES
    tab_spec = pl.BlockSpec((None, RET_CHUNK, LANES), lambda b, h: (h, 0, 0))
    in_specs = [
        pl.BlockSpec((None, seq, LANES), lambda b, h: (REST_QR + h, b, 0)),
        pl.BlockSpec((None, seq, LANES), lambda b, h: (REST_KR + h, b, 0)),
        pl.BlockSpec((vs, seq, LANES), lambda b, h: (REST_VR // vs + h, b, 0)),
        tab_spec, tab_spec, tab_spec,
        pl.BlockSpec((None, 1, LANES), lambda b, h: (h, 0, 0)),
    ]
    return pl.pallas_call(
        _retention_kernel,
        grid=(batch, RET_HEADS),
        in_specs=in_specs,
        out_specs=pl.BlockSpec((vs, seq, LANES), lambda b, h: (h, b, 0)),
        out_shape=jax.ShapeDtypeStruct((RET_V_WIDTH // LANES, batch * seq, LANES), BF16),
        scratch_shapes=[pltpu.VMEM((seq // RET_CHUNK, RET_KEY_DIM, RET_VALUE_DIM), BF16)],
        compiler_params=pltpu.CompilerParams(
            dimension_semantics=("arbitrary", "arbitrary"), vmem_limit_bytes=V7X_VMEM_LIMIT_BYTES),
        name="retention",
    )(rest, rest, rest, dec, zeta, xi, cd)


def _slabs(ref, lo=0, hi=None):
    hi = ref.shape[0] if hi is None else hi
    return jnp.concatenate([ref[k] for k in range(lo, hi)], axis=1)


def _tail_kernel(oa_ref, ret_ref, sg_ref, ga_ref, gr_ref, x_ref, wa_ref, wr_ref, wo_ref,
                 wg_ref, wu_ref, wd_ref, gf_ref, out_ref, ret_scr, act_scr):
    vs = RET_VALUE_DIM // LANES
    for h in range(RET_HEADS):
        o = _slabs(ret_ref, h * vs, (h + 1) * vs).astype(F32)
        oc = o - jnp.mean(o, axis=-1, keepdims=True)
        var = jnp.mean(oc * oc, axis=-1, keepdims=True)
        gate = _slabs(sg_ref, h * vs, (h + 1) * vs).astype(F32)
        gated = oc * lax.rsqrt(var + NORM_EPS) * (gate * jax.nn.sigmoid(gate))
        ret_scr[:, h * RET_VALUE_DIM:(h + 1) * RET_VALUE_DIM] = gated.astype(BF16)
    ya = jnp.dot(_slabs(oa_ref), wa_ref[...], preferred_element_type=F32)
    yr = jnp.dot(ret_scr[...], wr_ref[...], preferred_element_type=F32)
    merged = (jax.nn.sigmoid(_slabs(ga_ref).astype(F32)) * ya
              + jax.nn.sigmoid(_slabs(gr_ref).astype(F32)) * yr)
    x1 = x_ref[...] + jnp.dot(merged.astype(BF16), wo_ref[...], preferred_element_type=F32)
    scale = jnp.broadcast_to(lax.rsqrt(jnp.mean(x1 * x1, axis=-1, keepdims=True) + NORM_EPS),
                             (x1.shape[0], FFN_CHUNK))
    h2 = x1.astype(BF16)
    for c in range(0, FFN_HIDDEN, FFN_CHUNK):
        gate = jnp.dot(h2, wg_ref[:, c:c + FFN_CHUNK], preferred_element_type=F32) * scale
        up = jnp.dot(h2, wu_ref[:, c:c + FFN_CHUNK], preferred_element_type=F32) * scale
        act_scr[:, c:c + FFN_CHUNK] = (gate * jax.nn.sigmoid(gate) * up).astype(BF16)
    down = jnp.dot(act_scr[...], wd_ref[...], preferred_element_type=F32)
    out_ref[...] = _rms(x1 + down, gf_ref[...])


def _tail(oa, ret, rest, x2d, wa, wr, wo, wg, wu, wd, gf):
    t = x2d.shape[0]
    tm = TM_OUT
    gs = D_MODEL // LANES
    full = lambda shape: pl.BlockSpec(shape, lambda i: (0,) * len(shape), pipeline_mode=pl.Buffered(1))
    vec = pl.BlockSpec((1, D_MODEL), lambda i: (0, 0))
    row = pl.BlockSpec((tm, D_MODEL), lambda i: (i, 0))
    in_specs = [
        pl.BlockSpec((PAIRS, tm, LANES), lambda i: (0, i, 0)),
        pl.BlockSpec((gs, tm, LANES), lambda i: (0, i, 0)),
        pl.BlockSpec((gs, tm, LANES), lambda i: (REST_GR // gs, i, 0)),
        pl.BlockSpec((gs, tm, LANES), lambda i: (REST_GA // gs, i, 0)),
        pl.BlockSpec((gs, tm, LANES), lambda i: (REST_GRT // gs, i, 0)),
        row,
        full((GROUP_WIDTH, D_MODEL)), full((RET_V_WIDTH, D_MODEL)), full((D_MODEL, D_MODEL)),
        full((D_MODEL, FFN_HIDDEN)), full((D_MODEL, FFN_HIDDEN)), full((FFN_HIDDEN, D_MODEL)), vec,
    ]
    return pl.pallas_call(
        _tail_kernel,
        grid=(t // tm,),
        in_specs=in_specs,
        out_specs=row,
        out_shape=jax.ShapeDtypeStruct((t, D_MODEL), F32),
        scratch_shapes=[pltpu.VMEM((tm, RET_V_WIDTH), BF16), pltpu.VMEM((tm, FFN_HIDDEN), BF16)],
        compiler_params=pltpu.CompilerParams(
            dimension_semantics=("arbitrary",), vmem_limit_bytes=V7X_VMEM_LIMIT_BYTES),
        name="merge_ffn",
    )(oa, ret, rest, rest, rest, x2d, wa, wr, wo, wg, wu, wd, gf)


def _row_gain(g):
    return jnp.broadcast_to(g[:, None], (g.shape[0], LANES))


def _position_tables(seq):
    pos = np.arange(seq, dtype=np.float64)
    inv = ROPE_THETA ** (-np.arange(0, HEAD_DIM, 2, dtype=np.float64) / HEAD_DIM)
    ang = pos[:, None] * inv[None, :]
    c, s = np.cos(ang), np.sin(ang)
    cos_a = np.concatenate([c, c, c, c], axis=1)
    sin_a = np.concatenate([-s, s, -s, s], axis=1)
    base = 1.0 / (ROPE_THETA ** np.linspace(0.0, 1.0, RET_KEY_DIM // 2, dtype=np.float64))
    ang_r = pos[:, None] * base[None, :]
    cr, sr = np.cos(ang_r), np.sin(ang_r)
    cos_r = np.repeat(cr, 2, axis=1)
    sin_r = np.repeat(sr, 2, axis=1) * np.tile([-1.0, 1.0], RET_KEY_DIM // 2)[None, :]
    return [jnp.asarray(t.astype(np.float32)) for t in (cos_a, sin_a, cos_r, sin_r)]


def _retention_tables():
    c = RET_CHUNK
    log_g = np.log1p(-(2.0 ** (-5.0 - np.arange(RET_HEADS, dtype=np.float64))))
    idx = np.arange(c, dtype=np.float64)
    diff = idx[:, None] - idx[None, :]
    decay = np.where(diff[None] >= 0, np.exp(-(idx + 1.0)[None, None, :] * log_g[:, None, None]), 0.0)
    zeta = np.exp((c - 1 - idx)[None, :] * log_g[:, None])
    xi = np.exp((idx + 1.0)[None, :] * log_g[:, None])
    chunk_decay = np.exp(c * log_g)
    bcast = lambda v: np.broadcast_to(v[:, :, None], (RET_HEADS, c, LANES))
    cd = np.broadcast_to(chunk_decay[:, None, None], (RET_HEADS, 1, LANES))
    return tuple(jnp.asarray(np.ascontiguousarray(t).astype(np.float32))
                 for t in (decay, bcast(zeta), bcast(xi), cd))


def kernel(x, norm_mix_g, w_in, w_out_attn, w_out_ret, w_out, norm_ffn_g, w_ffn_gate, w_ffn_up,
           w_ffn_down, norm_final_g):
    batch, seq, _ = x.shape
    assert w_in.shape[0] == 1, "single-layer problem"
    assert seq % TM_IN == 0 and TM_IN % (DILATIONS[-1] * BF16_SUBLANES) == 0
    x2d = x.reshape(batch * seq, D_MODEL)
    a0, a1, a2, rest = _in_projection(
        x2d, w_in[0], _row_gain(norm_mix_g[0]), _position_tables(seq), batch, seq)
    tail_weights = [w_out_attn[0], w_out_ret[0], w_out[0], w_ffn_gate[0], w_ffn_up[0], w_ffn_down[0]]
    oa, (wa, wr, wo, wg, wu, wd) = _dilated_attention(
        a0, a1, a2, tail_weights, _row_gain(norm_ffn_g[0]), batch, seq)
    ret = _retention(rest, _retention_tables(), batch, seq)
    out = _tail(oa, ret, rest, x2d, wa, wr, wo, wg, wu, wd, norm_final_g[None, :])
    return out.reshape(batch, seq, D_MODEL)
```

```python
import functools

import jax
import jax.numpy as jnp
import numpy as np
from jax import lax
from jax.experimental import pallas as pl
from jax.experimental.pallas import tpu as pltpu

F32 = jnp.float32
BF16 = jnp.bfloat16

LANES = 128
BF16_SUBLANES = 16
V7X_VMEM_LIMIT_BYTES = 60 * 1024 * 1024

D_MODEL = 1024
HEAD_DIM = 64
HEADS_PER_GROUP = 8
DILATIONS = (1, 4, 16)
BAND = 128
GROUP_WIDTH = HEADS_PER_GROUP * HEAD_DIM
ATT_WIDTH = len(DILATIONS) * GROUP_WIDTH
PAIRS = GROUP_WIDTH // LANES
ROPE_THETA = 10000.0

RET_HEADS = 4
RET_KEY_DIM = 128
RET_VALUE_DIM = 256
RET_QK_WIDTH = RET_HEADS * RET_KEY_DIM
RET_V_WIDTH = RET_HEADS * RET_VALUE_DIM
RET_CHUNK = 128

FFN_HIDDEN = 2816
NORM_EPS = 1e-6
MASK_VALUE = -1e30

IN_WIDTH = 3 * ATT_WIDTH + 2 * RET_QK_WIDTH + 2 * RET_V_WIDTH + 2 * D_MODEL
COL_TILE = 512
ATT_TILES = 3 * ATT_WIDTH // COL_TILE
REST_SLABS = (IN_WIDTH - 3 * ATT_WIDTH) // LANES
REST_QR, REST_KR, REST_VR, REST_GR, REST_GA, REST_GRT = 0, 4, 8, 16, 24, 32

TM_IN = 512
TM_OUT = 512
FFN_CHUNK = 256
ATT_BODY_BLOCKS = 32
MERGE_ROWS = 64
MERGE_UNROLL = 8
ATT_SKEW = 2
LOG2_E = 1.4426950408889634
TAIL_WEIGHT_GAINED = (False, False, False, True, True, False)


def _rms(x, g):
    ms = jnp.mean(x * x, axis=-1, keepdims=True)
    return x * lax.rsqrt(ms + NORM_EPS) * g


def _rotate(t, cos, sin_signed, span):
    lane = lax.broadcasted_iota(jnp.int32, (1, LANES), 1)
    low = (lane % (2 * span)) < span
    partner = jnp.where(low, pltpu.roll(t, LANES - span, 1), pltpu.roll(t, span, 1))
    return t * cos + partner * sin_signed


W_STAGE_BYTES = 5 << 17
W_STAGE_SLOTS = 8


def _stage_rows(cols):
    return W_STAGE_BYTES // (4 * cols) // BF16_SUBLANES * BF16_SUBLANES


def _load_weights_bf16(jobs, stage, sems):
    slots, stage_rows, _ = stage.shape
    tiles = [(j, r0, min(stage_rows, jobs[j][1].shape[0] - r0))
             for j in range(len(jobs)) for r0 in range(0, jobs[j][1].shape[0], stage_rows)]

    def copy(k):
        j, r0, nr = tiles[k]
        cols = jobs[j][1].shape[1]
        return pltpu.make_async_copy(jobs[j][0].at[pl.ds(r0, nr), :],
                                     stage.at[k % slots, pl.ds(0, nr), pl.ds(0, cols)],
                                     sems.at[k % slots])

    for k in range(min(slots - 1, len(tiles))):
        copy(k).start()
    for k, (j, r0, nr) in enumerate(tiles):
        _, dst, row_gain, scale, scaled_cols = jobs[j]
        if k + slots - 1 < len(tiles):
            copy(k + slots - 1).start()
        copy(k).wait()
        for c in range(0, dst.shape[1], LANES):
            tile = stage[k % slots, 0:nr, c:c + LANES]
            if row_gain is not None:
                tile = tile * row_gain[r0:r0 + nr, :]
            if c < scaled_cols:
                tile = tile * scale
            dst[r0:r0 + nr, c:c + LANES] = tile.astype(BF16)


def _inproj_kernel(x_ref, w_hbm, gain_ref, ca, sa, cr, sr,
                   a0_ref, a1_ref, a2_ref, rest_ref, y_scr, h_scr, tab_scr, rs_scr,
                   w_ref, w_stage, w_sems):
    tm = x_ref.shape[0]
    n_slabs = D_MODEL // LANES

    @pl.when(pl.program_id(0) == 0)
    def _():
        _load_weights_bf16([(w_hbm, w_ref, gain_ref, HEAD_DIM ** -0.5 * LOG2_E, ATT_WIDTH)],
                           w_stage, w_sems)

    x = x_ref[...]
    h_scr[0] = x.astype(BF16)
    rs_scr[0] = jnp.broadcast_to(lax.rsqrt(jnp.mean(x * x, axis=-1, keepdims=True) + NORM_EPS),
                                 (tm, LANES))
    for k in range(n_slabs):
        y_scr[k] = x[:, k * LANES:(k + 1) * LANES]
    for gi, d in enumerate(DILATIONS):
        if d == 1:
            continue
        n = tm // d
        for r in range(d):
            rs_scr[gi, r * n:(r + 1) * n, :] = rs_scr[0, pl.ds(r, n, stride=d), :]
            for k in range(n_slabs):
                h_scr[gi, r * n:(r + 1) * n, k * LANES:(k + 1) * LANES] = (
                    y_scr[k, pl.ds(r, n, stride=d), :].astype(BF16))

    for gi, d in enumerate(DILATIONS):
        if d == 1:
            continue
        n = tm // d
        for r in range(d):
            tab_scr[gi - 1, 0, r * n:(r + 1) * n, :] = ca[pl.ds(r, n, stride=d), :]
            tab_scr[gi - 1, 1, r * n:(r + 1) * n, :] = sa[pl.ds(r, n, stride=d), :]

    def att_tables(g):
        if g == 0:
            return ca[...], sa[...]
        return tab_scr[g - 1, 0], tab_scr[g - 1, 1]

    def write_att(g, slab, val):
        v = val.astype(BF16)
        d = DILATIONS[g]
        n = tm // d
        if g == 0:
            a0_ref[slab] = v
        else:
            out = a1_ref if g == 1 else a2_ref
            for r in range(d):
                out[slab, r] = v[r * n:(r + 1) * n, :]

    def att_tile(c):
        kind, g = divmod(c, 3)
        res = jnp.dot(h_scr[g], w_ref[:, c * COL_TILE:(c + 1) * COL_TILE],
                      preferred_element_type=F32)
        for k in range(PAIRS):
            s = res[:, k * LANES:(k + 1) * LANES] * rs_scr[g]
            if kind < 2:
                s = _rotate(s, *att_tables(g), span=HEAD_DIM // 2)
            write_att(g, kind * PAIRS + k, s)

    def rest_tile(c):
        col = 3 * ATT_WIDTH + c * COL_TILE
        res = jnp.dot(h_scr[0], w_ref[:, col:col + COL_TILE], preferred_element_type=F32)
        for k in range(COL_TILE // LANES):
            slab = c * (COL_TILE // LANES) + k
            s = res[:, k * LANES:(k + 1) * LANES] * rs_scr[0]
            if slab < REST_VR:
                s = _rotate(s, cr[...], sr[...], span=1)
                if slab >= REST_KR:
                    s = s * (RET_KEY_DIM ** -0.5)
            rest_ref[slab] = s.astype(BF16)

    for c in range(REST_SLABS * LANES // COL_TILE):
        rest_tile(c)
    for g in range(len(DILATIONS)):
        for kind in range(3):
            att_tile(kind * 3 + g)


def _in_projection(x2d, w, gain, tabs, batch, seq):
    t = x2d.shape[0]
    tm = TM_IN
    nt = seq // tm
    n_steps = t // tm
    tab_spec = pl.BlockSpec((tm, LANES), lambda i: (i % nt, 0))
    in_specs = [
        pl.BlockSpec((tm, D_MODEL), lambda i: (i, 0)),
        pl.BlockSpec(memory_space=pl.ANY),
        pl.BlockSpec((D_MODEL, LANES), lambda i: (0, 0), pipeline_mode=pl.Buffered(1)),
    ] + [tab_spec] * 4
    n_att = 3 * PAIRS
    d1, d2 = DILATIONS[1], DILATIONS[2]
    out_shape = [
        jax.ShapeDtypeStruct((n_att, t, LANES), BF16),
        jax.ShapeDtypeStruct((n_att, batch, d1, seq // d1, LANES), BF16),
        jax.ShapeDtypeStruct((n_att, batch, d2, seq // d2, LANES), BF16),
        jax.ShapeDtypeStruct((REST_SLABS, t, LANES), BF16),
    ]
    out_specs = [
        pl.BlockSpec((n_att, tm, LANES), lambda i: (0, i, 0)),
        pl.BlockSpec((n_att, None, d1, tm // d1, LANES), lambda i: (0, i // nt, 0, i % nt, 0)),
        pl.BlockSpec((n_att, None, d2, tm // d2, LANES), lambda i: (0, i // nt, 0, i % nt, 0)),
        pl.BlockSpec((REST_SLABS, tm, LANES), lambda i: (0, i, 0)),
    ]
    return pl.pallas_call(
        _inproj_kernel,
        grid=(n_steps,),
        in_specs=in_specs,
        out_specs=out_specs,
        out_shape=out_shape,
        scratch_shapes=[
            pltpu.VMEM((D_MODEL // LANES, tm, LANES), F32),
            pltpu.VMEM((len(DILATIONS), tm, D_MODEL), BF16),
            pltpu.VMEM((len(DILATIONS) - 1, 2, tm, LANES), F32),
            pltpu.VMEM((len(DILATIONS), tm, LANES), F32),
            pltpu.VMEM((D_MODEL, IN_WIDTH), BF16),
            pltpu.VMEM((W_STAGE_SLOTS, _stage_rows(IN_WIDTH), IN_WIDTH), F32),
            pltpu.SemaphoreType.DMA((W_STAGE_SLOTS,)),
        ],
        compiler_params=pltpu.CompilerParams(
            dimension_semantics=("arbitrary",), vmem_limit_bytes=V7X_VMEM_LIMIT_BYTES),
        name="in_projection",
    )(x2d, w, gain, *tabs)


def _pitch(d):
    return d + 1 if d % 8 == 0 else d


def _attn_scores(q, kk, bias, first_head):
    zero = jnp.zeros_like(q)
    q2 = jnp.concatenate([jnp.where(first_head, q, zero), jnp.where(first_head, zero, q)], axis=0)
    return lax.dot_general(q2, kk, (((1,), (1,)), ((), ())), preferred_element_type=F32) + bias


def _attn_partials(s, vv, first_head):
    nq = s.shape[0] // 2
    m = jnp.max(s, axis=1, keepdims=True)
    p = jnp.exp2(s - m)
    l = jnp.sum(p, axis=1, keepdims=True)
    pv = jnp.dot(p.astype(BF16), vv, preferred_element_type=F32)
    acc = jnp.where(first_head, pv[:nq], pv[nq:])
    den = jnp.where(first_head, l[:nq], l[nq:])
    mm = jnp.where(first_head, m[:nq], m[nq:])
    return acc, den, mm


def _attn_kernel(q0, k0, v0, q1, k1, v1, q2, k2, v2, *refs):
    n_w = len(TAIL_WEIGHT_GAINED)
    w_src, gain_ref, o_ref = refs[:n_w], refs[n_w], refs[n_w + 1]
    w_dst = refs[n_w + 2:2 * n_w + 2]
    acc0, den0, max0, acc1, den1, max1, acc2, den2, max2 = refs[2 * n_w + 2:]

    for src, dst, gained in zip(w_src, w_dst, TAIL_WEIGHT_GAINED):
        for c in range(0, src.shape[1], LANES):
            tile = src[:, c:c + LANES]
            if gained:
                tile = tile * gain_ref[...]
            dst[:, c:c + LANES] = tile.astype(BF16)

    blk = BAND
    seq = q0.shape[0]
    lane = lax.broadcasted_iota(jnp.int32, (1, LANES), 1)
    first_head = lane < HEAD_DIM
    qi = lax.broadcasted_iota(jnp.int32, (2 * blk, 2 * blk), 0) & (blk - 1)
    kj = lax.broadcasted_iota(jnp.int32, (2 * blk, 2 * blk), 1)
    band_bias = jnp.where((kj >= qi) & (kj <= qi + BAND), 0.0, MASK_VALUE).astype(F32)
    qi1 = lax.broadcasted_iota(jnp.int32, (2 * blk, blk), 0) & (blk - 1)
    kj1 = lax.broadcasted_iota(jnp.int32, (2 * blk, blk), 1)
    first_bias = jnp.where(kj1 <= qi1, 0.0, MASK_VALUE).astype(F32)

    def key_rows(n):
        if isinstance(n, int) and n == 0:
            return pl.ds(0, blk), first_bias
        st = n * blk if isinstance(n, int) else pl.multiple_of(n * blk, blk)
        return pl.ds(st - blk, 2 * blk), band_bias

    def query_rows(n):
        return pl.ds(n * blk if isinstance(n, int) else pl.multiple_of(n * blk, blk), blk)

    def run_blocks(blocks, emit):
        scores = []
        for i in range(len(blocks) + ATT_SKEW):
            if i < len(blocks):
                qr, kr, _, n = blocks[i]
                rows, bias = key_rows(n)
                scores.append(_attn_scores(qr[query_rows(n), :], kr[rows, :], bias, first_head))
            if i >= ATT_SKEW:
                j = i - ATT_SKEW
                _, _, vr, n = blocks[j]
                rows, _ = key_rows(n)
                emit(j, _attn_partials(scores[j], vr[rows, :], first_head))
                scores[j] = None

    def dilated_group(qr, kr, vr, outs, d):
        nb = seq // d // blk
        subs = max(1, ATT_BODY_BLOCKS // nb)

        def body(j, carry):
            blocks, where = [], []
            for ri in range(subs):
                r = j * subs + ri
                for n in range(nb):
                    blocks.append((qr.at[r], kr.at[r], vr.at[r], n))
                    where.append(pl.ds(n * (blk * _pitch(d)) + r, blk, stride=_pitch(d)))

            def emit(i, parts):
                for ref, val in zip(outs, parts):
                    ref[where[i], :] = val

            run_blocks(blocks, emit)
            return carry

        lax.fori_loop(0, d // subs, body, 0)

    dilated_group(q1, k1, v1, (acc1, den1, max1), DILATIONS[1])
    dilated_group(q2, k2, v2, (acc2, den2, max2), DILATIONS[2])

    def dense_body(j, carry):
        ns = [j * ATT_BODY_BLOCKS + i for i in range(ATT_BODY_BLOCKS)]

        def emit(i, parts):
            for ref, val in zip((acc0, den0, max0), parts):
                ref[query_rows(ns[i]), :] = val

        run_blocks([(q0, k0, v0, n) for n in ns], emit)
        return carry

    dense_body(0, 0)
    lax.fori_loop(1, seq // blk // ATT_BODY_BLOCKS, dense_body, 0)

    def merge(t, carry):
        rows = pl.ds(pl.multiple_of(t * MERGE_ROWS, MERGE_ROWS), MERGE_ROWS)

        def padded(ref, d):
            if _pitch(d) == d:
                return ref[rows, :]
            per = MERGE_ROWS // d
            return jnp.concatenate(
                [ref[pl.ds((t * per + i) * _pitch(d), d), :] for i in range(per)], axis=0)

        d2 = DILATIONS[2]
        m0, m1, m2 = max0[rows, :], max1[rows, :], padded(max2, d2)
        mx = jnp.maximum(m0, jnp.maximum(m1, m2))
        e0, e1, e2 = jnp.exp2(m0 - mx), jnp.exp2(m1 - mx), jnp.exp2(m2 - mx)
        num = e0 * acc0[rows, :] + e1 * acc1[rows, :] + e2 * padded(acc2, d2)
        den = e0 * den0[rows, :] + e1 * den1[rows, :] + e2 * padded(den2, d2)
        o_ref[rows, :] = (num * (1.0 / den)).astype(o_ref.dtype)
        return carry

    lax.fori_loop(0, seq // MERGE_ROWS, merge, 0, unroll=MERGE_UNROLL)


def _dilated_attention(a0, a1, a2, weights, gain, batch, seq):
    d1, d2 = DILATIONS[1], DILATIONS[2]
    n_steps = batch * PAIRS

    def step_rows(w):
        rows, cols = w.shape
        n_blocks = max(n for n in range(1, n_steps + 1) if rows % (n * BF16_SUBLANES) == 0)
        return pl.BlockSpec((rows // n_blocks, cols),
                            lambda b, p: (jnp.minimum(b * PAIRS + p, n_blocks - 1), 0))

    for w in weights:
        assert w.shape[0] % BF16_SUBLANES == 0 and w.shape[1] % LANES == 0, w.shape
    assert gain.shape[0] == weights[3].shape[0] == weights[4].shape[0]
    specs0 = [pl.BlockSpec((None, seq, LANES), functools.partial(lambda b, p, k: (k * PAIRS + p, b, 0), k=k))
              for k in range(3)]
    specs1 = [pl.BlockSpec((None, None, d1, seq // d1, LANES),
                           functools.partial(lambda b, p, k: (k * PAIRS + p, b, 0, 0, 0), k=k))
              for k in range(3)]
    specs2 = [pl.BlockSpec((None, None, d2, seq // d2, LANES),
                           functools.partial(lambda b, p, k: (k * PAIRS + p, b, 0, 0, 0), k=k))
              for k in range(3)]
    in_specs = specs0 + specs1 + specs2 + [step_rows(w) for w in weights] + [step_rows(gain)]
    outs = pl.pallas_call(
        _attn_kernel,
        grid=(batch, PAIRS),
        in_specs=in_specs,
        out_specs=[pl.BlockSpec((None, seq, LANES), lambda b, p: (p, b, 0))]
        + [step_rows(w) for w in weights],
        out_shape=[jax.ShapeDtypeStruct((PAIRS, batch * seq, LANES), BF16)]
        + [jax.ShapeDtypeStruct(w.shape, BF16) for w in weights],
        scratch_shapes=[pltpu.VMEM((seq // d * _pitch(d), LANES), F32) for d in DILATIONS for _ in range(3)],
        compiler_params=pltpu.CompilerParams(
            dimension_semantics=("arbitrary", "arbitrary"), vmem_limit_bytes=V7X_VMEM_LIMIT_BYTES),
        name="dilated_attention",
    )(a0, a0, a0, a1, a1, a1, a2, a2, a2, *weights, gain)
    return outs[0], outs[1:]


def _retention_kernel(q_ref, k_ref, v_ref, dec_ref, zeta_ref, xi_ref, cd_ref, o_ref, state_scr):
    c_len = RET_CHUNK
    n_chunks = q_ref.shape[0] // c_len
    dec = dec_ref[...]
    zeta = zeta_ref[...]
    xi = xi_ref[...]
    xi2 = jnp.concatenate([xi, xi], axis=1)
    cd = cd_ref[...]
    cd2 = jnp.concatenate([cd, cd], axis=1)
    chunk_rows = [pl.ds(c * c_len, c_len) for c in range(n_chunks)]

    def values(rows):
        return jnp.concatenate([v_ref[0, rows, :], v_ref[1, rows, :]], axis=1)

    scores = []
    state = jnp.zeros((RET_KEY_DIM, RET_VALUE_DIM), F32)
    for c, rows in enumerate(chunk_rows):
        scores.append((lax.dot_general(q_ref[rows, :], k_ref[rows, :], (((1,), (1,)), ((), ())),
                                       preferred_element_type=F32) * dec).astype(BF16))
        state_scr[c] = state.astype(BF16)
        if c + 1 < n_chunks:
            kz = (k_ref[rows, :].astype(F32) * zeta).astype(BF16)
            kv = lax.dot_general(kz, values(rows), (((0,), (0,)), ((), ())),
                                 preferred_element_type=F32)
            state = state * cd2 + kv

    for c, rows in enumerate(chunk_rows):
        lhs = jnp.concatenate([scores[c], q_ref[rows, :]], axis=1)
        rhs = jnp.concatenate([values(rows), state_scr[c]], axis=0)
        o = jnp.dot(lhs, rhs, preferred_element_type=F32) * xi2
        o_ref[0, rows, :] = o[:, :LANES].astype(o_ref.dtype)
        o_ref[1, rows, :] = o[:, LANES:].astype(o_ref.dtype)


def _retention(rest, tabs, batch, seq):
    dec, zeta, xi, cd = tabs
    vs = RET_VALUE_DIM // LANES
    tab_spec = pl.BlockSpec((None, RET_CHUNK, LANES), lambda b, h: (h, 0, 0))
    in_specs = [
        pl.BlockSpec((None, seq, LANES), lambda b, h: (REST_QR + h, b, 0)),
        pl.BlockSpec((None, seq, LANES), lambda b, h: (REST_KR + h, b, 0)),
        pl.BlockSpec((vs, seq, LANES), lambda b, h: (REST_VR // vs + h, b, 0)),
        tab_spec, tab_spec, tab_spec,
        pl.BlockSpec((None, 1, LANES), lambda b, h: (h, 0, 0)),
    ]
    return pl.pallas_call(
        _retention_kernel,
        grid=(batch, RET_HEADS),
        in_specs=in_specs,
        out_specs=pl.BlockSpec((vs, seq, LANES), lambda b, h: (h, b, 0)),
        out_shape=jax.ShapeDtypeStruct((RET_V_WIDTH // LANES, batch * seq, LANES), BF16),
        scratch_shapes=[pltpu.VMEM((seq // RET_CHUNK, RET_KEY_DIM, RET_VALUE_DIM), BF16)],
        compiler_params=pltpu.CompilerParams(
            dimension_semantics=("arbitrary", "arbitrary"), vmem_limit_bytes=V7X_VMEM_LIMIT_BYTES),
        name="retention",
    )(rest, rest, rest, dec, zeta, xi, cd)


def _slabs(ref, lo=0, hi=None):
    hi = ref.shape[0] if hi is None else hi
    return jnp.concatenate([ref[k] for k in range(lo, hi)], axis=1)


def _tail_kernel(oa_ref, ret_ref, sg_ref, ga_ref, gr_ref, x_ref, wa_ref, wr_ref, wo_ref,
                 wg_ref, wu_ref, wd_ref, gf_ref, out_ref, ret_scr, act_scr):
    vs = RET_VALUE_DIM // LANES
    for h in range(RET_HEADS):
        o = _slabs(ret_ref, h * vs, (h + 1) * vs).astype(F32)
        oc = o - jnp.mean(o, axis=-1, keepdims=True)
        var = jnp.mean(oc * oc, axis=-1, keepdims=True)
        gate = _slabs(sg_ref, h * vs, (h + 1) * vs).astype(F32)
        gated = oc * lax.rsqrt(var + NORM_EPS) * (gate * jax.nn.sigmoid(gate))
        ret_scr[:, h * RET_VALUE_DIM:(h + 1) * RET_VALUE_DIM] = gated.astype(BF16)
    ya = jnp.dot(_slabs(oa_ref), wa_ref[...], preferred_element_type=F32)
    yr = jnp.dot(ret_scr[...], wr_ref[...], preferred_element_type=F32)
    merged = (jax.nn.sigmoid(_slabs(ga_ref).astype(F32)) * ya
              + jax.nn.sigmoid(_slabs(gr_ref).astype(F32)) * yr)
    x1 = x_ref[...] + jnp.dot(merged.astype(BF16), wo_ref[...], preferred_element_type=F32)
    scale = jnp.broadcast_to(lax.rsqrt(jnp.mean(x1 * x1, axis=-1, keepdims=True) + NORM_EPS),
                             (x1.shape[0], FFN_CHUNK))
    h2 = x1.astype(BF16)
    for c in range(0, FFN_HIDDEN, FFN_CHUNK):
        gate = jnp.dot(h2, wg_ref[:, c:c + FFN_CHUNK], preferred_element_type=F32) * scale
        up = jnp.dot(h2, wu_ref[:, c:c + FFN_CHUNK], preferred_element_type=F32) * scale
        act_scr[:, c:c + FFN_CHUNK] = (gate * jax.nn.sigmoid(gate) * up).astype(BF16)
    down = jnp.dot(act_scr[...], wd_ref[...], preferred_element_type=F32)
    out_ref[...] = _rms(x1 + down, gf_ref[...])


def _tail(oa, ret, rest, x2d, wa, wr, wo, wg, wu, wd, gf):
    t = x2d.shape[0]
    tm = TM_OUT
    gs = D_MODEL // LANES
    full = lambda shape: pl.BlockSpec(shape, lambda i: (0,) * len(shape), pipeline_mode=pl.Buffered(1))
    vec = pl.BlockSpec((1, D_MODEL), lambda i: (0, 0))
    row = pl.BlockSpec((tm, D_MODEL), lambda i: (i, 0))
    in_specs = [
        pl.BlockSpec((PAIRS, tm, LANES), lambda i: (0, i, 0)),
        pl.BlockSpec((gs, tm, LANES), lambda i: (0, i, 0)),
        pl.BlockSpec((gs, tm, LANES), lambda i: (REST_GR // gs, i, 0)),
        pl.BlockSpec((gs, tm, LANES), lambda i: (REST_GA // gs, i, 0)),
        pl.BlockSpec((gs, tm, LANES), lambda i: (REST_GRT // gs, i, 0)),
        row,
        full((GROUP_WIDTH, D_MODEL)), full((RET_V_WIDTH, D_MODEL)), full((D_MODEL, D_MODEL)),
        full((D_MODEL, FFN_HIDDEN)), full((D_MODEL, FFN_HIDDEN)), full((FFN_HIDDEN, D_MODEL)), vec,
    ]
    return pl.pallas_call(
        _tail_kernel,
        grid=(t // tm,),
        in_specs=in_specs,
        out_specs=row,
        out_shape=jax.ShapeDtypeStruct((t, D_MODEL), F32),
        scratch_shapes=[pltpu.VMEM((tm, RET_V_WIDTH), BF16), pltpu.VMEM((tm, FFN_HIDDEN), BF16)],
        compiler_params=pltpu.CompilerParams(
            dimension_semantics=("arbitrary",), vmem_limit_bytes=V7X_VMEM_LIMIT_BYTES),
        name="merge_ffn",
    )(oa, ret, rest, rest, rest, x2d, wa, wr, wo, wg, wu, wd, gf)


def _row_gain(g):
    return jnp.broadcast_to(g[:, None], (g.shape[0], LANES))


def _position_tables(seq):
    pos = np.arange(seq, dtype=np.float64)
    inv = ROPE_THETA ** (-np.arange(0, HEAD_DIM, 2, dtype=np.float64) / HEAD_DIM)
    ang = pos[:, None] * inv[None, :]
    c, s = np.cos(ang), np.sin(ang)
    cos_a = np.concatenate([c, c, c, c], axis=1)
    sin_a = np.concatenate([-s, s, -s, s], axis=1)
    base = 1.0 / (ROPE_THETA ** np.linspace(0.0, 1.0, RET_KEY_DIM // 2, dtype=np.float64))
    ang_r = pos[:, None] * base[None, :]
    cr, sr = np.cos(ang_r), np.sin(ang_r)
    cos_r = np.repeat(cr, 2, axis=1)
    sin_r = np.repeat(sr, 2, axis=1) * np.tile([-1.0, 1.0], RET_KEY_DIM // 2)[None, :]
    return [jnp.asarray(t.astype(np.float32)) for t in (cos_a, sin_a, cos_r, sin_r)]


def _retention_tables():
    c = RET_CHUNK
    log_g = np.log1p(-(2.0 ** (-5.0 - np.arange(RET_HEADS, dtype=np.float64))))
    idx = np.arange(c, dtype=np.float64)
    diff = idx[:, None] - idx[None, :]
    decay = np.where(diff[None] >= 0, np.exp(-(idx + 1.0)[None, None, :] * log_g[:, None, None]), 0.0)
    zeta = np.exp((c - 1 - idx)[None, :] * log_g[:, None])
    xi = np.exp((idx + 1.0)[None, :] * log_g[:, None])
    chunk_decay = np.exp(c * log_g)
    bcast = lambda v: np.broadcast_to(v[:, :, None], (RET_HEADS, c, LANES))
    cd = np.broadcast_to(chunk_decay[:, None, None], (RET_HEADS, 1, LANES))
    return tuple(jnp.asarray(np.ascontiguousarray(t).astype(np.float32))
                 for t in (decay, bcast(zeta), bcast(xi), cd))


def kernel(x, norm_mix_g, w_in, w_out_attn, w_out_ret, w_out, norm_ffn_g, w_ffn_gate, w_ffn_up,
           w_ffn_down, norm_final_g):
    batch, seq, _ = x.shape
    assert w_in.shape[0] == 1, "single-layer problem"
    assert seq % TM_IN == 0 and TM_IN % (DILATIONS[-1] * BF16_SUBLANES) == 0
    x2d = x.reshape(batch * seq, D_MODEL)
    a0, a1, a2, rest = _in_projection(
        x2d, w_in[0], _row_gain(norm_mix_g[0]), _position_tables(seq), batch, seq)
    tail_weights = [w_out_attn[0], w_out_ret[0], w_out[0], w_ffn_gate[0], w_ffn_up[0], w_ffn_down[0]]
    oa, (wa, wr, wo, wg, wu, wd) = _dilated_attention(
        a0, a1, a2, tail_weights, _row_gain(norm_ffn_g[0]), batch, seq)
    ret = _retention(rest, _retention_tables(), batch, seq)
    out = _tail(oa, ret, rest, x2d, wa, wr, wo, wg, wu, wd, norm_final_g[None, :])
    return out.reshape(batch, seq, D_MODEL)
```
